```python
import math
import jax, jax.numpy as jnp
from jax import lax
import numpy as np

D_MODEL = 2048
BATCH = 4
SEQ = 8192
DEPTH = 1

D_BRANCH = D_MODEL
CONV_WIDTH = 3
HEAD_SIZE = 64
N_RWKV_HEADS = D_BRANCH // HEAD_SIZE
DECAY_LORA = max(32, int(round(1.8 * math.sqrt(D_MODEL) / 32)) * 32)
ICLR_LORA = max(32, int(round(1.8 * math.sqrt(D_MODEL) / 32)) * 32)
GATE_LORA = max(32, int(round(0.6 * D_MODEL ** 0.8 / 32)) * 32)
D_FF = -(-8 * D_MODEL // (3 * 256)) * 256
ALPHA = (2.0 * DEPTH) ** 0.25
BETA = (8.0 * DEPTH) ** -0.25
LN_EPS = 1e-5
GN_EPS = 64e-5

CONV_LO = 0
RWKV_LO = 3 * D_BRANCH
RWKV_HI = RWKV_LO + 3 * D_BRANCH + DECAY_LORA + ICLR_LORA + GATE_LORA
D_IN_PROJ = RWKV_HI + 2 * D_BRANCH
D_SHIFT = RWKV_HI - RWKV_LO
RWKV_SPLITS = (D_BRANCH, 2 * D_BRANCH, 3 * D_BRANCH,
               3 * D_BRANCH + DECAY_LORA, 3 * D_BRANCH + DECAY_LORA + ICLR_LORA)

kernel_name = "hybrid_shortconv_rwkv7_gated_deepnorm"


def _layer_norm(x, g, b, eps):
    xf = x.astype(jnp.float32)
    mu = jnp.mean(xf, axis=-1, keepdims=True)
    var = jnp.mean(jnp.square(xf - mu), axis=-1, keepdims=True)
    y = (xf - mu) * lax.rsqrt(var + eps) * g.astype(jnp.float32) + b.astype(jnp.float32)
    return y.astype(x.dtype)


def _token_shift(z, mu):
    z_prev = jnp.pad(z, ((0, 0), (1, 0), (0, 0)))[:, :-1]
    return z + mu * (z_prev - z)


def _causal_depthwise_conv(u, w):
    return lax.conv_general_dilated(
        u, w[:, None, :].astype(u.dtype), window_strides=(1,),
        padding=[(CONV_WIDTH - 1, 0)], dimension_numbers=("NWC", "WIO", "NWC"),
        feature_group_count=u.shape[-1])


def _wkv7(r, w, k, v, a, b):
    bsz, _, h, n = r.shape

    def step(S, inp):
        r_t, w_t, k_t, v_t, a_t, b_t = inp
        sa = jnp.einsum("bhvk,bhk->bhv", S, a_t)
        S = (S * w_t[:, :, None, :] + sa[..., None] * b_t[:, :, None, :]
             + v_t[..., None] * k_t[:, :, None, :])
        y_t = jnp.einsum("bhvk,bhk->bhv", S, r_t)
        return S, y_t

    xs = tuple(jnp.moveaxis(t, 1, 0) for t in (r, w, k, v, a, b))
    S0 = jnp.zeros((bsz, h, n, n), jnp.float32)
    _, y = lax.scan(step, S0, xs)
    return jnp.moveaxis(y, 0, 1)


def _rwkv7_branch(z, w0, w_up, a0, a_up, g_up, k_k, k_a, r_k, gn_g, gn_b):
    f32 = jnp.float32
    r, k, v, wd, ad, gd = jnp.split(z, RWKV_SPLITS, axis=-1)
    bsz, t, _ = r.shape

    def heads(u):
        return u.astype(f32).reshape(bsz, t, N_RWKV_HEADS, HEAD_SIZE)

    w_log = -jax.nn.softplus(-(w0 + jnp.tanh(wd) @ w_up).astype(f32)) - 0.5
    decay = jnp.exp(-jnp.exp(w_log))
    a = jax.nn.sigmoid((a0 + ad @ a_up).astype(f32))
    g = jax.nn.sigmoid(gd) @ g_up
    kk = heads(k * k_k)
    kk = kk / jnp.maximum(jnp.sqrt(jnp.sum(kk * kk, axis=-1, keepdims=True)), 1e-12)
    k_mod = heads(k.astype(f32) * (1.0 + (a - 1.0) * k_a.astype(f32)))
    rh, vh = heads(r), heads(v)
    y = _wkv7(rh, heads(decay), k_mod, vh, -kk, kk * heads(a))
    mu = jnp.mean(y, axis=-1, keepdims=True)
    var = jnp.mean(jnp.square(y - mu), axis=-1, keepdims=True)
    yn = ((y - mu) * lax.rsqrt(var + GN_EPS)).reshape(bsz, t, D_BRANCH)
    yn = yn * gn_g.astype(f32) + gn_b.astype(f32)
    bonus = jnp.sum(rh * k_mod * r_k.astype(f32), axis=-1, keepdims=True) * vh
    out = (yn + bonus.reshape(bsz, t, D_BRANCH)) * g.astype(f32)
    return out.astype(z.dtype)


def setup_inputs(seed: int = 0) -> dict:
    key = jax.random.key(seed)
    ks = jax.random.split(key, 22)
    nrm = jax.random.normal
    L, D = DEPTH, D_MODEL
    f = jnp.float32
    return {
        "x": nrm(ks[0], (BATCH, SEQ, D), f),
        "w_in": nrm(ks[1], (L, D, D_IN_PROJ), f) * D ** -0.5,
        "shift_mu": jax.random.uniform(ks[2], (L, D_SHIFT), f),
        "conv_w": nrm(ks[3], (L, CONV_WIDTH, D_BRANCH), f) * CONV_WIDTH ** -0.5,
        "w0": jax.random.uniform(ks[4], (L, D_BRANCH), f, -6.0, 1.0),
        "w_up": nrm(ks[5], (L, DECAY_LORA, D_BRANCH), f) * 0.5 * DECAY_LORA ** -0.5,
        "a0": nrm(ks[6], (L, D_BRANCH), f) * 0.5,
        "a_up": nrm(ks[7], (L, ICLR_LORA, D_BRANCH), f) * 0.5 * ICLR_LORA ** -0.5,
        "g_up": nrm(ks[8], (L, GATE_LORA, D_BRANCH), f) * GATE_LORA ** -0.5,
        "k_k": 0.85 + 0.05 * nrm(ks[9], (L, D_BRANCH), f),
        "k_a": 1.0 + 0.05 * nrm(ks[10], (L, D_BRANCH), f),
        "r_k": 0.1 * nrm(ks[11], (L, N_RWKV_HEADS, HEAD_SIZE), f),
        "gn_g": 1.0 + 0.02 * nrm(ks[12], (L, D_BRANCH), f),
        "gn_b": 0.02 * nrm(ks[13], (L, D_BRANCH), f),
        "w_o": nrm(ks[14], (L, D_BRANCH, D), f) * D_BRANCH ** -0.5 * BETA,
        "ln1_g": 1.0 + 0.02 * nrm(ks[15], (L, D), f),
        "ln1_b": 0.02 * nrm(ks[16], (L, D), f),
        "w_gu": nrm(ks[17], (L, D, 2 * D_FF), f) * D ** -0.5,
        "w_down": nrm(ks[18], (L, D_FF, D), f) * D_FF ** -0.5 * BETA,
        "ln2_g": 1.0 + 0.02 * nrm(ks[19], (L, D), f),
        "ln2_b": 0.02 * nrm(ks[20], (L, D), f),
    }


def reference(x, w_in, shift_mu, conv_w, w0, w_up, a0, a_up, g_up, k_k, k_a, r_k,
              gn_g, gn_b, w_o, ln1_g, ln1_b, w_gu, w_down, ln2_g, ln2_b):
    for i in range(DEPTH):
        proj = x @ w_in[i]
        c_b = proj[..., CONV_LO:CONV_LO + D_BRANCH]
        c_c = proj[..., CONV_LO + D_BRANCH:CONV_LO + 2 * D_BRANCH]
        c_h = proj[..., CONV_LO + 2 * D_BRANCH:RWKV_LO]
        y_conv = c_b * _causal_depthwise_conv(c_c * c_h, conv_w[i])

        z = _token_shift(proj[..., RWKV_LO:RWKV_HI], shift_mu[i])
        y_rwkv = _rwkv7_branch(z, w0[i], w_up[i], a0[i], a_up[i], g_up[i], k_k[i], k_a[i],
                               r_k[i], gn_g[i], gn_b[i])

        gate_conv = proj[..., RWKV_HI:RWKV_HI + D_BRANCH]
        gate_rwkv = proj[..., RWKV_HI + D_BRANCH:]
        merged = jax.nn.sigmoid(gate_conv) * y_conv + jax.nn.sigmoid(gate_rwkv) * y_rwkv
        x = _layer_norm(ALPHA * x + merged @ w_o[i], ln1_g[i], ln1_b[i], LN_EPS)

        gate, up = jnp.split(x @ w_gu[i], 2, axis=-1)
        x = _layer_norm(ALPHA * x + (jax.nn.silu(gate) * up) @ w_down[i],
                        ln2_g[i], ln2_b[i], LN_EPS)
    return x
```

```python
import functools
import math

import jax
import jax.numpy as jnp
from jax import lax
from jax.experimental import pallas as pl
from jax.experimental.pallas import tpu as pltpu

HEAD_SIZE = 64
LOG2_HEAD = 6
LOG_DECAY_SCALE = -math.exp(-0.5)
LN_EPS = 1e-5
GN_EPS = 64e-5
LANES = 128
SUBLANES = 8
CHUNK = 64
VMEM_LIMIT_BYTES = 56 * 1024 * 1024

F32 = jnp.float32
BF16 = jnp.bfloat16


def _mm(a, b):
    return jnp.dot(a.astype(BF16), b.astype(BF16), preferred_element_type=F32)


def _mm_nt(a, b):
    return lax.dot_general(a.astype(BF16), b.astype(BF16), (((1,), (1,)), ((), ())),
                           preferred_element_type=F32)


def _split2(a):
    hi = a.astype(BF16)
    lo = (a - hi.astype(F32)).astype(BF16)
    return hi, lo


def _split3(a):
    hi = a.astype(BF16)
    r1 = a - hi.astype(F32)
    mid = r1.astype(BF16)
    lo = (r1 - mid.astype(F32)).astype(BF16)
    return hi, mid, lo


def _mm_hp(a, b):
    ah, al = _split2(a)
    bh, bl = _split2(b)
    d = functools.partial(jnp.dot, preferred_element_type=F32)
    return d(ah, bh) + (d(ah, bl) + d(al, bh))


def _mm_exact_lhs(a_exact, b):
    d = functools.partial(jnp.dot, preferred_element_type=F32)
    h, m, l = _split3(b)
    return d(a_exact, h) + (d(a_exact, m) + d(a_exact, l))


def _mm_t_exact_rhs(a, b_exact):
    d = functools.partial(jnp.dot, preferred_element_type=F32)
    h, m, l = _split3(a.T)
    return d(h, b_exact) + (d(m, b_exact) + d(l, b_exact))


def _iota_shr(shape, dim, log2_div):
    return lax.shift_right_logical(lax.broadcasted_iota(jnp.int32, shape, dim), log2_div)


def _head_ones(n):
    r = _iota_shr((n, n), 0, LOG2_HEAD)
    c = _iota_shr((n, n), 1, LOG2_HEAD)
    return jnp.where(r == c, 1.0, 0.0).astype(BF16)


def _head_sum(q, ones):
    d = functools.partial(jnp.dot, preferred_element_type=F32)
    h, l = _split2(q)
    return d(h, ones) + d(l, ones)


N_GROUPS = 8
N_CARRY = 4
LORA_PAD = 512


def _mixer_in_kernel(x_ref, w_ref, wl_ref, wup_ref, p_ref, mul_ref,
                     r_ref, lw_ref, k_ref, v_ref, kk_ref, a_ref, c1_ref, og_ref,
                     lora_ref, carry_ref, carry_l_ref, *, tm, tn):
    i = pl.program_id(1)
    j = pl.program_id(2)
    first = i == 0
    xt = x_ref[0]
    row = lax.broadcasted_iota(jnp.int32, (tm, tn), 0)

    def prev1(p, c8):
        return jnp.where(row == 0, c8[SUBLANES - 1:SUBLANES, :], pltpu.roll(p, 1, 0))

    def prev2(p, c8):
        rolled = pltpu.roll(p, 2, 0)
        rolled = jnp.where(row == 1, c8[SUBLANES - 1:SUBLANES, :], rolled)
        return jnp.where(row == 0, c8[SUBLANES - 2:SUBLANES - 1, :], rolled)

    @pl.when(j == 0)
    def _():
        pl_ = jnp.dot(xt, wl_ref[...], preferred_element_type=F32)
        cl = jnp.where(first, 0.0, carry_l_ref[...])
        rowl = lax.broadcasted_iota(jnp.int32, (tm, LORA_PAD), 0)
        prev = jnp.where(rowl == 0, cl[SUBLANES - 1:SUBLANES, :], pltpu.roll(pl_, 1, 0))
        carry_l_ref[...] = pl_[tm - SUBLANES:, :]
        z = pl_ + mul_ref[...] * (prev - pl_)
        lora_ref[:, 0:LANES] = jnp.tanh(z[:, 0:LANES]).astype(BF16)
        lora_ref[:, LANES:2 * LANES] = z[:, LANES:2 * LANES].astype(BF16)
        lora_ref[:, 2 * LANES:] = jax.nn.sigmoid(z[:, 2 * LANES:]).astype(BF16)

    proj = jnp.dot(xt, w_ref[0], preferred_element_type=F32)
    p_cb, p_cc, p_ch, p_gc, p_r, p_k, p_v, p_gr = (
        proj[:, g * tn:(g + 1) * tn] for g in range(N_GROUPS))
    prm = p_ref[...]
    mu_r, mu_k, mu_v = prm[0:1], prm[1:2], prm[2:3]
    cw0, cw1, cw2 = prm[3:4], prm[4:5], prm[5:6]
    w0, a0, k_k, k_a, r_k = prm[6:7], prm[7:8], prm[8:9], prm[9:10], prm[10:11]

    carry = [jnp.where(first, 0.0, carry_ref[j, q]) for q in range(N_CARRY)]

    u = p_cc * p_ch
    conv = cw2 * u + cw1 * prev1(u, carry[0]) + cw0 * prev2(u, carry[0])
    y_conv = jax.nn.sigmoid(p_gc) * (p_cb * conv)

    r = p_r + mu_r * (prev1(p_r, carry[1]) - p_r)
    k = p_k + mu_k * (prev1(p_k, carry[2]) - p_k)
    v = p_v + mu_v * (prev1(p_v, carry[3]) - p_v)
    for q, val in enumerate((u, p_r, p_k, p_v)):
        carry_ref[j, q] = val[tm - SUBLANES:, :]

    wup = wup_ref[0]
    d = functools.partial(jnp.dot, preferred_element_type=F32)
    w_pre = w0 + d(lora_ref[:, 0:LANES], wup[0:LANES])
    a_pre = a0 + d(lora_ref[:, LANES:2 * LANES], wup[LANES:2 * LANES])
    g = d(lora_ref[:, 2 * LANES:], wup[2 * LANES:])
    lw = LOG_DECAY_SCALE * jax.nn.sigmoid(w_pre)
    a = jax.nn.sigmoid(a_pre)

    ones = _head_ones(tn)
    kraw = k * k_k
    ss = _head_sum(kraw * kraw, ones)
    kk = kraw / jnp.maximum(jnp.sqrt(ss), 1e-12)
    k_mod = k * (1.0 + (a - 1.0) * k_a)
    bonus = _head_sum(r * k_mod * r_k, ones) * v
    og = g * jax.nn.sigmoid(p_gr)

    r_ref[0] = r
    lw_ref[0] = lw
    k_ref[0] = k_mod
    v_ref[0] = v
    kk_ref[0] = kk
    a_ref[0] = a
    c1_ref[0] = y_conv + og * bonus
    og_ref[0] = og


def _mixer_in(x_bf, w_t, wl, wup_t, prm, mu_l, *, tm, tn):
    bsz, t, dm = x_bf.shape
    nj = dm // tn
    grid = (bsz, t // tm, nj)
    out_sds = jax.ShapeDtypeStruct((bsz, t, dm), F32)
    out_spec = pl.BlockSpec((1, tm, tn), lambda b, i, j: (b, i, j))
    return pl.pallas_call(
        functools.partial(_mixer_in_kernel, tm=tm, tn=tn),
        grid=grid,
        in_specs=[
            pl.BlockSpec((1, tm, dm), lambda b, i, j: (b, i, 0)),
            pl.BlockSpec((1, dm, N_GROUPS * tn), lambda b, i, j: (j, 0, 0)),
            pl.BlockSpec((dm, LORA_PAD), lambda b, i, j: (0, 0)),
            pl.BlockSpec((1, LORA_PAD, tn), lambda b, i, j: (j, 0, 0)),
            pl.BlockSpec((16, tn), lambda b, i, j: (0, j)),
            pl.BlockSpec((1, LORA_PAD), lambda b, i, j: (0, 0)),
        ],
        out_specs=[out_spec] * 8,
        out_shape=[out_sds] * 8,
        scratch_shapes=[
            pltpu.VMEM((tm, LORA_PAD), BF16),
            pltpu.VMEM((nj, N_CARRY, SUBLANES, tn), F32),
            pltpu.VMEM((SUBLANES, LORA_PAD), F32),
        ],
        compiler_params=pltpu.CompilerParams(
            dimension_semantics=("arbitrary", "arbitrary", "arbitrary"),
            vmem_limit_bytes=VMEM_LIMIT_BYTES),
        name="mixer_in",
    )(x_bf, w_t, wl, wup_t, prm, mu_l)


def _wkv_pair(r, lw, k, v, kk, a, z, masks):
    (tri_incl, ones_cl, m0, m0_2, strict_w, incl_w, eye_w, bd, blk_w) = masks
    c = CHUNK

    def stack(xw):
        m = m0 if xw.shape[1] == LANES else m0_2
        return jnp.concatenate([jnp.where(m, xw, 0.0), jnp.where(m, 0.0, xw)], axis=0)

    def bdiag(xw):
        return jnp.where(bd, jnp.concatenate([xw, xw], axis=0), 0.0)

    cum = _mm_exact_lhs(tri_incl, lw)
    cum_c = cum[c - 1:c, :]
    e_in = jnp.exp(cum)
    e_ex = jnp.exp(cum - lw)
    e_inv = jnp.exp(-cum)
    e_rem = jnp.exp(cum_c - cum)
    beta = kk * a
    rt = r * e_in
    at = -kk * e_ex
    kt = k * e_inv
    bt = beta * e_inv
    kh = k * e_rem
    bh = beta * e_rem

    lhs = jnp.concatenate([at, rt], axis=0)
    rhs = jnp.concatenate([stack(bt), stack(kt)], axis=0)
    s = _mm_nt(lhs, rhs)
    lab = jnp.where(strict_w, s[0:c, 0:2 * c], 0.0)
    lak = jnp.where(strict_w, s[0:c, 2 * c:], 0.0)
    urb = jnp.where(incl_w, s[c:, 0:2 * c], 0.0)
    urk = jnp.where(incl_w, s[c:, 2 * c:], 0.0)

    l8 = jnp.where(blk_w[0], lab, 0.0)
    l8_2 = _mm_hp(l8, bdiag(l8))
    l8_4 = _mm_hp(l8_2, bdiag(l8_2))
    tw = _mm_hp(eye_w + l8, bdiag(eye_w + l8_2))
    tw = _mm_hp(tw, bdiag(eye_w + l8_4))
    for lvl in range(len(blk_w)):
        outside = jnp.logical_not(blk_w[lvl])
        join = outside if lvl + 1 == len(blk_w) else jnp.logical_and(blk_w[lvl + 1], outside)
        coff = jnp.where(join, lab, 0.0)
        tw = tw + _mm_hp(_mm_hp(tw, bdiag(coff)), bdiag(tw))

    nv = _mm(lak, stack(v))
    ah_d = _mm(tw, stack(jnp.concatenate([at, nv], axis=1)))
    ah, dm = ah_d[:, 0:LANES], ah_d[:, LANES:]
    u2 = _mm(urb, stack(ah_d))
    rh = rt + u2[:, 0:LANES]
    e = u2[:, LANES:] + _mm(urk, stack(v))
    g_f = _mm(bh.T, ah_d)
    gl = jnp.where(bd, g_f[:, 0:LANES], 0.0)
    f = jnp.where(bd, g_f[:, LANES:] + _mm(kh.T, v), 0.0)
    cum_c_mat = _mm_t_exact_rhs(lw, ones_cl)
    y = _mm(rh, z) + e
    z_new = jnp.exp(cum_c_mat) * z + _mm(gl, z) + f
    return y, z_new


def _wkv_masks():
    c = CHUNK
    assert c == HEAD_SIZE and 2 * c == LANES
    t = lax.broadcasted_iota(jnp.int32, (c, 2 * c), 0)
    i = jnp.bitwise_and(lax.broadcasted_iota(jnp.int32, (c, 2 * c), 1), c - 1)
    strict_w = i < t
    incl_w = i <= t
    eye_w = jnp.where(i == t, 1.0, 0.0).astype(F32)
    blk_w = [lax.shift_right_logical(t, s) == lax.shift_right_logical(i, s) for s in (3, 4, 5)]
    rr = lax.broadcasted_iota(jnp.int32, (c, c), 0)
    cc = lax.broadcasted_iota(jnp.int32, (c, c), 1)
    tri_incl = jnp.where(cc <= rr, 1.0, 0.0).astype(BF16)
    ones_cl = jnp.ones((c, LANES), BF16)
    m0 = lax.broadcasted_iota(jnp.int32, (c, LANES), 1) < HEAD_SIZE
    m0_2 = jnp.bitwise_and(lax.broadcasted_iota(jnp.int32, (c, 2 * LANES), 1),
                           LANES - 1) < HEAD_SIZE
    bd = _iota_shr((2 * c, 2 * c), 0, LOG2_HEAD) == _iota_shr((2 * c, 2 * c), 1, LOG2_HEAD)
    return (tri_incl, ones_cl, m0, m0_2, strict_w, incl_w, eye_w, bd, blk_w)


def _wkv_kernel(r_ref, lw_ref, k_ref, v_ref, kk_ref, a_ref, c1_ref, og_ref, gn_ref,
                o_ref, z_ref, *, n_chunks, n_pairs):
    @pl.when(pl.program_id(2) == 0)
    def _():
        z_ref[...] = jnp.zeros_like(z_ref)

    masks = _wkv_masks()
    ones = _head_ones(LANES)
    inv_n = 1.0 / HEAD_SIZE
    for p in range(n_pairs):
        ls = slice(p * LANES, (p + 1) * LANES)
        z = z_ref[p]
        gn_g = gn_ref[0:1, ls]
        gn_b = gn_ref[1:2, ls]
        for ci in range(n_chunks):
            ts = slice(ci * CHUNK, (ci + 1) * CHUNK)
            y, z = _wkv_pair(r_ref[0, ts, ls], lw_ref[0, ts, ls], k_ref[0, ts, ls],
                             v_ref[0, ts, ls], kk_ref[0, ts, ls], a_ref[0, ts, ls], z, masks)
            mu = _head_sum(y, ones) * inv_n
            yc = y - mu
            var = _head_sum(yc * yc, ones) * inv_n
            yn = yc * lax.rsqrt(var + GN_EPS) * gn_g + gn_b
            o_ref[0, ts, ls] = (c1_ref[0, ts, ls] + og_ref[0, ts, ls] * yn).astype(o_ref.dtype)
        z_ref[p] = z


def _wkv(r, lw, k, v, kk, a, c1, og, gn, *, tb, tw):
    bsz, t, dm = r.shape
    n_pairs = tw // LANES
    grid = (bsz, dm // tw, t // tb)
    spec = pl.BlockSpec((1, tb, tw), lambda b, h, i: (b, i, h))
    return pl.pallas_call(
        functools.partial(_wkv_kernel, n_chunks=tb // CHUNK, n_pairs=n_pairs),
        grid=grid,
        in_specs=[spec] * 8 + [pl.BlockSpec((SUBLANES, tw), lambda b, h, i: (0, h))],
        out_specs=spec,
        out_shape=jax.ShapeDtypeStruct((bsz, t, dm), BF16),
        scratch_shapes=[pltpu.VMEM((n_pairs, LANES, LANES), F32)],
        compiler_params=pltpu.CompilerParams(
            dimension_semantics=("arbitrary", "arbitrary", "arbitrary"),
            vmem_limit_bytes=VMEM_LIMIT_BYTES),
        name="wkv",
    )(r, lw, k, v, kk, a, c1, og, gn)


def _layer_norm(h, g, b):
    mu = jnp.mean(h, axis=-1, keepdims=True)
    hc = h - mu
    var = jnp.mean(hc * hc, axis=-1, keepdims=True)
    return hc * lax.rsqrt(var + LN_EPS) * g + b


def _mixer_out_kernel(m_ref, x_ref, wo_ref, ln_ref, o_ref, obf_ref, *, alpha):
    h = alpha * x_ref[...] + jnp.dot(m_ref[...], wo_ref[...], preferred_element_type=F32)
    y = _layer_norm(h, ln_ref[0:1, :], ln_ref[1:2, :])
    o_ref[...] = y
    obf_ref[...] = y.astype(BF16)


def _mixer_out(merged, x2d, wo, ln, *, alpha, tm):
    m, dm = x2d.shape
    return pl.pallas_call(
        functools.partial(_mixer_out_kernel, alpha=alpha),
        grid=(m // tm,),
        in_specs=[
            pl.BlockSpec((tm, dm), lambda i: (i, 0)),
            pl.BlockSpec((tm, dm), lambda i: (i, 0)),
            pl.BlockSpec((dm, dm), lambda i: (0, 0)),
            pl.BlockSpec((SUBLANES, dm), lambda i: (0, 0)),
        ],
        out_specs=[pl.BlockSpec((tm, dm), lambda i: (i, 0))] * 2,
        out_shape=[jax.ShapeDtypeStruct((m, dm), F32), jax.ShapeDtypeStruct((m, dm), BF16)],
        compiler_params=pltpu.CompilerParams(
            dimension_semantics=("arbitrary",), vmem_limit_bytes=VMEM_LIMIT_BYTES),
        name="mixer_out",
    )(merged, x2d, wo, ln)


def _ffn_kernel(xb_ref, x_ref, wg_ref, wu_ref, wd_ref, ln_ref, o_ref, acc_ref, *, alpha):
    f = pl.program_id(1)
    xb = xb_ref[...]
    gate = jnp.dot(xb, wg_ref[...], preferred_element_type=F32)
    up = jnp.dot(xb, wu_ref[...], preferred_element_type=F32)
    hid = (gate * jax.nn.sigmoid(gate) * up).astype(BF16)
    part = jnp.dot(hid, wd_ref[...], preferred_element_type=F32)

    @pl.when(f == 0)
    def _():
        acc_ref[...] = part

    @pl.when(f > 0)
    def _():
        acc_ref[...] += part

    @pl.when(f == pl.num_programs(1) - 1)
    def _():
        h = alpha * x_ref[...] + acc_ref[...]
        o_ref[...] = _layer_norm(h, ln_ref[0:1, :], ln_ref[1:2, :])


def _ffn(x_bf, x2d, w_gu, w_down, ln, *, alpha, tm, tf):
    m, dm = x2d.shape
    dff = w_down.shape[0]
    nf = dff // tf
    return pl.pallas_call(
        functools.partial(_ffn_kernel, alpha=alpha),
        grid=(m // tm, nf),
        in_specs=[
            pl.BlockSpec((tm, dm), lambda i, f: (i, 0)),
            pl.BlockSpec((tm, dm), lambda i, f: (i, 0)),
            pl.BlockSpec((dm, tf), lambda i, f: (0, f)),
            pl.BlockSpec((dm, tf), lambda i, f: (0, nf + f)),
            pl.BlockSpec((tf, dm), lambda i, f: (f, 0)),
            pl.BlockSpec((SUBLANES, dm), lambda i, f: (0, 0)),
        ],
        out_specs=pl.BlockSpec((tm, dm), lambda i, f: (i, 0)),
        out_shape=jax.ShapeDtypeStruct((m, dm), F32),
        scratch_shapes=[pltpu.VMEM((tm, dm), F32)],
        compiler_params=pltpu.CompilerParams(
            dimension_semantics=("arbitrary", "arbitrary"), vmem_limit_bytes=VMEM_LIMIT_BYTES),
        name="ffn",
    )(x_bf, x2d, w_gu, w_gu, w_down, ln)


def _pad_rows(a, n):
    return jnp.pad(a, ((0, n - a.shape[0]), (0, 0)))


def _tile_cols(w, tn):
    k, n = w.shape[-2], w.shape[-1]
    return jnp.moveaxis(w.reshape(w.shape[:-1] + (n // tn, tn)), -2, 0)


def _pick(n, candidates):
    for c in candidates:
        if n % c == 0:
            return c
    raise ValueError(f"no tile size for extent {n}")


def kernel(x, w_in, shift_mu, conv_w, w0, w_up, a0, a_up, g_up, k_k, k_a, r_k,
           gn_g, gn_b, w_o, ln1_g, ln1_b, w_gu, w_down, ln2_g, ln2_b):
    bsz, t, dm = x.shape
    depth = w_in.shape[0]
    lw_n, la_n, lg_n = w_up.shape[1], a_up.shape[1], g_up.shape[1]
    assert dm % (2 * LANES) == 0 and t % CHUNK == 0
    assert lw_n <= LANES and la_n <= LANES and lg_n <= 2 * LANES
    assert w_in.shape[2] == 8 * dm + lw_n + la_n + lg_n
    alpha = (2.0 * depth) ** 0.25
    tn = LANES
    tm_in = _pick(t, (512, 256, 128, 64))
    tb = _pick(t, (128, 64))
    tw = 2 * LANES
    m = bsz * t
    tm = _pick(m, (512, 256, 128, 64))
    tf = _pick(w_down.shape[1], (512, 256, 128))

    for l in range(depth):
        wi = w_in[l]
        c0 = 6 * dm
        cols = [wi[:, 0:dm], wi[:, dm:2 * dm], wi[:, 2 * dm:3 * dm],
                wi[:, c0 + lw_n + la_n + lg_n:c0 + lw_n + la_n + lg_n + dm],
                wi[:, 3 * dm:4 * dm], wi[:, 4 * dm:5 * dm], wi[:, 5 * dm:6 * dm],
                wi[:, c0 + lw_n + la_n + lg_n + dm:]]
        w_t = jnp.concatenate([_tile_cols(c.astype(BF16), tn) for c in cols], axis=-1)

        def lane_pad(a, n):
            return jnp.pad(a, ((0, 0), (0, n - a.shape[1])))

        wl = jnp.concatenate([
            lane_pad(wi[:, c0:c0 + lw_n], LANES),
            lane_pad(wi[:, c0 + lw_n:c0 + lw_n + la_n], LANES),
            lane_pad(wi[:, c0 + lw_n + la_n:c0 + lw_n + la_n + lg_n], 2 * LANES)],
            axis=1).astype(BF16)
        mu = shift_mu[l]
        s0 = 3 * dm
        mu_l = jnp.concatenate([
            jnp.pad(mu[s0:s0 + lw_n], (0, LANES - lw_n)),
            jnp.pad(mu[s0 + lw_n:s0 + lw_n + la_n], (0, LANES - la_n)),
            jnp.pad(mu[s0 + lw_n + la_n:], (0, 2 * LANES - lg_n))])[None, :]
        wup = jnp.concatenate([_pad_rows(w_up[l], LANES), _pad_rows(a_up[l], LANES),
                               _pad_rows(g_up[l], 2 * LANES)], axis=0).astype(BF16)
        wup_t = _tile_cols(wup, tn)
        prm = _pad_rows(jnp.stack([
            mu[0:dm], mu[dm:2 * dm], mu[2 * dm:3 * dm],
            conv_w[l, 0], conv_w[l, 1], conv_w[l, 2],
            w0[l], a0[l], k_k[l], k_a[l], r_k[l].reshape(-1)]), 16)

        r, lw, k, v, kk, a, c1, og = _mixer_in(
            x.astype(BF16), w_t, wl, wup_t, prm, mu_l, tm=tm_in, tn=tn)
        gn = _pad_rows(jnp.stack([gn_g[l], gn_b[l]]), SUBLANES)
        merged = _wkv(r, lw, k, v, kk, a, c1, og, gn, tb=tb, tw=tw)
        ln1 = _pad_rows(jnp.stack([ln1_g[l], ln1_b[l]]), SUBLANES)
        x1, x1_bf = _mixer_out(merged.reshape(m, dm), x.reshape(m, dm), w_o[l].astype(BF16),
                               ln1, alpha=alpha, tm=tm)
        ln2 = _pad_rows(jnp.stack([ln2_g[l], ln2_b[l]]), SUBLANES)
        x = _ffn(x1_bf, x1, w_gu[l].astype(BF16), w_down[l].astype(BF16), ln2,
                 alpha=alpha, tm=tm, tf=tf).reshape(bsz, t, dm)
    return x
```

```python
import functools
import math

import jax
import jax.numpy as jnp
from jax import lax
from jax.experimental import pallas as pl
from jax.experimental.pallas import tpu as pltpu

HEAD_SIZE = 64
LOG2_HEAD = 6
LOG_DECAY_SCALE = -math.exp(-0.5)
LN_EPS = 1e-5
GN_EPS = 64e-5
LANES = 128
SUBLANES = 8
CHUNK = 64
VMEM_LIMIT_BYTES = 56 * 1024 * 1024

F32 = jnp.float32
BF16 = jnp.bfloat16


def _mm(a, b):
    return jnp.dot(a.astype(BF16), b.astype(BF16), preferred_element_type=F32)


def _mm_nt(a, b):
    return lax.dot_general(a.astype(BF16), b.astype(BF16), (((1,), (1,)), ((), ())),
                           preferred_element_type=F32)


def _split2(a):
    hi = a.astype(BF16)
    lo = (a - hi.astype(F32)).astype(BF16)
    return hi, lo


def _split3(a):
    hi = a.astype(BF16)
    r1 = a - hi.astype(F32)
    mid = r1.astype(BF16)
    lo = (r1 - mid.astype(F32)).astype(BF16)
    return hi, mid, lo


def _mm_exact_lhs(a_exact, b):
    d = functools.partial(jnp.dot, preferred_element_type=F32)
    h, m, l = _split3(b)
    return d(a_exact, h) + (d(a_exact, m) + d(a_exact, l))


def _iota_shr(shape, dim, log2_div):
    return lax.shift_right_logical(lax.broadcasted_iota(jnp.int32, shape, dim), log2_div)


def _head_ones(n):
    r = _iota_shr((n, n), 0, LOG2_HEAD)
    c = _iota_shr((n, n), 1, LOG2_HEAD)
    return jnp.where(r == c, 1.0, 0.0).astype(BF16)


def _head_sum(q, ones):
    d = functools.partial(jnp.dot, preferred_element_type=F32)
    h, l = _split2(q)
    return d(h, ones) + d(l, ones)


N_GROUPS = 8
N_CARRY = 4
LORA_PAD = 512


def _mixer_in_kernel(x_ref, w_ref, wl_ref, wup_ref, p_ref, mul_ref,
                     r_ref, lw_ref, k_ref, v_ref, kk_ref, a_ref, c1_ref, og_ref,
                     lora_ref, carry_ref, carry_l_ref, *, tm, tn):
    i = pl.program_id(1)
    j = pl.program_id(2)
    first = i == 0
    xt = x_ref[0]
    row = lax.broadcasted_iota(jnp.int32, (tm, tn), 0)

    def prev1(p, c8):
        return jnp.where(row == 0, c8[SUBLANES - 1:SUBLANES, :], pltpu.roll(p, 1, 0))

    def prev2(p, c8):
        rolled = pltpu.roll(p, 2, 0)
        rolled = jnp.where(row == 1, c8[SUBLANES - 1:SUBLANES, :], rolled)
        return jnp.where(row == 0, c8[SUBLANES - 2:SUBLANES - 1, :], rolled)

    @pl.when(j == 0)
    def _():
        pl_ = jnp.dot(xt, wl_ref[...], preferred_element_type=F32)
        cl = jnp.where(first, 0.0, carry_l_ref[...])
        rowl = lax.broadcasted_iota(jnp.int32, (tm, LORA_PAD), 0)
        prev = jnp.where(rowl == 0, cl[SUBLANES - 1:SUBLANES, :], pltpu.roll(pl_, 1, 0))
        carry_l_ref[...] = pl_[tm - SUBLANES:, :]
        z = pl_ + mul_ref[...] * (prev - pl_)
        lora_ref[:, 0:LANES] = jnp.tanh(z[:, 0:LANES]).astype(BF16)
        lora_ref[:, LANES:2 * LANES] = z[:, LANES:2 * LANES].astype(BF16)
        lora_ref[:, 2 * LANES:] = jax.nn.sigmoid(z[:, 2 * LANES:]).astype(BF16)

    proj = jnp.dot(xt, w_ref[0], preferred_element_type=F32)
    p_cb, p_cc, p_ch, p_gc, p_r, p_k, p_v, p_gr = (
        proj[:, g * tn:(g + 1) * tn] for g in range(N_GROUPS))
    prm = p_ref[...]
    mu_r, mu_k, mu_v = prm[0:1], prm[1:2], prm[2:3]
    cw0, cw1, cw2 = prm[3:4], prm[4:5], prm[5:6]
    w0, a0, k_k, k_a, r_k = prm[6:7], prm[7:8], prm[8:9], prm[9:10], prm[10:11]

    carry = [jnp.where(first, 0.0, carry_ref[j, q]) for q in range(N_CARRY)]

    u = p_cc * p_ch
    conv = cw2 * u + cw1 * prev1(u, carry[0]) + cw0 * prev2(u, carry[0])
    y_conv = jax.nn.sigmoid(p_gc) * (p_cb * conv)

    r = p_r + mu_r * (prev1(p_r, carry[1]) - p_r)
    k = p_k + mu_k * (prev1(p_k, carry[2]) - p_k)
    v = p_v + mu_v * (prev1(p_v, carry[3]) - p_v)
    for q, val in enumerate((u, p_r, p_k, p_v)):
        carry_ref[j, q] = val[tm - SUBLANES:, :]

    wup = wup_ref[0]
    d = functools.partial(jnp.dot, preferred_element_type=F32)
    w_pre = w0 + d(lora_ref[:, 0:LANES], wup[0:LANES])
    a_pre = a0 + d(lora_ref[:, LANES:2 * LANES], wup[LANES:2 * LANES])
    g = d(lora_ref[:, 2 * LANES:], wup[2 * LANES:])
    lw = LOG_DECAY_SCALE * jax.nn.sigmoid(w_pre)
    a = jax.nn.sigmoid(a_pre)

    ones = _head_ones(tn)
    kraw = k * k_k
    ss = _head_sum(kraw * kraw, ones)
    kk = kraw / jnp.maximum(jnp.sqrt(ss), 1e-12)
    k_mod = k * (1.0 + (a - 1.0) * k_a)
    bonus = _head_sum(r * k_mod * r_k, ones) * v
    og = g * jax.nn.sigmoid(p_gr)

    r_ref[0] = r
    lw_ref[0] = lw
    k_ref[0] = k_mod
    v_ref[0] = v
    kk_ref[0] = kk
    a_ref[0] = a
    c1_ref[0] = y_conv + og * bonus
    og_ref[0] = og


def _mixer_in(x_bf, w_t, wl, wup_t, prm, mu_l, *, tm, tn):
    bsz, t, dm = x_bf.shape
    nj = dm // tn
    grid = (bsz, t // tm, nj)
    out_sds = jax.ShapeDtypeStruct((bsz, t, dm), F32)
    out_spec = pl.BlockSpec((1, tm, tn), lambda b, i, j: (b, i, j))
    return pl.pallas_call(
        functools.partial(_mixer_in_kernel, tm=tm, tn=tn),
        grid=grid,
        in_specs=[
            pl.BlockSpec((1, tm, dm), lambda b, i, j: (b, i, 0)),
            pl.BlockSpec((1, dm, N_GROUPS * tn), lambda b, i, j: (j, 0, 0)),
            pl.BlockSpec((dm, LORA_PAD), lambda b, i, j: (0, 0)),
            pl.BlockSpec((1, LORA_PAD, tn), lambda b, i, j: (j, 0, 0)),
            pl.BlockSpec((16, tn), lambda b, i, j: (0, j)),
            pl.BlockSpec((1, LORA_PAD), lambda b, i, j: (0, 0)),
        ],
        out_specs=[out_spec] * 8,
        out_shape=[out_sds] * 8,
        scratch_shapes=[
            pltpu.VMEM((tm, LORA_PAD), BF16),
            pltpu.VMEM((nj, N_CARRY, SUBLANES, tn), F32),
            pltpu.VMEM((SUBLANES, LORA_PAD), F32),
        ],
        compiler_params=pltpu.CompilerParams(
            dimension_semantics=("arbitrary", "arbitrary", "arbitrary"),
            vmem_limit_bytes=VMEM_LIMIT_BYTES),
        name="mixer_in",
    )(x_bf, w_t, wl, wup_t, prm, mu_l)


def _wkv_block(r, lw, k, v, kk, a, s0, consts, n_chunks):
    cs = consts
    c = CHUNK
    idx = range(len(r))

    def stack(x):
        xb = x.astype(BF16)
        m0, m1 = (cs["m0"], cs["m1"]) if x.shape[1] == LANES else (cs["m0_2"], cs["m1_2"])
        return jnp.concatenate([xb * m0, xb * m1], axis=0)

    def bdiag(xw):
        xb = xw.astype(BF16)
        return jnp.concatenate([xb, xb], axis=0) * cs["bd_bf"]

    def each(fn, *lists):
        return [fn(*args) for args in zip(*lists)]

    cum = each(lambda x: _mm_exact_lhs(cs["tri"], x), lw)
    beta = each(lambda x, y: x * y, kk, a)
    e_inv = each(lambda x: jnp.exp(-x), cum)
    e_rem = each(lambda x: jnp.exp(x[c - 1:c, :] - x), cum)
    decay_c = each(lambda x: jnp.exp(x[c - 1:c, :]), cum)
    rt = each(lambda x, y: x * jnp.exp(y), r, cum)
    at = each(lambda x, y, z_: -x * jnp.exp(y - z_), kk, cum, lw)
    kt = each(lambda x, y: x * y, k, e_inv)
    bt = each(lambda x, y: x * y, beta, e_inv)
    kh = each(lambda x, y: x * y, k, e_rem)
    bh = each(lambda x, y: x * y, beta, e_rem)

    s = each(lambda a_, r_, b_, k_: _mm_nt(jnp.concatenate([a_, r_], axis=0),
                                           jnp.concatenate([stack(b_), stack(k_)], axis=0)),
             at, rt, bt, kt)
    lab = each(lambda x: jnp.where(cs["strict"], x[0:c, 0:2 * c], 0.0), s)
    lak = each(lambda x: jnp.where(cs["strict"], x[0:c, 2 * c:], 0.0), s)
    urb = each(lambda x: jnp.where(cs["incl"], x[c:, 0:2 * c], 0.0), s)
    urk = each(lambda x: jnp.where(cs["incl"], x[c:, 2 * c:], 0.0), s)

    l8 = each(lambda x: jnp.where(cs["blk8"], x, 0.0), lab)
    l8_2 = each(lambda x: _mm(x, bdiag(x)), l8)
    l8_4 = each(lambda x: _mm(x, bdiag(x)), l8_2)
    tw = each(lambda x, y: _mm(cs["eye"] + x, bdiag(cs["eye"] + y)), l8, l8_2)
    tw = each(lambda x, y: _mm(x, bdiag(cs["eye"] + y)), tw, l8_4)
    for join in cs["joins"]:
        half = each(lambda t_, l_: _mm(t_, bdiag(jnp.where(join, l_, 0.0))), tw, lab)
        tw = each(lambda t_, h_: t_ + _mm(h_, bdiag(t_)), tw, half)

    nv = each(lambda l_, v_: _mm(l_, stack(v_)), lak, v)
    ah_d = each(lambda t_, a_, n_: _mm(t_, stack(jnp.concatenate([a_, n_], axis=1))),
                tw, at, nv)
    u2 = each(lambda u_, x: _mm(u_, stack(x)), urb, ah_d)
    uv = each(lambda u_, v_: _mm(u_, stack(v_)), urk, v)
    rh = each(lambda r_, u_: r_ + u_[:, 0:LANES], rt, u2)
    e = each(lambda u_, w_: u_[:, LANES:] + w_, u2, uv)
    gf = each(lambda x, b_: _mm(jnp.concatenate([x[:, 0:LANES].T, x[:, LANES:].T], axis=0), b_),
              ah_d, bh)
    vk = each(lambda v_, k_: _mm(v_.T, k_), v, kh)
    glt = each(lambda x: jnp.where(cs["bd"], x[0:LANES], 0.0), gf)
    ft = each(lambda x, w_: jnp.where(cs["bd"], x[LANES:] + w_, 0.0), gf, vk)

    ys = [None] * len(r)
    states = list(s0)
    for ci in range(n_chunks):
        for p in range(len(states)):
            n = p * n_chunks + ci
            st = states[p]
            ys[n] = _mm_nt(rh[n], st) + e[n]
            states[p] = decay_c[n] * st + _mm(st, glt[n]) + ft[n]
    return ys, states


def _wkv_consts():
    c = CHUNK
    assert c == HEAD_SIZE and 2 * c == LANES
    t = lax.broadcasted_iota(jnp.int32, (c, 2 * c), 0)
    i = jnp.bitwise_and(lax.broadcasted_iota(jnp.int32, (c, 2 * c), 1), c - 1)
    blk = [lax.shift_right_logical(t, s) == lax.shift_right_logical(i, s) for s in (3, 4, 5)]
    out = [jnp.logical_not(b) for b in blk]
    rr = lax.broadcasted_iota(jnp.int32, (c, c), 0)
    cc = lax.broadcasted_iota(jnp.int32, (c, c), 1)
    def head_lane_masks(width):
        lane = jnp.bitwise_and(lax.broadcasted_iota(jnp.int32, (c, width), 1), LANES - 1)
        return (jnp.where(lane < HEAD_SIZE, 1.0, 0.0).astype(BF16),
                jnp.where(lane < HEAD_SIZE, 0.0, 1.0).astype(BF16))

    m0, m1 = head_lane_masks(LANES)
    m0_2, m1_2 = head_lane_masks(2 * LANES)
    bd = _iota_shr((2 * c, 2 * c), 0, LOG2_HEAD) == _iota_shr((2 * c, 2 * c), 1, LOG2_HEAD)
    return dict(
        strict=i < t, incl=i <= t, eye=jnp.where(i == t, 1.0, 0.0).astype(F32), blk8=blk[0],
        joins=[jnp.logical_and(blk[1], out[0]), jnp.logical_and(blk[2], out[1]), out[2]],
        tri=jnp.where(cc <= rr, 1.0, 0.0).astype(BF16),
        m0=m0, m1=m1, m0_2=m0_2, m1_2=m1_2,
        bd=bd, bd_bf=jnp.where(bd, 1.0, 0.0).astype(BF16))


def _wkv_kernel(r_ref, lw_ref, k_ref, v_ref, kk_ref, a_ref, c1_ref, og_ref, gn_ref,
                o_ref, s_ref, *, n_chunks, n_pairs):
    @pl.when(pl.program_id(2) == 0)
    def _():
        s_ref[...] = jnp.zeros_like(s_ref)

    def chunks(ref):
        return [ref[0, ci * CHUNK:(ci + 1) * CHUNK, p * LANES:(p + 1) * LANES]
                for p in range(n_pairs) for ci in range(n_chunks)]

    ys, states = _wkv_block(chunks(r_ref), chunks(lw_ref), chunks(k_ref), chunks(v_ref),
                            chunks(kk_ref), chunks(a_ref), [s_ref[p] for p in range(n_pairs)],
                            _wkv_consts(), n_chunks)
    ones = _head_ones(LANES)
    inv_n = 1.0 / HEAD_SIZE
    for p in range(n_pairs):
        s_ref[p] = states[p]
        ls = slice(p * LANES, (p + 1) * LANES)
        y = jnp.concatenate(ys[p * n_chunks:(p + 1) * n_chunks], axis=0)
        mu = _head_sum(y, ones) * inv_n
        yc = y - mu
        var = _head_sum(yc * yc, ones) * inv_n
        yn = yc * lax.rsqrt(var + GN_EPS) * gn_ref[0:1, ls] + gn_ref[1:2, ls]
        o_ref[0, :, ls] = (c1_ref[0, :, ls] + og_ref[0, :, ls] * yn).astype(o_ref.dtype)


def _wkv(r, lw, k, v, kk, a, c1, og, gn, *, tb, tw):
    bsz, t, dm = r.shape
    n_pairs = tw // LANES
    grid = (bsz, dm // tw, t // tb)
    spec = pl.BlockSpec((1, tb, tw), lambda b, h, i: (b, i, h))
    return pl.pallas_call(
        functools.partial(_wkv_kernel, n_chunks=tb // CHUNK, n_pairs=n_pairs),
        grid=grid,
        in_specs=[spec] * 8 + [pl.BlockSpec((SUBLANES, tw), lambda b, h, i: (0, h))],
        out_specs=spec,
        out_shape=jax.ShapeDtypeStruct((bsz, t, dm), BF16),
        scratch_shapes=[pltpu.VMEM((n_pairs, LANES, LANES), F32)],
        compiler_params=pltpu.CompilerParams(
            dimension_semantics=("arbitrary", "arbitrary", "arbitrary"),
            vmem_limit_bytes=VMEM_LIMIT_BYTES),
        name="wkv",
    )(r, lw, k, v, kk, a, c1, og, gn)


def _layer_norm(h, g, b):
    mu = jnp.mean(h, axis=-1, keepdims=True)
    hc = h - mu
    var = jnp.mean(hc * hc, axis=-1, keepdims=True)
    return hc * lax.rsqrt(var + LN_EPS) * g + b


def _mixer_out_kernel(m_ref, x_ref, wo_ref, ln_ref, o_ref, obf_ref, *, alpha):
    h = alpha * x_ref[...] + jnp.dot(m_ref[...], wo_ref[...], preferred_element_type=F32)
    y = _layer_norm(h, ln_ref[0:1, :], ln_ref[1:2, :])
    o_ref[...] = y
    obf_ref[...] = y.astype(BF16)


def _mixer_out(merged, x2d, wo, ln, *, alpha, tm):
    m, dm = x2d.shape
    return pl.pallas_call(
        functools.partial(_mixer_out_kernel, alpha=alpha),
        grid=(m // tm,),
        in_specs=[
            pl.BlockSpec((tm, dm), lambda i: (i, 0)),
            pl.BlockSpec((tm, dm), lambda i: (i, 0)),
            pl.BlockSpec((dm, dm), lambda i: (0, 0)),
            pl.BlockSpec((SUBLANES, dm), lambda i: (0, 0)),
        ],
        out_specs=[pl.BlockSpec((tm, dm), lambda i: (i, 0))] * 2,
        out_shape=[jax.ShapeDtypeStruct((m, dm), F32), jax.ShapeDtypeStruct((m, dm), BF16)],
        compiler_params=pltpu.CompilerParams(
            dimension_semantics=("arbitrary",), vmem_limit_bytes=VMEM_LIMIT_BYTES),
        name="mixer_out",
    )(merged, x2d, wo, ln)


def _ffn_kernel(xb_ref, x_ref, wg_ref, wu_ref, wd_ref, ln_ref, o_ref, acc_ref, *, alpha):
    f = pl.program_id(1)
    xb = xb_ref[...]
    gate = jnp.dot(xb, wg_ref[...], preferred_element_type=F32)
    up = jnp.dot(xb, wu_ref[...], preferred_element_type=F32)
    hid = (gate * jax.nn.sigmoid(gate) * up).astype(BF16)
    part = jnp.dot(hid, wd_ref[...], preferred_element_type=F32)

    @pl.when(f == 0)
    def _():
        acc_ref[...] = part

    @pl.when(f > 0)
    def _():
        acc_ref[...] += part

    @pl.when(f == pl.num_programs(1) - 1)
    def _():
        h = alpha * x_ref[...] + acc_ref[...]
        o_ref[...] = _layer_norm(h, ln_ref[0:1, :], ln_ref[1:2, :])


def _ffn(x_bf, x2d, w_gu, w_down, ln, *, alpha, tm, tf):
    m, dm = x2d.shape
    dff = w_down.shape[0]
    nf = dff // tf
    return pl.pallas_call(
        functools.partial(_ffn_kernel, alpha=alpha),
        grid=(m // tm, nf),
        in_specs=[
            pl.BlockSpec((tm, dm), lambda i, f: (i, 0)),
            pl.BlockSpec((tm, dm), lambda i, f: (i, 0)),
            pl.BlockSpec((dm, tf), lambda i, f: (0, f)),
            pl.BlockSpec((dm, tf), lambda i, f: (0, nf + f)),
            pl.BlockSpec((tf, dm), lambda i, f: (f, 0)),
            pl.BlockSpec((SUBLANES, dm), lambda i, f: (0, 0)),
        ],
        out_specs=pl.BlockSpec((tm, dm), lambda i, f: (i, 0)),
        out_shape=jax.ShapeDtypeStruct((m, dm), F32),
        scratch_shapes=[pltpu.VMEM((tm, dm), F32)],
        compiler_params=pltpu.CompilerParams(
            dimension_semantics=("arbitrary", "arbitrary"), vmem_limit_bytes=VMEM_LIMIT_BYTES),
        name="ffn",
    )(x_bf, x2d, w_gu, w_gu, w_down, ln)


def _pad_rows(a, n):
    return jnp.pad(a, ((0, n - a.shape[0]), (0, 0)))


def _tile_cols(w, tn):
    k, n = w.shape[-2], w.shape[-1]
    return jnp.moveaxis(w.reshape(w.shape[:-1] + (n // tn, tn)), -2, 0)


def _pick(n, candidates):
    for c in candidates:
        if n % c == 0:
            return c
    raise ValueError(f"no tile size for extent {n}")


def kernel(x, w_in, shift_mu, conv_w, w0, w_up, a0, a_up, g_up, k_k, k_a, r_k,
           gn_g, gn_b, w_o, ln1_g, ln1_b, w_gu, w_down, ln2_g, ln2_b):
    bsz, t, dm = x.shape
    depth = w_in.shape[0]
    lw_n, la_n, lg_n = w_up.shape[1], a_up.shape[1], g_up.shape[1]
    assert dm % (2 * LANES) == 0 and t % CHUNK == 0
    assert lw_n <= LANES and la_n <= LANES and lg_n <= 2 * LANES
    assert w_in.shape[2] == 8 * dm + lw_n + la_n + lg_n
    alpha = (2.0 * depth) ** 0.25
    tn = LANES
    tm_in = _pick(t, (512, 256, 128, 64))
    tb = _pick(t, (512, 256, 128, 64))
    tw = 2 * LANES
    m = bsz * t
    tm = _pick(m, (512, 256, 128, 64))
    tf = _pick(w_down.shape[1], (512, 256, 128))

    for l in range(depth):
        wi = w_in[l]
        c0 = 6 * dm
        cols = [wi[:, 0:dm], wi[:, dm:2 * dm], wi[:, 2 * dm:3 * dm],
                wi[:, c0 + lw_n + la_n + lg_n:c0 + lw_n + la_n + lg_n + dm],
                wi[:, 3 * dm:4 * dm], wi[:, 4 * dm:5 * dm], wi[:, 5 * dm:6 * dm],
                wi[:, c0 + lw_n + la_n + lg_n + dm:]]
        w_t = jnp.concatenate([_tile_cols(c.astype(BF16), tn) for c in cols], axis=-1)

        def lane_pad(a, n):
            return jnp.pad(a, ((0, 0), (0, n - a.shape[1])))

        wl = jnp.concatenate([
            lane_pad(wi[:, c0:c0 + lw_n], LANES),
            lane_pad(wi[:, c0 + lw_n:c0 + lw_n + la_n], LANES),
            lane_pad(wi[:, c0 + lw_n + la_n:c0 + lw_n + la_n + lg_n], 2 * LANES)],
            axis=1).astype(BF16)
        mu = shift_mu[l]
        s0 = 3 * dm
        mu_l = jnp.concatenate([
            jnp.pad(mu[s0:s0 + lw_n], (0, LANES - lw_n)),
            jnp.pad(mu[s0 + lw_n:s0 + lw_n + la_n], (0, LANES - la_n)),
            jnp.pad(mu[s0 + lw_n + la_n:], (0, 2 * LANES - lg_n))])[None, :]
        wup = jnp.concatenate([_pad_rows(w_up[l], LANES), _pad_rows(a_up[l], LANES),
                               _pad_rows(g_up[l], 2 * LANES)], axis=0).astype(BF16)
        wup_t = _tile_cols(wup, tn)
        prm = _pad_rows(jnp.stack([
            mu[0:dm], mu[dm:2 * dm], mu[2 * dm:3 * dm],
            conv_w[l, 0], conv_w[l, 1], conv_w[l, 2],
            w0[l], a0[l], k_k[l], k_a[l], r_k[l].reshape(-1)]), 16)

        r, lw, k, v, kk, a, c1, og = _mixer_in(
            x.astype(BF16), w_t, wl, wup_t, prm, mu_l, tm=tm_in, tn=tn)
        gn = _pad_rows(jnp.stack([gn_g[l], gn_b[l]]), SUBLANES)
        merged = _wkv(r, lw, k, v, kk, a, c1, og, gn, tb=tb, tw=tw)
        ln1 = _pad_rows(jnp.stack([ln1_g[l], ln1_b[l]]), SUBLANES)
        x1, x1_bf = _mixer_out(merged.reshape(m, dm), x.reshape(m, dm), w_o[l].astype(BF16),
                               ln1, alpha=alpha, tm=tm)
        ln2 = _pad_rows(jnp.stack([ln2_g[l], ln2_b[l]]), SUBLANES)
        x = _ffn(x1_bf, x1, w_gu[l].astype(BF16), w_down[l].astype(BF16), ln2,
                 alpha=alpha, tm=tm, tf=tf).reshape(bsz, t, dm)
    return x
```

```python
import functools
import math

import jax
import jax.numpy as jnp
from jax import lax
from jax.experimental import pallas as pl
from jax.experimental.pallas import tpu as pltpu

HEAD_SIZE = 64
LOG2_HEAD = 6
LOG_DECAY_SCALE = -math.exp(-0.5)
LN_EPS = 1e-5
GN_EPS = 64e-5
LANES = 128
SUBLANES = 8
CHUNK = 64
VMEM_LIMIT_BYTES = 56 * 1024 * 1024

F32 = jnp.float32
BF16 = jnp.bfloat16


def _mm(a, b):
    return jnp.dot(a.astype(BF16), b.astype(BF16), preferred_element_type=F32)


def _mm_nt(a, b):
    return lax.dot_general(a.astype(BF16), b.astype(BF16), (((1,), (1,)), ((), ())),
                           preferred_element_type=F32)


def _split2(a):
    hi = a.astype(BF16)
    lo = (a - hi.astype(F32)).astype(BF16)
    return hi, lo


def _split3(a):
    hi = a.astype(BF16)
    r1 = a - hi.astype(F32)
    mid = r1.astype(BF16)
    lo = (r1 - mid.astype(F32)).astype(BF16)
    return hi, mid, lo


def _mm_exact_lhs(a_exact, b):
    d = functools.partial(jnp.dot, preferred_element_type=F32)
    h, m, l = _split3(b)
    return d(a_exact, h) + (d(a_exact, m) + d(a_exact, l))


def _iota_shr(shape, dim, log2_div):
    return lax.shift_right_logical(lax.broadcasted_iota(jnp.int32, shape, dim), log2_div)


def _head_ones(n):
    r = _iota_shr((n, n), 0, LOG2_HEAD)
    c = _iota_shr((n, n), 1, LOG2_HEAD)
    return jnp.where(r == c, 1.0, 0.0).astype(BF16)


def _head_sum(q, ones):
    d = functools.partial(jnp.dot, preferred_element_type=F32)
    h, l = _split2(q)
    return d(h, ones) + d(l, ones)


N_GROUPS = 8
N_CARRY = 4
LORA_PAD = 512


def _mixer_in_kernel(x_ref, w_ref, wl_ref, wup_ref, p_ref, mul_ref,
                     r_ref, lw_ref, k_ref, v_ref, kk_ref, a_ref, c1_ref, og_ref,
                     lora_ref, carry_ref, carry_l_ref, *, tm, tn):
    i = pl.program_id(1)
    j = pl.program_id(2)
    first = i == 0
    xt = x_ref[0]
    row = lax.broadcasted_iota(jnp.int32, (tm, tn), 0)

    def prev1(p, c8):
        return jnp.where(row == 0, c8[SUBLANES - 1:SUBLANES, :], pltpu.roll(p, 1, 0))

    def prev2(p, c8):
        rolled = pltpu.roll(p, 2, 0)
        rolled = jnp.where(row == 1, c8[SUBLANES - 1:SUBLANES, :], rolled)
        return jnp.where(row == 0, c8[SUBLANES - 2:SUBLANES - 1, :], rolled)

    @pl.when(j == 0)
    def _():
        pl_ = jnp.dot(xt, wl_ref[...], preferred_element_type=F32)
        cl = jnp.where(first, 0.0, carry_l_ref[...])
        rowl = lax.broadcasted_iota(jnp.int32, (tm, LORA_PAD), 0)
        prev = jnp.where(rowl == 0, cl[SUBLANES - 1:SUBLANES, :], pltpu.roll(pl_, 1, 0))
        carry_l_ref[...] = pl_[tm - SUBLANES:, :]
        z = pl_ + mul_ref[...] * (prev - pl_)
        lora_ref[:, 0:LANES] = jnp.tanh(z[:, 0:LANES]).astype(BF16)
        lora_ref[:, LANES:2 * LANES] = z[:, LANES:2 * LANES].astype(BF16)
        lora_ref[:, 2 * LANES:] = jax.nn.sigmoid(z[:, 2 * LANES:]).astype(BF16)

    p_cb, p_cc, p_ch, p_gc, p_r, p_k, p_v, p_gr = (
        jnp.dot(xt, w_ref[g], preferred_element_type=F32) for g in range(N_GROUPS))
    prm = p_ref[...]
    mu_r, mu_k, mu_v = prm[0:1], prm[1:2], prm[2:3]
    cw0, cw1, cw2 = prm[3:4], prm[4:5], prm[5:6]
    w0, a0, k_k, k_a, r_k = prm[6:7], prm[7:8], prm[8:9], prm[9:10], prm[10:11]

    carry = [jnp.where(first, 0.0, carry_ref[j, q]) for q in range(N_CARRY)]

    u = p_cc * p_ch
    conv = cw2 * u + cw1 * prev1(u, carry[0]) + cw0 * prev2(u, carry[0])
    y_conv = jax.nn.sigmoid(p_gc) * (p_cb * conv)

    r = p_r + mu_r * (prev1(p_r, carry[1]) - p_r)
    k = p_k + mu_k * (prev1(p_k, carry[2]) - p_k)
    v = p_v + mu_v * (prev1(p_v, carry[3]) - p_v)
    for q, val in enumerate((u, p_r, p_k, p_v)):
        carry_ref[j, q] = val[tm - SUBLANES:, :]

    wup = wup_ref[...]
    d = functools.partial(jnp.dot, preferred_element_type=F32)
    w_pre = w0 + d(lora_ref[:, 0:LANES], wup[0:LANES])
    a_pre = a0 + d(lora_ref[:, LANES:2 * LANES], wup[LANES:2 * LANES])
    g = d(lora_ref[:, 2 * LANES:], wup[2 * LANES:])
    lw = LOG_DECAY_SCALE * jax.nn.sigmoid(w_pre)
    a = jax.nn.sigmoid(a_pre)

    ones = _head_ones(tn)
    kraw = k * k_k
    ss = _mm(kraw * kraw, ones)
    kk = kraw / jnp.maximum(jnp.sqrt(ss), 1e-12)
    k_mod = k * (1.0 + (a - 1.0) * k_a)
    bonus = _mm(r * k_mod * r_k, ones) * v
    og = g * jax.nn.sigmoid(p_gr)

    r_ref[0] = r
    lw_ref[0] = lw
    k_ref[0] = k_mod
    v_ref[0] = v
    kk_ref[0] = kk
    a_ref[0] = a
    c1_ref[0] = y_conv + og * bonus
    og_ref[0] = og


def _mixer_in(x_bf, w_t, wl, wup_t, prm, mu_l, *, tm, tn):
    bsz, t, dm = x_bf.shape
    nj = dm // tn
    grid = (bsz, t // tm, nj)
    out_sds = jax.ShapeDtypeStruct((bsz, t, dm), F32)
    out_spec = pl.BlockSpec((1, tm, tn), lambda b, i, j: (b, i, j))
    return pl.pallas_call(
        functools.partial(_mixer_in_kernel, tm=tm, tn=tn),
        grid=grid,
        in_specs=[
            pl.BlockSpec((1, tm, dm), lambda b, i, j: (b, i, 0)),
            pl.BlockSpec((N_GROUPS, dm, tn), lambda b, i, j: (0, 0, j)),
            pl.BlockSpec((dm, LORA_PAD), lambda b, i, j: (0, 0)),
            pl.BlockSpec((LORA_PAD, tn), lambda b, i, j: (0, j)),
            pl.BlockSpec((16, tn), lambda b, i, j: (0, j)),
            pl.BlockSpec((1, LORA_PAD), lambda b, i, j: (0, 0)),
        ],
        out_specs=[out_spec] * 8,
        out_shape=[out_sds] * 8,
        scratch_shapes=[
            pltpu.VMEM((tm, LORA_PAD), BF16),
            pltpu.VMEM((nj, N_CARRY, SUBLANES, tn), F32),
            pltpu.VMEM((SUBLANES, LORA_PAD), F32),
        ],
        compiler_params=pltpu.CompilerParams(
            dimension_semantics=("arbitrary", "arbitrary", "arbitrary"),
            vmem_limit_bytes=VMEM_LIMIT_BYTES),
        name="mixer_in",
    )(x_bf, w_t, wl, wup_t, prm, mu_l)


def _wkv_block(r, lw, k, v, kk, a, s0, consts, n_chunks):
    cs = consts
    c = CHUNK
    idx = range(len(r))

    def stack(x):
        xb = x.astype(BF16)
        m0, m1 = (cs["m0"], cs["m1"]) if x.shape[1] == LANES else (cs["m0_2"], cs["m1_2"])
        return jnp.concatenate([xb * m0, xb * m1], axis=0)

    def bdiag(xw):
        xb = xw.astype(BF16)
        return jnp.concatenate([xb, xb], axis=0) * cs["bd_bf"]

    def each(fn, *lists):
        return [fn(*args) for args in zip(*lists)]

    cum = each(lambda x: _mm_exact_lhs(cs["tri"], x), lw)
    beta = each(lambda x, y: x * y, kk, a)
    e_inv = each(lambda x: jnp.exp(-x), cum)
    e_rem = each(lambda x: jnp.exp(x[c - 1:c, :] - x), cum)
    decay_c = each(lambda x: jnp.exp(x[c - 1:c, :]), cum)
    rt = each(lambda x, y: x * jnp.exp(y), r, cum)
    at = each(lambda x, y, z_: -x * jnp.exp(y - z_), kk, cum, lw)
    kt = each(lambda x, y: x * y, k, e_inv)
    bt = each(lambda x, y: x * y, beta, e_inv)
    kh = each(lambda x, y: x * y, k, e_rem)
    bh = each(lambda x, y: x * y, beta, e_rem)

    s = each(lambda a_, r_, b_, k_: _mm_nt(jnp.concatenate([a_, r_], axis=0),
                                           jnp.concatenate([stack(b_), stack(k_)], axis=0)),
             at, rt, bt, kt)
    lab = each(lambda x: jnp.where(cs["strict"], x[0:c, 0:2 * c], 0.0), s)
    lak = each(lambda x: jnp.where(cs["strict"], x[0:c, 2 * c:], 0.0), s)
    urb = each(lambda x: jnp.where(cs["incl"], x[c:, 0:2 * c], 0.0), s)
    urk = each(lambda x: jnp.where(cs["incl"], x[c:, 2 * c:], 0.0), s)

    l8 = each(lambda x: jnp.where(cs["blk8"], x, 0.0), lab)
    l8_2 = each(lambda x: _mm(x, bdiag(x)), l8)
    l8_4 = each(lambda x: _mm(x, bdiag(x)), l8_2)
    tw = each(lambda x, y: _mm(cs["eye"] + x, bdiag(cs["eye"] + y)), l8, l8_2)
    tw = each(lambda x, y: _mm(x, bdiag(cs["eye"] + y)), tw, l8_4)
    for join in cs["joins"]:
        half = each(lambda t_, l_: _mm(t_, bdiag(jnp.where(join, l_, 0.0))), tw, lab)
        tw = each(lambda t_, h_: t_ + _mm(h_, bdiag(t_)), tw, half)

    nv = each(lambda l_, v_: _mm(l_, stack(v_)), lak, v)
    ah_d = each(lambda t_, a_, n_: _mm(t_, stack(jnp.concatenate([a_, n_], axis=1))),
                tw, at, nv)
    u2 = each(lambda u_, x: _mm(u_, stack(x)), urb, ah_d)
    uv = each(lambda u_, v_: _mm(u_, stack(v_)), urk, v)
    rh = each(lambda r_, u_: r_ + u_[:, 0:LANES], rt, u2)
    e = each(lambda u_, w_: u_[:, LANES:] + w_, u2, uv)
    gf = each(lambda x, b_: _mm(jnp.concatenate([x[:, 0:LANES].T, x[:, LANES:].T], axis=0), b_),
              ah_d, bh)
    vk = each(lambda v_, k_: _mm(v_.T, k_), v, kh)
    glt = each(lambda x: jnp.where(cs["bd"], x[0:LANES], 0.0), gf)
    ft = each(lambda x, w_: jnp.where(cs["bd"], x[LANES:] + w_, 0.0), gf, vk)

    ys = [None] * len(r)
    states = list(s0)
    for ci in range(n_chunks):
        for p in range(len(states)):
            n = p * n_chunks + ci
            st = states[p]
            ys[n] = _mm_nt(rh[n], st) + e[n]
            states[p] = decay_c[n] * st + _mm(st, glt[n]) + ft[n]
    return ys, states


def _wkv_consts():
    c = CHUNK
    assert c == HEAD_SIZE and 2 * c == LANES
    t = lax.broadcasted_iota(jnp.int32, (c, 2 * c), 0)
    i = jnp.bitwise_and(lax.broadcasted_iota(jnp.int32, (c, 2 * c), 1), c - 1)
    blk = [lax.shift_right_logical(t, s) == lax.shift_right_logical(i, s) for s in (3, 4, 5)]
    out = [jnp.logical_not(b) for b in blk]
    rr = lax.broadcasted_iota(jnp.int32, (c, c), 0)
    cc = lax.broadcasted_iota(jnp.int32, (c, c), 1)
    def head_lane_masks(width):
        lane = jnp.bitwise_and(lax.broadcasted_iota(jnp.int32, (c, width), 1), LANES - 1)
        return (jnp.where(lane < HEAD_SIZE, 1.0, 0.0).astype(BF16),
                jnp.where(lane < HEAD_SIZE, 0.0, 1.0).astype(BF16))

    m0, m1 = head_lane_masks(LANES)
    m0_2, m1_2 = head_lane_masks(2 * LANES)
    bd = _iota_shr((2 * c, 2 * c), 0, LOG2_HEAD) == _iota_shr((2 * c, 2 * c), 1, LOG2_HEAD)
    return dict(
        strict=i < t, incl=i <= t, eye=jnp.where(i == t, 1.0, 0.0).astype(F32), blk8=blk[0],
        joins=[jnp.logical_and(blk[1], out[0]), jnp.logical_and(blk[2], out[1]), out[2]],
        tri=jnp.where(cc <= rr, 1.0, 0.0).astype(BF16),
        m0=m0, m1=m1, m0_2=m0_2, m1_2=m1_2,
        bd=bd, bd_bf=jnp.where(bd, 1.0, 0.0).astype(BF16))


def _wkv_kernel(r_ref, lw_ref, k_ref, v_ref, kk_ref, a_ref, c1_ref, og_ref, gn_ref,
                o_ref, s_ref, *, n_chunks, n_pairs):
    @pl.when(pl.program_id(2) == 0)
    def _():
        s_ref[...] = jnp.zeros_like(s_ref)

    def chunks(ref):
        return [ref[0, ci * CHUNK:(ci + 1) * CHUNK, p * LANES:(p + 1) * LANES]
                for p in range(n_pairs) for ci in range(n_chunks)]

    ys, states = _wkv_block(chunks(r_ref), chunks(lw_ref), chunks(k_ref), chunks(v_ref),
                            chunks(kk_ref), chunks(a_ref), [s_ref[p] for p in range(n_pairs)],
                            _wkv_consts(), n_chunks)
    ones = _head_ones(LANES)
    inv_n = 1.0 / HEAD_SIZE
    for p in range(n_pairs):
        s_ref[p] = states[p]
        ls = slice(p * LANES, (p + 1) * LANES)
        y = jnp.concatenate(ys[p * n_chunks:(p + 1) * n_chunks], axis=0)
        mu = _head_sum(y, ones) * inv_n
        yc = y - mu
        var = _head_sum(yc * yc, ones) * inv_n
        yn = yc * lax.rsqrt(var + GN_EPS) * gn_ref[0:1, ls] + gn_ref[1:2, ls]
        o_ref[0, :, ls] = (c1_ref[0, :, ls] + og_ref[0, :, ls] * yn).astype(o_ref.dtype)


def _wkv(r, lw, k, v, kk, a, c1, og, gn, *, tb, tw):
    bsz, t, dm = r.shape
    n_pairs = tw // LANES
    grid = (bsz, dm // tw, t // tb)
    spec = pl.BlockSpec((1, tb, tw), lambda b, h, i: (b, i, h))
    return pl.pallas_call(
        functools.partial(_wkv_kernel, n_chunks=tb // CHUNK, n_pairs=n_pairs),
        grid=grid,
        in_specs=[spec] * 8 + [pl.BlockSpec((SUBLANES, tw), lambda b, h, i: (0, h))],
        out_specs=spec,
        out_shape=jax.ShapeDtypeStruct((bsz, t, dm), BF16),
        scratch_shapes=[pltpu.VMEM((n_pairs, LANES, LANES), F32)],
        compiler_params=pltpu.CompilerParams(
            dimension_semantics=("arbitrary", "arbitrary", "arbitrary"),
            vmem_limit_bytes=VMEM_LIMIT_BYTES),
        name="wkv",
    )(r, lw, k, v, kk, a, c1, og, gn)


def _layer_norm(h, g, b):
    mu = jnp.mean(h, axis=-1, keepdims=True)
    hc = h - mu
    var = jnp.mean(hc * hc, axis=-1, keepdims=True)
    return hc * lax.rsqrt(var + LN_EPS) * g + b


def _mixer_out_kernel(m_ref, x_ref, wo_ref, ln_ref, o_ref, obf_ref, *, alpha):
    h = alpha * x_ref[...] + jnp.dot(m_ref[...], wo_ref[...], preferred_element_type=F32)
    y = _layer_norm(h, ln_ref[0:1, :], ln_ref[1:2, :])
    o_ref[...] = y
    obf_ref[...] = y.astype(BF16)


def _mixer_out(merged, x2d, wo, ln, *, alpha, tm):
    m, dm = x2d.shape
    return pl.pallas_call(
        functools.partial(_mixer_out_kernel, alpha=alpha),
        grid=(m // tm,),
        in_specs=[
            pl.BlockSpec((tm, dm), lambda i: (i, 0)),
            pl.BlockSpec((tm, dm), lambda i: (i, 0)),
            pl.BlockSpec((dm, dm), lambda i: (0, 0)),
            pl.BlockSpec((SUBLANES, dm), lambda i: (0, 0)),
        ],
        out_specs=[pl.BlockSpec((tm, dm), lambda i: (i, 0))] * 2,
        out_shape=[jax.ShapeDtypeStruct((m, dm), F32), jax.ShapeDtypeStruct((m, dm), BF16)],
        compiler_params=pltpu.CompilerParams(
            dimension_semantics=("arbitrary",), vmem_limit_bytes=VMEM_LIMIT_BYTES),
        name="mixer_out",
    )(merged, x2d, wo, ln)


def _ffn_kernel(xb_ref, x_ref, wg_ref, wu_ref, wd_ref, ln_ref, o_ref, acc_ref, *, alpha):
    f = pl.program_id(1)

    @pl.when(jnp.logical_and(pl.program_id(0) == 0, f == 0))
    def _():
        acc_ref[...] = jnp.zeros_like(acc_ref)

    xb = xb_ref[...]
    gate = jnp.dot(xb, wg_ref[...], preferred_element_type=F32)
    up = jnp.dot(xb, wu_ref[...], preferred_element_type=F32)
    hid = (gate * jax.nn.sigmoid(gate) * up).astype(BF16)
    part = jnp.dot(hid, wd_ref[...], preferred_element_type=F32)
    acc_ref[...] += part

    @pl.when(f == pl.num_programs(1) - 1)
    def _():
        h = alpha * x_ref[...] + acc_ref[...]
        o_ref[...] = _layer_norm(h, ln_ref[0:1, :], ln_ref[1:2, :])
        acc_ref[...] = jnp.zeros_like(acc_ref)


def _ffn(x_bf, x2d, w_gu, w_down, ln, *, alpha, tm, tf):
    m, dm = x2d.shape
    dff = w_down.shape[0]
    nf = dff // tf
    return pl.pallas_call(
        functools.partial(_ffn_kernel, alpha=alpha),
        grid=(m // tm, nf),
        in_specs=[
            pl.BlockSpec((tm, dm), lambda i, f: (i, 0)),
            pl.BlockSpec((tm, dm), lambda i, f: (i, 0)),
            pl.BlockSpec((dm, tf), lambda i, f: (0, f)),
            pl.BlockSpec((dm, tf), lambda i, f: (0, nf + f)),
            pl.BlockSpec((tf, dm), lambda i, f: (f, 0)),
            pl.BlockSpec((SUBLANES, dm), lambda i, f: (0, 0)),
        ],
        out_specs=pl.BlockSpec((tm, dm), lambda i, f: (i, 0)),
        out_shape=jax.ShapeDtypeStruct((m, dm), F32),
        scratch_shapes=[pltpu.VMEM((tm, dm), F32)],
        compiler_params=pltpu.CompilerParams(
            dimension_semantics=("arbitrary", "arbitrary"), vmem_limit_bytes=VMEM_LIMIT_BYTES),
        name="ffn",
    )(x_bf, x2d, w_gu, w_gu, w_down, ln)


def _pad_rows(a, n):
    return jnp.pad(a, ((0, n - a.shape[0]), (0, 0)))


def _pick(n, candidates):
    for c in candidates:
        if n % c == 0:
            return c
    raise ValueError(f"no tile size for extent {n}")


def kernel(x, w_in, shift_mu, conv_w, w0, w_up, a0, a_up, g_up, k_k, k_a, r_k,
           gn_g, gn_b, w_o, ln1_g, ln1_b, w_gu, w_down, ln2_g, ln2_b):
    bsz, t, dm = x.shape
    depth = w_in.shape[0]
    lw_n, la_n, lg_n = w_up.shape[1], a_up.shape[1], g_up.shape[1]
    assert dm % (2 * LANES) == 0 and t % CHUNK == 0
    assert lw_n <= LANES and la_n <= LANES and lg_n <= 2 * LANES
    assert w_in.shape[2] == 8 * dm + lw_n + la_n + lg_n
    alpha = (2.0 * depth) ** 0.25
    tn = 2 * LANES
    tm_in = _pick(t, (512, 256, 128, 64))
    tb = _pick(t, (512, 256, 128, 64))
    tw = 2 * LANES
    m = bsz * t
    tm = _pick(m, (512, 256, 128, 64))
    tf = _pick(w_down.shape[1], (512, 256, 128))

    for l in range(depth):
        wi = w_in[l]
        c0 = 6 * dm
        cols = [wi[:, 0:dm], wi[:, dm:2 * dm], wi[:, 2 * dm:3 * dm],
                wi[:, c0 + lw_n + la_n + lg_n:c0 + lw_n + la_n + lg_n + dm],
                wi[:, 3 * dm:4 * dm], wi[:, 4 * dm:5 * dm], wi[:, 5 * dm:6 * dm],
                wi[:, c0 + lw_n + la_n + lg_n + dm:]]
        w_t = jnp.stack([c.astype(BF16) for c in cols])

        def lane_pad(a, n):
            return jnp.pad(a, ((0, 0), (0, n - a.shape[1])))

        wl = jnp.concatenate([
            lane_pad(wi[:, c0:c0 + lw_n], LANES),
            lane_pad(wi[:, c0 + lw_n:c0 + lw_n + la_n], LANES),
            lane_pad(wi[:, c0 + lw_n + la_n:c0 + lw_n + la_n + lg_n], 2 * LANES)],
            axis=1).astype(BF16)
        mu = shift_mu[l]
        s0 = 3 * dm
        mu_l = jnp.concatenate([
            jnp.pad(mu[s0:s0 + lw_n], (0, LANES - lw_n)),
            jnp.pad(mu[s0 + lw_n:s0 + lw_n + la_n], (0, LANES - la_n)),
            jnp.pad(mu[s0 + lw_n + la_n:], (0, 2 * LANES - lg_n))])[None, :]
        wup = jnp.concatenate([_pad_rows(w_up[l], LANES), _pad_rows(a_up[l], LANES),
                               _pad_rows(g_up[l], 2 * LANES)], axis=0).astype(BF16)
        prm = _pad_rows(jnp.stack([
            mu[0:dm], mu[dm:2 * dm], mu[2 * dm:3 * dm],
            conv_w[l, 0], conv_w[l, 1], conv_w[l, 2],
            w0[l], a0[l], k_k[l], k_a[l], r_k[l].reshape(-1)]), 16)

        r, lw, k, v, kk, a, c1, og = _mixer_in(
            x.astype(BF16), w_t, wl, wup, prm, mu_l, tm=tm_in, tn=tn)
        gn = _pad_rows(jnp.stack([gn_g[l], gn_b[l]]), SUBLANES)
        merged = _wkv(r, lw, k, v, kk, a, c1, og, gn, tb=tb, tw=tw)
        ln1 = _pad_rows(jnp.stack([ln1_g[l], ln1_b[l]]), SUBLANES)
        x1, x1_bf = _mixer_out(merged.reshape(m, dm), x.reshape(m, dm), w_o[l].astype(BF16),
                               ln1, alpha=alpha, tm=tm)
        ln2 = _pad_rows(jnp.stack([ln2_g[l], ln2_b[l]]), SUBLANES)
        x = _ffn(x1_bf, x1, w_gu[l].astype(BF16), w_down[l].astype(BF16), ln2,
                 alpha=alpha, tm=tm, tf=tf).reshape(bsz, t, dm)
    return x
```

```python
import functools
import math

import jax
import jax.numpy as jnp
from jax import lax
from jax.experimental import pallas as pl
from jax.experimental.pallas import tpu as pltpu

HEAD_SIZE = 64
LOG2_HEAD = 6
LOG_DECAY_SCALE = -math.exp(-0.5)
LN_EPS = 1e-5
GN_EPS = 64e-5
LANES = 128
SUBLANES = 8
CHUNK = 64
VMEM_LIMIT_BYTES = 56 * 1024 * 1024

F32 = jnp.float32
BF16 = jnp.bfloat16


def _mm(a, b):
    return jnp.dot(a.astype(BF16), b.astype(BF16), preferred_element_type=F32)


def _mm_nt(a, b):
    return lax.dot_general(a.astype(BF16), b.astype(BF16), (((1,), (1,)), ((), ())),
                           preferred_element_type=F32)


def _split2(a):
    hi = a.astype(BF16)
    lo = (a - hi.astype(F32)).astype(BF16)
    return hi, lo


def _split3(a):
    hi = a.astype(BF16)
    r1 = a - hi.astype(F32)
    mid = r1.astype(BF16)
    lo = (r1 - mid.astype(F32)).astype(BF16)
    return hi, mid, lo


def _mm_exact_lhs(a_exact, b):
    d = functools.partial(jnp.dot, preferred_element_type=F32)
    h, m, l = _split3(b)
    return d(a_exact, h) + (d(a_exact, m) + d(a_exact, l))


def _iota_shr(shape, dim, log2_div):
    return lax.shift_right_logical(lax.broadcasted_iota(jnp.int32, shape, dim), log2_div)


def _head_ones(n):
    r = _iota_shr((n, n), 0, LOG2_HEAD)
    c = _iota_shr((n, n), 1, LOG2_HEAD)
    return jnp.where(r == c, 1.0, 0.0).astype(BF16)


def _head_sum(q, ones):
    d = functools.partial(jnp.dot, preferred_element_type=F32)
    h, l = _split2(q)
    return d(h, ones) + d(l, ones)


N_GROUPS = 8
N_CARRY = 4
LORA_PAD = 512


def _mixer_in_kernel(x_ref, *refs, tm, tn):
    w_refs = refs[:N_GROUPS]
    wl_ref, wup_ref, p_ref, mul_ref = refs[N_GROUPS:N_GROUPS + 4]
    r_ref, lw_ref, k_ref, v_ref, kk_ref, a_ref, c1_ref, og_ref = refs[N_GROUPS + 4:N_GROUPS + 12]
    xb_ref, lora_ref, carry_ref, carry_l_ref = refs[N_GROUPS + 12:]
    i = pl.program_id(1)
    j = pl.program_id(2)
    first = i == 0
    row = lax.broadcasted_iota(jnp.int32, (tm, tn), 0)

    @pl.when(j == 0)
    def _():
        xb_ref[...] = x_ref[0].astype(BF16)

    xt = xb_ref[...]

    def prev1(p, c8):
        return jnp.where(row == 0, c8[SUBLANES - 1:SUBLANES, :], pltpu.roll(p, 1, 0))

    def prev2(p, c8):
        rolled = pltpu.roll(p, 2, 0)
        rolled = jnp.where(row == 1, c8[SUBLANES - 1:SUBLANES, :], rolled)
        return jnp.where(row == 0, c8[SUBLANES - 2:SUBLANES - 1, :], rolled)

    @pl.when(j == 0)
    def _():
        pl_ = jnp.dot(xt, wl_ref[...], preferred_element_type=F32)
        cl = jnp.where(first, 0.0, carry_l_ref[...])
        rowl = lax.broadcasted_iota(jnp.int32, (tm, LORA_PAD), 0)
        prev = jnp.where(rowl == 0, cl[SUBLANES - 1:SUBLANES, :], pltpu.roll(pl_, 1, 0))
        carry_l_ref[...] = pl_[tm - SUBLANES:, :]
        z = pl_ + mul_ref[...] * (prev - pl_)
        lora_ref[:, 0:LANES] = jnp.tanh(z[:, 0:LANES]).astype(BF16)
        lora_ref[:, LANES:2 * LANES] = z[:, LANES:2 * LANES].astype(BF16)
        lora_ref[:, 2 * LANES:] = jax.nn.sigmoid(z[:, 2 * LANES:]).astype(BF16)

    p_cb, p_cc, p_ch, p_gc, p_r, p_k, p_v, p_gr = (
        jnp.dot(xt, w_ref[...], preferred_element_type=F32) for w_ref in w_refs)
    prm = p_ref[...]
    mu_r, mu_k, mu_v = prm[0:1], prm[1:2], prm[2:3]
    cw0, cw1, cw2 = prm[3:4], prm[4:5], prm[5:6]
    w0, a0, k_k, k_a, r_k = prm[6:7], prm[7:8], prm[8:9], prm[9:10], prm[10:11]

    carry = [jnp.where(first, 0.0, carry_ref[j, q]) for q in range(N_CARRY)]

    u = p_cc * p_ch
    conv = cw2 * u + cw1 * prev1(u, carry[0]) + cw0 * prev2(u, carry[0])
    y_conv = jax.nn.sigmoid(p_gc) * (p_cb * conv)

    r = p_r + mu_r * (prev1(p_r, carry[1]) - p_r)
    k = p_k + mu_k * (prev1(p_k, carry[2]) - p_k)
    v = p_v + mu_v * (prev1(p_v, carry[3]) - p_v)
    for q, val in enumerate((u, p_r, p_k, p_v)):
        carry_ref[j, q] = val[tm - SUBLANES:, :]

    wup = wup_ref[...]
    d = functools.partial(jnp.dot, preferred_element_type=F32)
    w_pre = w0 + d(lora_ref[:, 0:LANES], wup[0:LANES])
    a_pre = a0 + d(lora_ref[:, LANES:2 * LANES], wup[LANES:2 * LANES])
    g = d(lora_ref[:, 2 * LANES:], wup[2 * LANES:])
    lw = LOG_DECAY_SCALE * jax.nn.sigmoid(w_pre)
    a = jax.nn.sigmoid(a_pre)

    ones = _head_ones(tn)
    kraw = k * k_k
    ss = _mm(kraw * kraw, ones)
    kk = kraw / jnp.maximum(jnp.sqrt(ss), 1e-12)
    k_mod = k * (1.0 + (a - 1.0) * k_a)
    bonus = _mm(r * k_mod * r_k, ones) * v
    og = g * jax.nn.sigmoid(p_gr)

    r_ref[0] = r
    lw_ref[0] = lw
    k_ref[0] = k_mod
    v_ref[0] = v
    kk_ref[0] = kk
    a_ref[0] = a
    c1_ref[0] = y_conv + og * bonus
    og_ref[0] = og


def _mixer_in(x, w_groups, wl, wup, prm, mu_l, *, tm, tn):
    bsz, t, dm = x.shape
    nj = dm // tn
    grid = (bsz, t // tm, nj)
    out_sds = jax.ShapeDtypeStruct((bsz, t, dm), F32)
    out_spec = pl.BlockSpec((1, tm, tn), lambda b, i, j: (b, i, j))
    return pl.pallas_call(
        functools.partial(_mixer_in_kernel, tm=tm, tn=tn),
        grid=grid,
        in_specs=[
            pl.BlockSpec((1, tm, dm), lambda b, i, j: (b, i, 0)),
            *[pl.BlockSpec((dm, tn), lambda b, i, j: (0, j))] * N_GROUPS,
            pl.BlockSpec((dm, LORA_PAD), lambda b, i, j: (0, 0)),
            pl.BlockSpec((LORA_PAD, tn), lambda b, i, j: (0, j)),
            pl.BlockSpec((16, tn), lambda b, i, j: (0, j)),
            pl.BlockSpec((1, LORA_PAD), lambda b, i, j: (0, 0)),
        ],
        out_specs=[out_spec] * 8,
        out_shape=[out_sds] * 8,
        scratch_shapes=[
            pltpu.VMEM((tm, dm), BF16),
            pltpu.VMEM((tm, LORA_PAD), BF16),
            pltpu.VMEM((nj, N_CARRY, SUBLANES, tn), F32),
            pltpu.VMEM((SUBLANES, LORA_PAD), F32),
        ],
        compiler_params=pltpu.CompilerParams(
            dimension_semantics=("arbitrary", "arbitrary", "arbitrary"),
            vmem_limit_bytes=VMEM_LIMIT_BYTES),
        name="mixer_in",
    )(x, *w_groups, wl, wup, prm, mu_l)


def _wkv_block(r, lw, k, v, kk, a, z0, consts, n_chunks):
    cs = consts
    c = CHUNK
    idx = range(len(r))

    def stack(x):
        xb = x.astype(BF16)
        m0, m1 = (cs["m0"], cs["m1"]) if x.shape[1] == LANES else (cs["m0_2"], cs["m1_2"])
        return jnp.concatenate([xb * m0, xb * m1], axis=0)

    def bdiag(xw):
        xb = xw.astype(BF16)
        return jnp.concatenate([xb, xb], axis=0) * cs["bd_bf"]

    def each(fn, *lists):
        return [fn(*args) for args in zip(*lists)]

    cum = each(lambda x: _mm_exact_lhs(cs["tri"], x), lw)
    beta = each(lambda x, y: x * y, kk, a)
    e_inv = each(lambda x: jnp.exp(-x), cum)
    e_rem = each(lambda x: jnp.exp(x[c - 1:c, :] - x), cum)
    rt = each(lambda x, y: x * jnp.exp(y), r, cum)
    at = each(lambda x, y, z_: -x * jnp.exp(y - z_), kk, cum, lw)
    kt = each(lambda x, y: x * y, k, e_inv)
    bt = each(lambda x, y: x * y, beta, e_inv)
    kh = each(lambda x, y: x * y, k, e_rem)
    bh = each(lambda x, y: x * y, beta, e_rem)

    s = each(lambda a_, r_, b_, k_: _mm_nt(jnp.concatenate([a_, r_], axis=0),
                                           jnp.concatenate([stack(b_), stack(k_)], axis=0)),
             at, rt, bt, kt)
    lab = each(lambda x: jnp.where(cs["strict"], x[0:c, 0:2 * c], 0.0), s)
    urb = each(lambda x: jnp.where(cs["incl"], x[c:, 0:2 * c], 0.0), s)

    l8 = each(lambda x: jnp.where(cs["blk8"], x, 0.0), lab)
    l8_2 = each(lambda x: _mm(x, bdiag(x)), l8)
    l8_4 = each(lambda x: _mm(x, bdiag(x)), l8_2)
    tw = each(lambda x, y: _mm(cs["eye"] + x, bdiag(cs["eye"] + y)), l8, l8_2)
    tw = each(lambda x, y: _mm(x, bdiag(cs["eye"] + y)), tw, l8_4)
    for join in cs["joins"]:
        half = each(lambda t_, l_: _mm(t_, bdiag(jnp.where(join, l_, 0.0))), tw, lab)
        tw = each(lambda t_, h_: t_ + _mm(h_, bdiag(t_)), tw, half)

    nv_uv = each(lambda x, v_: _mm(jnp.where(cs["strict_incl"], x[:, 2 * c:], 0.0), stack(v_)),
                 s, v)
    ah_d = each(lambda t_, a_, n_: _mm(t_, stack(jnp.concatenate([a_, n_[0:c]], axis=1))),
                tw, at, nv_uv)
    u2 = each(lambda u_, x: _mm(u_, stack(x)), urb, ah_d)
    rh = each(lambda r_, u_: r_ + u_[:, 0:LANES], rt, u2)
    e = each(lambda u_, w_: u_[:, LANES:] + w_[c:], u2, nv_uv)
    zeros = jnp.zeros((c, LANES), F32)
    gf = each(lambda b_, k_, x, v_: _mm(
        jnp.concatenate([b_, k_], axis=0).T,
        jnp.concatenate([x, jnp.concatenate([zeros, v_], axis=1)], axis=0)), bh, kh, ah_d, v)
    gl = each(lambda x: jnp.where(cs["bd"], x[:, 0:LANES], 0.0), gf)
    f = each(lambda x: jnp.where(cs["bd"], x[:, LANES:], 0.0), gf)
    decay_col = each(lambda x: jnp.exp(x.T[:, c - 1:c]), cum)

    ys = [None] * len(r)
    states = list(z0)
    for ci in range(n_chunks):
        for p in range(len(states)):
            n = p * n_chunks + ci
            z = states[p]
            rz_gz = _mm(jnp.concatenate([rh[n], gl[n]], axis=0), z)
            ys[n] = rz_gz[0:c] + e[n]
            states[p] = decay_col[n] * z + rz_gz[c:] + f[n]
    return ys, states


def _wkv_consts():
    c = CHUNK
    assert c == HEAD_SIZE and 2 * c == LANES
    t = lax.broadcasted_iota(jnp.int32, (c, 2 * c), 0)
    i = jnp.bitwise_and(lax.broadcasted_iota(jnp.int32, (c, 2 * c), 1), c - 1)
    blk = [lax.shift_right_logical(t, s) == lax.shift_right_logical(i, s) for s in (3, 4, 5)]
    out = [jnp.logical_not(b) for b in blk]
    rr = lax.broadcasted_iota(jnp.int32, (c, c), 0)
    cc = lax.broadcasted_iota(jnp.int32, (c, c), 1)
    def head_lane_masks(width):
        lane = jnp.bitwise_and(lax.broadcasted_iota(jnp.int32, (c, width), 1), LANES - 1)
        return (jnp.where(lane < HEAD_SIZE, 1.0, 0.0).astype(BF16),
                jnp.where(lane < HEAD_SIZE, 0.0, 1.0).astype(BF16))

    m0, m1 = head_lane_masks(LANES)
    m0_2, m1_2 = head_lane_masks(2 * LANES)
    bd = _iota_shr((2 * c, 2 * c), 0, LOG2_HEAD) == _iota_shr((2 * c, 2 * c), 1, LOG2_HEAD)
    t2 = lax.broadcasted_iota(jnp.int32, (2 * c, 2 * c), 0)
    i2 = jnp.bitwise_and(lax.broadcasted_iota(jnp.int32, (2 * c, 2 * c), 1), c - 1)
    strict_incl = jnp.logical_or(jnp.logical_and(t2 < c, i2 < t2),
                                 jnp.logical_and(t2 >= c, i2 <= t2 - c))
    return dict(
        strict=i < t, incl=i <= t, strict_incl=strict_incl, eye=jnp.where(i == t, 1.0, 0.0).astype(F32), blk8=blk[0],
        joins=[jnp.logical_and(blk[1], out[0]), jnp.logical_and(blk[2], out[1]), out[2]],
        tri=jnp.where(cc <= rr, 1.0, 0.0).astype(BF16),
        m0=m0, m1=m1, m0_2=m0_2, m1_2=m1_2,
        bd=bd, bd_bf=jnp.where(bd, 1.0, 0.0).astype(BF16))


def _wkv_kernel(r_ref, lw_ref, k_ref, v_ref, kk_ref, a_ref, c1_ref, og_ref, gn_ref,
                o_ref, z_ref, *, n_chunks, n_pairs):
    @pl.when(pl.program_id(2) == 0)
    def _():
        z_ref[...] = jnp.zeros_like(z_ref)

    def chunks(ref):
        return [ref[0, ci * CHUNK:(ci + 1) * CHUNK, p * LANES:(p + 1) * LANES]
                for p in range(n_pairs) for ci in range(n_chunks)]

    ys, states = _wkv_block(chunks(r_ref), chunks(lw_ref), chunks(k_ref), chunks(v_ref),
                            chunks(kk_ref), chunks(a_ref), [z_ref[p] for p in range(n_pairs)],
                            _wkv_consts(), n_chunks)
    ones = _head_ones(LANES)
    inv_n = 1.0 / HEAD_SIZE
    y = jnp.concatenate(ys, axis=0)
    mu = _mm(y, ones) * inv_n
    yc = y - mu
    var = _mm(yc * yc, ones) * inv_n
    yn = yc * lax.rsqrt(var + GN_EPS)
    tb = n_chunks * CHUNK
    for p in range(n_pairs):
        z_ref[p] = states[p]
        ls = slice(p * LANES, (p + 1) * LANES)
        yp = yn[p * tb:(p + 1) * tb] * gn_ref[0:1, ls] + gn_ref[1:2, ls]
        o_ref[0, :, ls] = (c1_ref[0, :, ls] + og_ref[0, :, ls] * yp).astype(o_ref.dtype)


def _wkv(r, lw, k, v, kk, a, c1, og, gn, *, tb, tw):
    bsz, t, dm = r.shape
    n_pairs = tw // LANES
    grid = (bsz, dm // tw, t // tb)
    spec = pl.BlockSpec((1, tb, tw), lambda b, h, i: (b, i, h))
    return pl.pallas_call(
        functools.partial(_wkv_kernel, n_chunks=tb // CHUNK, n_pairs=n_pairs),
        grid=grid,
        in_specs=[spec] * 8 + [pl.BlockSpec((SUBLANES, tw), lambda b, h, i: (0, h))],
        out_specs=spec,
        out_shape=jax.ShapeDtypeStruct((bsz, t, dm), BF16),
        scratch_shapes=[pltpu.VMEM((n_pairs, LANES, LANES), F32)],
        compiler_params=pltpu.CompilerParams(
            dimension_semantics=("arbitrary", "arbitrary", "arbitrary"),
            vmem_limit_bytes=VMEM_LIMIT_BYTES),
        name="wkv",
    )(r, lw, k, v, kk, a, c1, og, gn)


def _layer_norm(h, g, b):
    mu = jnp.mean(h, axis=-1, keepdims=True)
    hc = h - mu
    var = jnp.mean(hc * hc, axis=-1, keepdims=True)
    return hc * lax.rsqrt(var + LN_EPS) * g + b


def _mixer_out_kernel(m_ref, x_ref, wo_ref, ln_ref, o_ref, obf_ref, *, alpha):
    h = alpha * x_ref[...] + jnp.dot(m_ref[...], wo_ref[...], preferred_element_type=F32)
    y = _layer_norm(h, ln_ref[0:1, :], ln_ref[1:2, :])
    o_ref[...] = y
    obf_ref[...] = y.astype(BF16)


def _mixer_out(merged, x2d, wo, ln, *, alpha, tm):
    m, dm = x2d.shape
    return pl.pallas_call(
        functools.partial(_mixer_out_kernel, alpha=alpha),
        grid=(m // tm,),
        in_specs=[
            pl.BlockSpec((tm, dm), lambda i: (i, 0)),
            pl.BlockSpec((tm, dm), lambda i: (i, 0)),
            pl.BlockSpec((dm, dm), lambda i: (0, 0)),
            pl.BlockSpec((SUBLANES, dm), lambda i: (0, 0)),
        ],
        out_specs=[pl.BlockSpec((tm, dm), lambda i: (i, 0))] * 2,
        out_shape=[jax.ShapeDtypeStruct((m, dm), F32), jax.ShapeDtypeStruct((m, dm), BF16)],
        compiler_params=pltpu.CompilerParams(
            dimension_semantics=("arbitrary",), vmem_limit_bytes=VMEM_LIMIT_BYTES),
        name="mixer_out",
    )(merged, x2d, wo, ln)


def _ffn_kernel(xb_ref, x_ref, wg_ref, wu_ref, wd_ref, ln_ref, o_ref, acc_ref, *, alpha):
    f = pl.program_id(1)

    @pl.when(jnp.logical_and(pl.program_id(0) == 0, f == 0))
    def _():
        acc_ref[...] = jnp.zeros_like(acc_ref)

    xb = xb_ref[...]
    gate = jnp.dot(xb, wg_ref[...], preferred_element_type=F32)
    up = jnp.dot(xb, wu_ref[...], preferred_element_type=F32)
    hid = (gate * jax.nn.sigmoid(gate) * up).astype(BF16)
    part = jnp.dot(hid, wd_ref[...], preferred_element_type=F32)
    acc_ref[...] += part

    @pl.when(f == pl.num_programs(1) - 1)
    def _():
        h = alpha * x_ref[...] + acc_ref[...]
        o_ref[...] = _layer_norm(h, ln_ref[0:1, :], ln_ref[1:2, :])
        acc_ref[...] = jnp.zeros_like(acc_ref)


def _ffn(x_bf, x2d, w_gu, w_down, ln, *, alpha, tm, tf):
    m, dm = x2d.shape
    dff = w_down.shape[0]
    nf = dff // tf
    return pl.pallas_call(
        functools.partial(_ffn_kernel, alpha=alpha),
        grid=(m // tm, nf),
        in_specs=[
            pl.BlockSpec((tm, dm), lambda i, f: (i, 0)),
            pl.BlockSpec((tm, dm), lambda i, f: (i, 0)),
            pl.BlockSpec((dm, tf), lambda i, f: (0, f)),
            pl.BlockSpec((dm, tf), lambda i, f: (0, nf + f)),
            pl.BlockSpec((tf, dm), lambda i, f: (f, 0)),
            pl.BlockSpec((SUBLANES, dm), lambda i, f: (0, 0)),
        ],
        out_specs=pl.BlockSpec((tm, dm), lambda i, f: (i, 0)),
        out_shape=jax.ShapeDtypeStruct((m, dm), F32),
        scratch_shapes=[pltpu.VMEM((tm, dm), F32)],
        compiler_params=pltpu.CompilerParams(
            dimension_semantics=("arbitrary", "arbitrary"), vmem_limit_bytes=VMEM_LIMIT_BYTES),
        name="ffn",
    )(x_bf, x2d, w_gu, w_gu, w_down, ln)


def _pad_rows(a, n):
    return jnp.pad(a, ((0, n - a.shape[0]), (0, 0)))


def _pick(n, candidates):
    for c in candidates:
        if n % c == 0:
            return c
    raise ValueError(f"no tile size for extent {n}")


def kernel(x, w_in, shift_mu, conv_w, w0, w_up, a0, a_up, g_up, k_k, k_a, r_k,
           gn_g, gn_b, w_o, ln1_g, ln1_b, w_gu, w_down, ln2_g, ln2_b):
    bsz, t, dm = x.shape
    depth = w_in.shape[0]
    lw_n, la_n, lg_n = w_up.shape[1], a_up.shape[1], g_up.shape[1]
    assert dm % (2 * LANES) == 0 and t % CHUNK == 0
    assert lw_n <= LANES and la_n <= LANES and lg_n <= 2 * LANES
    assert w_in.shape[2] == 8 * dm + lw_n + la_n + lg_n
    alpha = (2.0 * depth) ** 0.25
    tn = 2 * LANES
    tm_in = _pick(t, (512, 256, 128, 64))
    tb = _pick(t, (256, 128, 64))
    tw = _pick(dm, (8 * LANES, 4 * LANES, 2 * LANES))
    m = bsz * t
    tm = _pick(m, (512, 256, 128, 64))
    tf = _pick(w_down.shape[1], (512, 256, 128))

    for l in range(depth):
        wi = w_in[l]
        c0 = 6 * dm
        cols = [wi[:, 0:dm], wi[:, dm:2 * dm], wi[:, 2 * dm:3 * dm],
                wi[:, c0 + lw_n + la_n + lg_n:c0 + lw_n + la_n + lg_n + dm],
                wi[:, 3 * dm:4 * dm], wi[:, 4 * dm:5 * dm], wi[:, 5 * dm:6 * dm],
                wi[:, c0 + lw_n + la_n + lg_n + dm:]]
        w_groups = [c.astype(BF16) for c in cols]

        def lane_pad(a, n):
            return jnp.pad(a, ((0, 0), (0, n - a.shape[1])))

        wl = jnp.concatenate([
            lane_pad(wi[:, c0:c0 + lw_n], LANES),
            lane_pad(wi[:, c0 + lw_n:c0 + lw_n + la_n], LANES),
            lane_pad(wi[:, c0 + lw_n + la_n:c0 + lw_n + la_n + lg_n], 2 * LANES)],
            axis=1).astype(BF16)
        mu = shift_mu[l]
        s0 = 3 * dm
        mu_l = jnp.concatenate([
            jnp.pad(mu[s0:s0 + lw_n], (0, LANES - lw_n)),
            jnp.pad(mu[s0 + lw_n:s0 + lw_n + la_n], (0, LANES - la_n)),
            jnp.pad(mu[s0 + lw_n + la_n:], (0, 2 * LANES - lg_n))])[None, :]
        wup = jnp.concatenate([_pad_rows(w_up[l], LANES), _pad_rows(a_up[l], LANES),
                               _pad_rows(g_up[l], 2 * LANES)], axis=0).astype(BF16)
        prm = _pad_rows(jnp.stack([
            mu[0:dm], mu[dm:2 * dm], mu[2 * dm:3 * dm],
            conv_w[l, 0], conv_w[l, 1], conv_w[l, 2],
            w0[l], a0[l], k_k[l], k_a[l], r_k[l].reshape(-1)]), 16)

        r, lw, k, v, kk, a, c1, og = _mixer_in(
            x, w_groups, wl, wup, prm, mu_l, tm=tm_in, tn=tn)
        gn = _pad_rows(jnp.stack([gn_g[l], gn_b[l]]), SUBLANES)
        merged = _wkv(r, lw, k, v, kk, a, c1, og, gn, tb=tb, tw=tw)
        ln1 = _pad_rows(jnp.stack([ln1_g[l], ln1_b[l]]), SUBLANES)
        x1, x1_bf = _mixer_out(merged.reshape(m, dm), x.reshape(m, dm), w_o[l].astype(BF16),
                               ln1, alpha=alpha, tm=tm)
        ln2 = _pad_rows(jnp.stack([ln2_g[l], ln2_b[l]]), SUBLANES)
        x = _ffn(x1_bf, x1, w_gu[l].astype(BF16), w_down[l].astype(BF16), ln2,
                 alpha=alpha, tm=tm, tf=tf).reshape(bsz, t, dm)
    return x
```

```python
import functools
import math

import jax
import jax.numpy as jnp
from jax import lax
from jax.experimental import pallas as pl
from jax.experimental.pallas import tpu as pltpu

HEAD_SIZE = 64
LOG2_HEAD = 6
LOG_DECAY_SCALE = -math.exp(-0.5)
LN_EPS = 1e-5
GN_EPS = 64e-5
LANES = 128
SUBLANES = 8
CHUNK = 64
VMEM_LIMIT_BYTES = 56 * 1024 * 1024

F32 = jnp.float32
BF16 = jnp.bfloat16


def _mm(a, b):
    return jnp.dot(a.astype(BF16), b.astype(BF16), preferred_element_type=F32)


def _mm_nt(a, b):
    return lax.dot_general(a.astype(BF16), b.astype(BF16), (((1,), (1,)), ((), ())),
                           preferred_element_type=F32)


def _split2(a):
    hi = a.astype(BF16)
    lo = (a - hi.astype(F32)).astype(BF16)
    return hi, lo


def _mm_exact_lhs(a_exact, b):
    d = functools.partial(jnp.dot, preferred_element_type=F32)
    h, l = _split2(b)
    return d(a_exact, h) + d(a_exact, l)


def _iota_shr(shape, dim, log2_div):
    return lax.shift_right_logical(lax.broadcasted_iota(jnp.int32, shape, dim), log2_div)


def _head_ones(n):
    r = _iota_shr((n, n), 0, LOG2_HEAD)
    c = _iota_shr((n, n), 1, LOG2_HEAD)
    return jnp.where(r == c, 1.0, 0.0).astype(BF16)


N_GROUPS = 8
N_CARRY = 4
LORA_PAD = 512


def _mixer_in_kernel(x_ref, *refs, tm, tn):
    w_refs = refs[:N_GROUPS]
    wl_ref, wup_ref, p_ref, mul_ref = refs[N_GROUPS:N_GROUPS + 4]
    r_ref, lw_ref, k_ref, v_ref, kk_ref, a_ref, c1_ref, og_ref = refs[N_GROUPS + 4:N_GROUPS + 12]
    xb_ref, lora_ref, carry_ref, carry_l_ref = refs[N_GROUPS + 12:]
    i = pl.program_id(1)
    j = pl.program_id(2)
    first = i == 0
    row = lax.broadcasted_iota(jnp.int32, (tm, tn), 0)

    @pl.when(j == 0)
    def _():
        xb_ref[...] = x_ref[0].astype(BF16)

    xt = xb_ref[...]

    def prev1(p, c8):
        return jnp.where(row == 0, c8[SUBLANES - 1:SUBLANES, :], pltpu.roll(p, 1, 0))

    def prev2(p, c8):
        rolled = pltpu.roll(p, 2, 0)
        rolled = jnp.where(row == 1, c8[SUBLANES - 1:SUBLANES, :], rolled)
        return jnp.where(row == 0, c8[SUBLANES - 2:SUBLANES - 1, :], rolled)

    @pl.when(j == 0)
    def _():
        pl_ = jnp.dot(xt, wl_ref[...], preferred_element_type=F32)
        cl = jnp.where(first, 0.0, carry_l_ref[...])
        rowl = lax.broadcasted_iota(jnp.int32, (tm, LORA_PAD), 0)
        prev = jnp.where(rowl == 0, cl[SUBLANES - 1:SUBLANES, :], pltpu.roll(pl_, 1, 0))
        carry_l_ref[...] = pl_[tm - SUBLANES:, :]
        z = pl_ + mul_ref[...] * (prev - pl_)
        lora_ref[:, 0:LANES] = jnp.tanh(z[:, 0:LANES]).astype(BF16)
        lora_ref[:, LANES:2 * LANES] = z[:, LANES:2 * LANES].astype(BF16)
        lora_ref[:, 2 * LANES:] = jax.nn.sigmoid(z[:, 2 * LANES:]).astype(BF16)

    p_cb, p_cc, p_ch, p_gc, p_r, p_k, p_v, p_gr = (
        jnp.dot(xt, w_ref[...], preferred_element_type=F32) for w_ref in w_refs)
    prm = p_ref[...]
    mu_r, mu_k, mu_v = prm[0:1], prm[1:2], prm[2:3]
    cw0, cw1, cw2 = prm[3:4], prm[4:5], prm[5:6]
    w0, a0, k_k, k_a, r_k = prm[6:7], prm[7:8], prm[8:9], prm[9:10], prm[10:11]

    carry = [jnp.where(first, 0.0, carry_ref[j, q]) for q in range(N_CARRY)]

    u = p_cc * p_ch
    conv = cw2 * u + cw1 * prev1(u, carry[0]) + cw0 * prev2(u, carry[0])
    y_conv = jax.nn.sigmoid(p_gc) * (p_cb * conv)

    r = p_r + mu_r * (prev1(p_r, carry[1]) - p_r)
    k = p_k + mu_k * (prev1(p_k, carry[2]) - p_k)
    v = p_v + mu_v * (prev1(p_v, carry[3]) - p_v)
    for q, val in enumerate((u, p_r, p_k, p_v)):
        carry_ref[j, q] = val[tm - SUBLANES:, :]

    wup = wup_ref[...]
    d = functools.partial(jnp.dot, preferred_element_type=F32)
    w_pre = w0 + d(lora_ref[:, 0:LANES], wup[0:LANES])
    a_pre = a0 + d(lora_ref[:, LANES:2 * LANES], wup[LANES:2 * LANES])
    g = d(lora_ref[:, 2 * LANES:], wup[2 * LANES:])
    lw = LOG_DECAY_SCALE * jax.nn.sigmoid(w_pre)
    a = jax.nn.sigmoid(a_pre)

    ones = _head_ones(tn)
    kraw = k * k_k
    ss = _mm(kraw * kraw, ones)
    kk = kraw / jnp.maximum(jnp.sqrt(ss), 1e-12)
    k_mod = k * (1.0 + (a - 1.0) * k_a)
    bonus = _mm(r * k_mod * r_k, ones) * v
    og = g * jax.nn.sigmoid(p_gr)

    r_ref[0] = r
    lw_ref[0] = lw
    k_ref[0] = k_mod
    v_ref[0] = v.astype(v_ref.dtype)
    kk_ref[0] = kk
    a_ref[0] = a
    c1_ref[0] = (y_conv + og * bonus).astype(c1_ref.dtype)
    og_ref[0] = og.astype(og_ref.dtype)


def _mixer_in(x, w_groups, wl, wup, prm, mu_l, *, tm, tn):
    bsz, t, dm = x.shape
    nj = dm // tn
    grid = (bsz, t // tm, nj)
    out_sds = jax.ShapeDtypeStruct((bsz, t, dm), F32)
    out_bf = jax.ShapeDtypeStruct((bsz, t, dm), BF16)
    out_spec = pl.BlockSpec((1, tm, tn), lambda b, i, j: (b, i, j))
    return pl.pallas_call(
        functools.partial(_mixer_in_kernel, tm=tm, tn=tn),
        grid=grid,
        in_specs=[
            pl.BlockSpec((1, tm, dm), lambda b, i, j: (b, i, 0)),
            *[pl.BlockSpec((dm, tn), lambda b, i, j, o=o: (0, o + j)) for _, o in w_groups],
            pl.BlockSpec((dm, LORA_PAD), lambda b, i, j: (0, 0)),
            pl.BlockSpec((LORA_PAD, tn), lambda b, i, j: (0, j)),
            pl.BlockSpec((16, tn), lambda b, i, j: (0, j)),
            pl.BlockSpec((1, LORA_PAD), lambda b, i, j: (0, 0)),
        ],
        out_specs=[out_spec] * 8,
        out_shape=[out_sds, out_sds, out_sds, out_bf, out_sds, out_sds, out_bf, out_bf],
        scratch_shapes=[
            pltpu.VMEM((tm, dm), BF16),
            pltpu.VMEM((tm, LORA_PAD), BF16),
            pltpu.VMEM((nj, N_CARRY, SUBLANES, tn), F32),
            pltpu.VMEM((SUBLANES, LORA_PAD), F32),
        ],
        compiler_params=pltpu.CompilerParams(
            dimension_semantics=("arbitrary", "arbitrary", "arbitrary"),
            vmem_limit_bytes=VMEM_LIMIT_BYTES),
        name="mixer_in",
    )(x, *[w for w, _ in w_groups], wl, wup, prm, mu_l)


def _wkv_block(r, lw, k, v, kk, a, z0, consts, n_chunks):
    cs = consts
    c = CHUNK
    idx = range(len(r))

    def stack(x):
        xb = x.astype(BF16)
        m0, m1 = (cs["m0"], cs["m1"]) if x.shape[1] == LANES else (cs["m0_2"], cs["m1_2"])
        return jnp.concatenate([xb * m0, xb * m1], axis=0)

    def bdiag(xw):
        xb = xw.astype(BF16)
        return jnp.concatenate([xb, xb], axis=0) * cs["bd_bf"]

    def each(fn, *lists):
        return [fn(*args) for args in zip(*lists)]

    cum = each(lambda x: _mm_exact_lhs(cs["tri"], x), lw)
    beta = each(lambda x, y: x * y, kk, a)
    e_inv = each(lambda x: jnp.exp(-x), cum)
    e_rem = each(lambda x: jnp.exp(x[c - 1:c, :] - x), cum)
    rt = each(lambda x, y: x * jnp.exp(y), r, cum)
    at = each(lambda x, y, z_: -x * jnp.exp(y - z_), kk, cum, lw)
    kt = each(lambda x, y: x * y, k, e_inv)
    bt = each(lambda x, y: x * y, beta, e_inv)
    kh = each(lambda x, y: x * y, k, e_rem)
    bh = each(lambda x, y: x * y, beta, e_rem)

    s = each(lambda a_, r_, b_, k_: _mm_nt(jnp.concatenate([a_, r_], axis=0),
                                           jnp.concatenate([stack(b_), stack(k_)], axis=0)),
             at, rt, bt, kt)
    lab = each(lambda x: jnp.where(cs["strict"], x[0:c, 0:2 * c], 0.0), s)
    urb = each(lambda x: jnp.where(cs["incl"], x[c:, 0:2 * c], 0.0), s)

    def twice(x):
        xb = x.astype(BF16)
        return jnp.concatenate([xb, xb], axis=0)

    lab2 = each(twice, lab)
    l8 = each(lambda x: jnp.where(cs["blk8"], x, 0.0), lab)
    l8_2 = each(lambda x, x2: _mm(x, x2 * cs["blk8_bd"]), l8, lab2)
    l8_4 = each(lambda x: _mm(x, bdiag(x)), l8_2)
    tw = each(lambda x, y: _mm(cs["eye"] + x, bdiag(cs["eye"] + y)), l8, l8_2)
    tw = each(lambda x, y: _mm(x, bdiag(cs["eye"] + y)), tw, l8_4)
    for join_bd in cs["joins_bd"]:
        half = each(lambda t_, x2: _mm(t_, x2 * join_bd), tw, lab2)
        tw = each(lambda t_, h_: t_ + _mm(h_, bdiag(t_)), tw, half)

    nv_uv = each(lambda x, v_: _mm(jnp.where(cs["strict_incl"], x[:, 2 * c:], 0.0), stack(v_)),
                 s, v)
    ah_d = each(lambda t_, a_, n_: _mm(t_, stack(jnp.concatenate([a_, n_[0:c]], axis=1))),
                tw, at, nv_uv)
    u2 = each(lambda u_, x: _mm(u_, stack(x)), urb, ah_d)
    rh = each(lambda r_, u_: r_ + u_[:, 0:LANES], rt, u2)
    e = each(lambda u_, w_: u_[:, LANES:] + w_[c:], u2, nv_uv)
    zeros = jnp.zeros((c, LANES), BF16)
    gf = each(lambda b_, k_, x, v_: _mm(
        jnp.concatenate([b_, k_], axis=0).T,
        jnp.concatenate([x.astype(BF16), jnp.concatenate([zeros, v_.astype(BF16)], axis=1)],
                        axis=0)), bh, kh, ah_d, v)
    gl = each(lambda x: jnp.where(cs["bd"], x[:, 0:LANES], 0.0), gf)
    f = each(lambda x: jnp.where(cs["bd"], x[:, LANES:], 0.0), gf)
    decay_col = each(lambda x: jnp.exp(x.T[:, c - 1:c]), cum)

    ys = [None] * len(r)
    states = list(z0)
    for ci in range(n_chunks):
        for p in range(len(states)):
            n = p * n_chunks + ci
            z = states[p]
            rz_gz = _mm(jnp.concatenate([rh[n], gl[n]], axis=0), z)
            ys[n] = rz_gz[0:c] + e[n]
            states[p] = decay_col[n] * z + rz_gz[c:] + f[n]
    return ys, states


def _wkv_consts():
    c = CHUNK
    assert c == HEAD_SIZE and 2 * c == LANES
    t = lax.broadcasted_iota(jnp.int32, (c, 2 * c), 0)
    i = jnp.bitwise_and(lax.broadcasted_iota(jnp.int32, (c, 2 * c), 1), c - 1)
    rr = lax.broadcasted_iota(jnp.int32, (c, c), 0)
    cc = lax.broadcasted_iota(jnp.int32, (c, c), 1)
    def head_lane_masks(width):
        lane = jnp.bitwise_and(lax.broadcasted_iota(jnp.int32, (c, width), 1), LANES - 1)
        return (jnp.where(lane < HEAD_SIZE, 1.0, 0.0).astype(BF16),
                jnp.where(lane < HEAD_SIZE, 0.0, 1.0).astype(BF16))

    m0, m1 = head_lane_masks(LANES)
    m0_2, m1_2 = head_lane_masks(2 * LANES)
    bd = _iota_shr((2 * c, 2 * c), 0, LOG2_HEAD) == _iota_shr((2 * c, 2 * c), 1, LOG2_HEAD)
    t2 = lax.broadcasted_iota(jnp.int32, (2 * c, 2 * c), 0)
    i2 = jnp.bitwise_and(lax.broadcasted_iota(jnp.int32, (2 * c, 2 * c), 1), c - 1)
    strict_incl = jnp.logical_or(jnp.logical_and(t2 < c, i2 < t2),
                                 jnp.logical_and(t2 >= c, i2 <= t2 - c))
    tt = jnp.bitwise_and(t2, c - 1)
    blk2 = [lax.shift_right_logical(tt, s) == lax.shift_right_logical(i2, s) for s in (3, 4, 5)]
    out2 = [jnp.logical_not(b) for b in blk2]
    joins = [jnp.logical_and(blk2[1], out2[0]), jnp.logical_and(blk2[2], out2[1]), out2[2]]

    def bd_mask(m):
        return jnp.where(jnp.logical_and(bd, m), 1.0, 0.0).astype(BF16)

    return dict(
        strict=i < t, incl=i <= t, strict_incl=strict_incl,
        eye=jnp.where(i == t, 1.0, 0.0).astype(F32),
        blk8=lax.shift_right_logical(t, 3) == lax.shift_right_logical(i, 3),
        blk8_bd=bd_mask(blk2[0]), joins_bd=[bd_mask(m) for m in joins],
        tri=jnp.where(cc <= rr, 1.0, 0.0).astype(BF16),
        m0=m0, m1=m1, m0_2=m0_2, m1_2=m1_2,
        bd=bd, bd_bf=jnp.where(bd, 1.0, 0.0).astype(BF16))


def _wkv_kernel(r_ref, lw_ref, k_ref, v_ref, kk_ref, a_ref, c1_ref, og_ref, gn_ref,
                o_ref, z_ref, *, n_chunks, n_pairs):
    @pl.when(pl.program_id(2) == 0)
    def _():
        z_ref[...] = jnp.zeros_like(z_ref)

    def chunks(ref):
        return [ref[0, ci * CHUNK:(ci + 1) * CHUNK, p * LANES:(p + 1) * LANES]
                for p in range(n_pairs) for ci in range(n_chunks)]

    ys, states = _wkv_block(chunks(r_ref), chunks(lw_ref), chunks(k_ref), chunks(v_ref),
                            chunks(kk_ref), chunks(a_ref), [z_ref[p] for p in range(n_pairs)],
                            _wkv_consts(), n_chunks)
    ones = _head_ones(LANES)
    inv_n = 1.0 / HEAD_SIZE
    y = jnp.concatenate(ys, axis=0)
    mu = _mm(y, ones) * inv_n
    yc = y - mu
    var = _mm(yc * yc, ones) * inv_n
    yn = yc * lax.rsqrt(var + GN_EPS)
    tb = n_chunks * CHUNK
    for p in range(n_pairs):
        z_ref[p] = states[p]
        ls = slice(p * LANES, (p + 1) * LANES)
        yp = yn[p * tb:(p + 1) * tb] * gn_ref[0:1, ls] + gn_ref[1:2, ls]
        o_ref[0, :, ls] = (c1_ref[0, :, ls].astype(F32)
                           + og_ref[0, :, ls].astype(F32) * yp).astype(o_ref.dtype)


def _wkv(r, lw, k, v, kk, a, c1, og, gn, *, tb, tw):
    bsz, t, dm = r.shape
    n_pairs = tw // LANES
    grid = (bsz, dm // tw, t // tb)
    spec = pl.BlockSpec((1, tb, tw), lambda b, h, i: (b, i, h))
    return pl.pallas_call(
        functools.partial(_wkv_kernel, n_chunks=tb // CHUNK, n_pairs=n_pairs),
        grid=grid,
        in_specs=[spec] * 8 + [pl.BlockSpec((SUBLANES, tw), lambda b, h, i: (0, h))],
        out_specs=spec,
        out_shape=jax.ShapeDtypeStruct((bsz, t, dm), BF16),
        scratch_shapes=[pltpu.VMEM((n_pairs, LANES, LANES), F32)],
        compiler_params=pltpu.CompilerParams(
            dimension_semantics=("arbitrary", "arbitrary", "arbitrary"),
            vmem_limit_bytes=VMEM_LIMIT_BYTES),
        name="wkv",
    )(r, lw, k, v, kk, a, c1, og, gn)


def _layer_norm(h, g, b):
    mu = jnp.mean(h, axis=-1, keepdims=True)
    hc = h - mu
    var = jnp.mean(hc * hc, axis=-1, keepdims=True)
    return hc * lax.rsqrt(var + LN_EPS) * g + b


def _mixer_out_kernel(m_ref, x_ref, wo_ref, ln_ref, o_ref, obf_ref, *, alpha):
    h = alpha * x_ref[...] + jnp.dot(m_ref[...], wo_ref[...], preferred_element_type=F32)
    y = _layer_norm(h, ln_ref[0:1, :], ln_ref[1:2, :])
    o_ref[...] = y
    obf_ref[...] = y.astype(BF16)


def _mixer_out(merged, x2d, wo, ln, *, alpha, tm):
    m, dm = x2d.shape
    return pl.pallas_call(
        functools.partial(_mixer_out_kernel, alpha=alpha),
        grid=(m // tm,),
        in_specs=[
            pl.BlockSpec((tm, dm), lambda i: (i, 0)),
            pl.BlockSpec((tm, dm), lambda i: (i, 0)),
            pl.BlockSpec((dm, dm), lambda i: (0, 0)),
            pl.BlockSpec((SUBLANES, dm), lambda i: (0, 0)),
        ],
        out_specs=[pl.BlockSpec((tm, dm), lambda i: (i, 0))] * 2,
        out_shape=[jax.ShapeDtypeStruct((m, dm), F32), jax.ShapeDtypeStruct((m, dm), BF16)],
        compiler_params=pltpu.CompilerParams(
            dimension_semantics=("arbitrary",), vmem_limit_bytes=VMEM_LIMIT_BYTES),
        name="mixer_out",
    )(merged, x2d, wo, ln)


def _ffn_kernel(xb_ref, x_ref, wg_ref, wu_ref, wd_ref, ln_ref, o_ref, acc_ref, *, alpha):
    f = pl.program_id(1)

    @pl.when(jnp.logical_and(pl.program_id(0) == 0, f == 0))
    def _():
        acc_ref[...] = jnp.zeros_like(acc_ref)

    xb = xb_ref[...]
    gate = jnp.dot(xb, wg_ref[...], preferred_element_type=F32)
    up = jnp.dot(xb, wu_ref[...], preferred_element_type=F32)
    hid = (gate * jax.nn.sigmoid(gate) * up).astype(BF16)
    part = jnp.dot(hid, wd_ref[...], preferred_element_type=F32)
    acc_ref[...] += part

    @pl.when(f == pl.num_programs(1) - 1)
    def _():
        h = alpha * x_ref[...] + acc_ref[...]
        o_ref[...] = _layer_norm(h, ln_ref[0:1, :], ln_ref[1:2, :])
        acc_ref[...] = jnp.zeros_like(acc_ref)


def _ffn(x_bf, x2d, w_gu, w_down, ln, *, alpha, tm, tf):
    m, dm = x2d.shape
    dff = w_down.shape[0]
    nf = dff // tf
    return pl.pallas_call(
        functools.partial(_ffn_kernel, alpha=alpha),
        grid=(m // tm, nf),
        in_specs=[
            pl.BlockSpec((tm, dm), lambda i, f: (i, 0)),
            pl.BlockSpec((tm, dm), lambda i, f: (i, 0)),
            pl.BlockSpec((dm, tf), lambda i, f: (0, f)),
            pl.BlockSpec((dm, tf), lambda i, f: (0, nf + f)),
            pl.BlockSpec((tf, dm), lambda i, f: (f, 0)),
            pl.BlockSpec((SUBLANES, dm), lambda i, f: (0, 0)),
        ],
        out_specs=pl.BlockSpec((tm, dm), lambda i, f: (i, 0)),
        out_shape=jax.ShapeDtypeStruct((m, dm), F32),
        scratch_shapes=[pltpu.VMEM((tm, dm), F32)],
        compiler_params=pltpu.CompilerParams(
            dimension_semantics=("arbitrary", "arbitrary"), vmem_limit_bytes=VMEM_LIMIT_BYTES),
        name="ffn",
    )(x_bf, x2d, w_gu, w_gu, w_down, ln)


def _pad_rows(a, n):
    return jnp.pad(a, ((0, n - a.shape[0]), (0, 0)))


def _pick(n, candidates):
    for c in candidates:
        if n % c == 0:
            return c
    raise ValueError(f"no tile size for extent {n}")


def kernel(x, w_in, shift_mu, conv_w, w0, w_up, a0, a_up, g_up, k_k, k_a, r_k,
           gn_g, gn_b, w_o, ln1_g, ln1_b, w_gu, w_down, ln2_g, ln2_b):
    bsz, t, dm = x.shape
    depth = w_in.shape[0]
    lw_n, la_n, lg_n = w_up.shape[1], a_up.shape[1], g_up.shape[1]
    assert dm % (2 * LANES) == 0 and t % CHUNK == 0
    assert lw_n <= LANES and la_n <= LANES and lg_n <= 2 * LANES
    assert w_in.shape[2] == 8 * dm + lw_n + la_n + lg_n
    alpha = (2.0 * depth) ** 0.25
    tn = 2 * LANES
    tm_in = _pick(t, (512, 256, 128, 64))
    tb = _pick(t, (256, 128, 64))
    tw = _pick(dm, (8 * LANES, 4 * LANES, 2 * LANES))
    m = bsz * t
    tm = _pick(m, (512, 256, 128, 64))
    tf = _pick(w_down.shape[1], (512, 256, 128))

    for l in range(depth):
        wi = w_in[l]
        c0 = 6 * dm
        wi_bf = wi.astype(BF16)
        g0 = c0 + lw_n + la_n + lg_n
        nj = dm // tn
        w_groups = [(wi_bf, 0), (wi_bf, nj), (wi_bf, 2 * nj), (wi_bf[:, g0:g0 + dm], 0),
                    (wi_bf, 3 * nj), (wi_bf, 4 * nj), (wi_bf, 5 * nj), (wi_bf[:, g0 + dm:], 0)]

        def lane_pad(a, n):
            return jnp.pad(a, ((0, 0), (0, n - a.shape[1])))

        wl = jnp.concatenate([
            lane_pad(wi[:, c0:c0 + lw_n], LANES),
            lane_pad(wi[:, c0 + lw_n:c0 + lw_n + la_n], LANES),
            lane_pad(wi[:, c0 + lw_n + la_n:c0 + lw_n + la_n + lg_n], 2 * LANES)],
            axis=1).astype(BF16)
        mu = shift_mu[l]
        s0 = 3 * dm
        mu_l = jnp.concatenate([
            jnp.pad(mu[s0:s0 + lw_n], (0, LANES - lw_n)),
            jnp.pad(mu[s0 + lw_n:s0 + lw_n + la_n], (0, LANES - la_n)),
            jnp.pad(mu[s0 + lw_n + la_n:], (0, 2 * LANES - lg_n))])[None, :]
        wup = jnp.concatenate([_pad_rows(w_up[l], LANES), _pad_rows(a_up[l], LANES),
                               _pad_rows(g_up[l], 2 * LANES)], axis=0).astype(BF16)
        prm = _pad_rows(jnp.stack([
            mu[0:dm], mu[dm:2 * dm], mu[2 * dm:3 * dm],
            conv_w[l, 0], conv_w[l, 1], conv_w[l, 2],
            w0[l], a0[l], k_k[l], k_a[l], r_k[l].reshape(-1)]), 16)

        r, lw, k, v, kk, a, c1, og = _mixer_in(
            x, w_groups, wl, wup, prm, mu_l, tm=tm_in, tn=tn)
        gn = _pad_rows(jnp.stack([gn_g[l], gn_b[l]]), SUBLANES)
        merged = _wkv(r, lw, k, v, kk, a, c1, og, gn, tb=tb, tw=tw)
        ln1 = _pad_rows(jnp.stack([ln1_g[l], ln1_b[l]]), SUBLANES)
        x1, x1_bf = _mixer_out(merged.reshape(m, dm), x.reshape(m, dm), w_o[l].astype(BF16),
                               ln1, alpha=alpha, tm=tm)
        ln2 = _pad_rows(jnp.stack([ln2_g[l], ln2_b[l]]), SUBLANES)
        x = _ffn(x1_bf, x1, w_gu[l].astype(BF16), w_down[l].astype(BF16), ln2,
                 alpha=alpha, tm=tm, tf=tf).reshape(bsz, t, dm)
    return x
```

```python
import functools
import math

import jax
import jax.numpy as jnp
from jax import lax
from jax.experimental import pallas as pl
from jax.experimental.pallas import tpu as pltpu

HEAD_SIZE = 64
LOG2_HEAD = 6
LOG_DECAY_SCALE = -math.exp(-0.5)
LN_EPS = 1e-5
GN_EPS = 64e-5
LANES = 128
SUBLANES = 8
CHUNK = 64
VMEM_LIMIT_BYTES = 56 * 1024 * 1024

F32 = jnp.float32
BF16 = jnp.bfloat16


def _mm(a, b):
    return jnp.dot(a.astype(BF16), b.astype(BF16), preferred_element_type=F32)


def _mm_nt(a, b):
    return lax.dot_general(a.astype(BF16), b.astype(BF16), (((1,), (1,)), ((), ())),
                           preferred_element_type=F32)


def _split2(a):
    hi = a.astype(BF16)
    lo = (a - hi.astype(F32)).astype(BF16)
    return hi, lo


def _mm_exact_lhs(a_exact, b):
    d = functools.partial(jnp.dot, preferred_element_type=F32)
    h, l = _split2(b)
    return d(a_exact, h) + d(a_exact, l)


def _iota_shr(shape, dim, log2_div):
    return lax.shift_right_logical(lax.broadcasted_iota(jnp.int32, shape, dim), log2_div)


def _head_ones(n):
    r = _iota_shr((n, n), 0, LOG2_HEAD)
    c = _iota_shr((n, n), 1, LOG2_HEAD)
    return jnp.where(r == c, 1.0, 0.0).astype(BF16)


N_GROUPS = 8
N_CARRY = 4
LORA_PAD = 512


def _mixer_in_kernel(x_ref, *refs, tm, tn):
    w_ref, wl_ref, wup_ref, p_ref, mul_ref = refs[:5]
    r_ref, lw_ref, k_ref, v_ref, kk_ref, a_ref, c1_ref, og_ref = refs[5:13]
    xb_ref, lora_ref, carry_ref, carry_l_ref = refs[13:]
    i = pl.program_id(1)
    j = pl.program_id(2)
    first = i == 0
    row = lax.broadcasted_iota(jnp.int32, (tm, tn), 0)

    @pl.when(j == 0)
    def _():
        xb_ref[...] = x_ref[0].astype(BF16)

    xt = xb_ref[...]

    def prev1(p, c8):
        return jnp.where(row == 0, c8[SUBLANES - 1:SUBLANES, :], pltpu.roll(p, 1, 0))

    def prev2(p, c8):
        rolled = pltpu.roll(p, 2, 0)
        rolled = jnp.where(row == 1, c8[SUBLANES - 1:SUBLANES, :], rolled)
        return jnp.where(row == 0, c8[SUBLANES - 2:SUBLANES - 1, :], rolled)

    @pl.when(j == 0)
    def _():
        pl_ = jnp.dot(xt, wl_ref[...], preferred_element_type=F32)
        cl = jnp.where(first, 0.0, carry_l_ref[...])
        rowl = lax.broadcasted_iota(jnp.int32, (tm, LORA_PAD), 0)
        prev = jnp.where(rowl == 0, cl[SUBLANES - 1:SUBLANES, :], pltpu.roll(pl_, 1, 0))
        carry_l_ref[...] = pl_[tm - SUBLANES:, :]
        z = pl_ + mul_ref[...] * (prev - pl_)
        lora_ref[:, 0:LANES] = jnp.tanh(z[:, 0:LANES]).astype(BF16)
        lora_ref[:, LANES:2 * LANES] = z[:, LANES:2 * LANES].astype(BF16)
        lora_ref[:, 2 * LANES:] = jax.nn.sigmoid(z[:, 2 * LANES:]).astype(BF16)

    p_cb, p_cc, p_ch, p_gc, p_r, p_k, p_v, p_gr = (
        jnp.dot(xt, w_ref[0, :, g * tn:(g + 1) * tn], preferred_element_type=F32)
        for g in range(N_GROUPS))
    prm = p_ref[...]
    mu_r, mu_k, mu_v = prm[0:1], prm[1:2], prm[2:3]
    cw0, cw1, cw2 = prm[3:4], prm[4:5], prm[5:6]
    w0, a0, k_k, k_a, r_k = prm[6:7], prm[7:8], prm[8:9], prm[9:10], prm[10:11]

    carry = [jnp.where(first, 0.0, carry_ref[j, q]) for q in range(N_CARRY)]

    u = p_cc * p_ch
    conv = cw2 * u + cw1 * prev1(u, carry[0]) + cw0 * prev2(u, carry[0])
    y_conv = jax.nn.sigmoid(p_gc) * (p_cb * conv)

    r = p_r + mu_r * (prev1(p_r, carry[1]) - p_r)
    k = p_k + mu_k * (prev1(p_k, carry[2]) - p_k)
    v = p_v + mu_v * (prev1(p_v, carry[3]) - p_v)
    for q, val in enumerate((u, p_r, p_k, p_v)):
        carry_ref[j, q] = val[tm - SUBLANES:, :]

    wup = wup_ref[...]
    d = functools.partial(jnp.dot, preferred_element_type=F32)
    w_pre = w0 + d(lora_ref[:, 0:LANES], wup[0:LANES])
    a_pre = a0 + d(lora_ref[:, LANES:2 * LANES], wup[LANES:2 * LANES])
    g = d(lora_ref[:, 2 * LANES:], wup[2 * LANES:])
    lw = LOG_DECAY_SCALE * jax.nn.sigmoid(w_pre)
    a = jax.nn.sigmoid(a_pre)

    ones = _head_ones(tn)
    kraw = k * k_k
    ss = _mm(kraw * kraw, ones)
    kk = kraw / jnp.maximum(jnp.sqrt(ss), 1e-12)
    k_mod = k * (1.0 + (a - 1.0) * k_a)
    bonus = _mm(r * k_mod * r_k, ones) * v
    og = g * jax.nn.sigmoid(p_gr)

    r_ref[0] = r
    lw_ref[0] = lw
    k_ref[0] = k_mod
    v_ref[0] = v.astype(v_ref.dtype)
    kk_ref[0] = kk
    a_ref[0] = a
    c1_ref[0] = (y_conv + og * bonus).astype(c1_ref.dtype)
    og_ref[0] = og.astype(og_ref.dtype)


def _mixer_in(x, w_tiles, wl, wup, prm, mu_l, *, tm, tn):
    bsz, t, dm = x.shape
    nj = dm // tn
    grid = (bsz, t // tm, nj)
    out_sds = jax.ShapeDtypeStruct((bsz, t, dm), F32)
    out_bf = jax.ShapeDtypeStruct((bsz, t, dm), BF16)
    out_spec = pl.BlockSpec((1, tm, tn), lambda b, i, j: (b, i, j))
    return pl.pallas_call(
        functools.partial(_mixer_in_kernel, tm=tm, tn=tn),
        grid=grid,
        in_specs=[
            pl.BlockSpec((1, tm, dm), lambda b, i, j: (b, i, 0)),
            pl.BlockSpec((1, dm, N_GROUPS * tn), lambda b, i, j: (j, 0, 0)),
            pl.BlockSpec((dm, LORA_PAD), lambda b, i, j: (0, 0)),
            pl.BlockSpec((LORA_PAD, tn), lambda b, i, j: (0, j)),
            pl.BlockSpec((16, tn), lambda b, i, j: (0, j)),
            pl.BlockSpec((1, LORA_PAD), lambda b, i, j: (0, 0)),
        ],
        out_specs=[out_spec] * 8,
        out_shape=[out_sds, out_sds, out_sds, out_bf, out_sds, out_sds, out_bf, out_bf],
        scratch_shapes=[
            pltpu.VMEM((tm, dm), BF16),
            pltpu.VMEM((tm, LORA_PAD), BF16),
            pltpu.VMEM((nj, N_CARRY, SUBLANES, tn), F32),
            pltpu.VMEM((SUBLANES, LORA_PAD), F32),
        ],
        compiler_params=pltpu.CompilerParams(
            dimension_semantics=("arbitrary", "arbitrary", "arbitrary"),
            vmem_limit_bytes=VMEM_LIMIT_BYTES),
        name="mixer_in",
    )(x, w_tiles, wl, wup, prm, mu_l)


def _wkv_block(r, lw, k, v, kk, a, z0, consts, n_chunks):
    cs = consts
    c = CHUNK
    idx = range(len(r))

    def stack(x):
        xb = x.astype(BF16)
        m0, m1 = (cs["m0"], cs["m1"]) if x.shape[1] == LANES else (cs["m0_2"], cs["m1_2"])
        return jnp.concatenate([xb * m0, xb * m1], axis=0)

    def bdiag(xw):
        xb = xw.astype(BF16)
        return jnp.concatenate([xb, xb], axis=0) * cs["bd_bf"]

    def each(fn, *lists):
        return [fn(*args) for args in zip(*lists)]

    cum = each(lambda x: _mm_exact_lhs(cs["tri"], x), lw)
    beta = each(lambda x, y: x * y, kk, a)
    e_inv = each(lambda x: jnp.exp(-x), cum)
    e_rem = each(lambda x: jnp.exp(x[c - 1:c, :] - x), cum)
    rt = each(lambda x, y: x * jnp.exp(y), r, cum)
    at = each(lambda x, y, z_: -x * jnp.exp(y - z_), kk, cum, lw)
    kt = each(lambda x, y: x * y, k, e_inv)
    bt = each(lambda x, y: x * y, beta, e_inv)
    kh = each(lambda x, y: x * y, k, e_rem)
    bh = each(lambda x, y: x * y, beta, e_rem)

    s = each(lambda a_, r_, b_, k_: _mm_nt(jnp.concatenate([a_, r_], axis=0),
                                           jnp.concatenate([stack(b_), stack(k_)], axis=0)),
             at, rt, bt, kt)
    lab = each(lambda x: jnp.where(cs["strict"], x[0:c, 0:2 * c], 0.0), s)
    urb = each(lambda x: jnp.where(cs["incl"], x[c:, 0:2 * c], 0.0), s)

    def twice(x):
        xb = x.astype(BF16)
        return jnp.concatenate([xb, xb], axis=0)

    lab2 = each(twice, lab)
    l8 = each(lambda x: jnp.where(cs["blk8"], x, 0.0), lab)
    l8_2 = each(lambda x, x2: _mm(x, x2 * cs["blk8_bd"]), l8, lab2)
    l8_4 = each(lambda x: _mm(x, bdiag(x)), l8_2)
    tw = each(lambda x, y: _mm(cs["eye"] + x, bdiag(cs["eye"] + y)), l8, l8_2)
    tw = each(lambda x, y: _mm(x, bdiag(cs["eye"] + y)), tw, l8_4)
    for join_bd in cs["joins_bd"]:
        half = each(lambda t_, x2: _mm(t_, x2 * join_bd), tw, lab2)
        tw = each(lambda t_, h_: t_ + _mm(h_, bdiag(t_)), tw, half)

    nv_uv = each(lambda x, v_: _mm(jnp.where(cs["strict_incl"], x[:, 2 * c:], 0.0), stack(v_)),
                 s, v)
    ah_d = each(lambda t_, a_, n_: _mm(t_, stack(jnp.concatenate([a_, n_[0:c]], axis=1))),
                tw, at, nv_uv)
    u2 = each(lambda u_, x: _mm(u_, stack(x)), urb, ah_d)
    rh = each(lambda r_, u_: r_ + u_[:, 0:LANES], rt, u2)
    e = each(lambda u_, w_: u_[:, LANES:] + w_[c:], u2, nv_uv)
    zeros = jnp.zeros((c, LANES), BF16)
    gf = each(lambda b_, k_, x, v_: _mm(
        jnp.concatenate([b_, k_], axis=0).T,
        jnp.concatenate([x.astype(BF16), jnp.concatenate([zeros, v_.astype(BF16)], axis=1)],
                        axis=0)), bh, kh, ah_d, v)
    gl = each(lambda x: jnp.where(cs["bd"], x[:, 0:LANES], 0.0), gf)
    f = each(lambda x: jnp.where(cs["bd"], x[:, LANES:], 0.0), gf)
    decay_col = each(lambda x: jnp.exp(x.T[:, c - 1:c]), cum)

    ys = [None] * len(r)
    states = list(z0)
    for ci in range(n_chunks):
        for p in range(len(states)):
            n = p * n_chunks + ci
            z = states[p]
            rz_gz = _mm(jnp.concatenate([rh[n], gl[n]], axis=0), z)
            ys[n] = rz_gz[0:c] + e[n]
            states[p] = decay_col[n] * z + rz_gz[c:] + f[n]
    return ys, states


def _wkv_consts():
    c = CHUNK
    assert c == HEAD_SIZE and 2 * c == LANES
    t = lax.broadcasted_iota(jnp.int32, (c, 2 * c), 0)
    i = jnp.bitwise_and(lax.broadcasted_iota(jnp.int32, (c, 2 * c), 1), c - 1)
    rr = lax.broadcasted_iota(jnp.int32, (c, c), 0)
    cc = lax.broadcasted_iota(jnp.int32, (c, c), 1)
    def head_lane_masks(width):
        lane = jnp.bitwise_and(lax.broadcasted_iota(jnp.int32, (c, width), 1), LANES - 1)
        return (jnp.where(lane < HEAD_SIZE, 1.0, 0.0).astype(BF16),
                jnp.where(lane < HEAD_SIZE, 0.0, 1.0).astype(BF16))

    m0, m1 = head_lane_masks(LANES)
    m0_2, m1_2 = head_lane_masks(2 * LANES)
    bd = _iota_shr((2 * c, 2 * c), 0, LOG2_HEAD) == _iota_shr((2 * c, 2 * c), 1, LOG2_HEAD)
    t2 = lax.broadcasted_iota(jnp.int32, (2 * c, 2 * c), 0)
    i2 = jnp.bitwise_and(lax.broadcasted_iota(jnp.int32, (2 * c, 2 * c), 1), c - 1)
    strict_incl = jnp.logical_or(jnp.logical_and(t2 < c, i2 < t2),
                                 jnp.logical_and(t2 >= c, i2 <= t2 - c))
    tt = jnp.bitwise_and(t2, c - 1)
    blk2 = [lax.shift_right_logical(tt, s) == lax.shift_right_logical(i2, s) for s in (3, 4, 5)]
    out2 = [jnp.logical_not(b) for b in blk2]
    joins = [jnp.logical_and(blk2[1], out2[0]), jnp.logical_and(blk2[2], out2[1]), out2[2]]

    def bd_mask(m):
        return jnp.where(jnp.logical_and(bd, m), 1.0, 0.0).astype(BF16)

    return dict(
        strict=i < t, incl=i <= t, strict_incl=strict_incl,
        eye=jnp.where(i == t, 1.0, 0.0).astype(F32),
        blk8=lax.shift_right_logical(t, 3) == lax.shift_right_logical(i, 3),
        blk8_bd=bd_mask(blk2[0]), joins_bd=[bd_mask(m) for m in joins],
        tri=jnp.where(cc <= rr, 1.0, 0.0).astype(BF16),
        m0=m0, m1=m1, m0_2=m0_2, m1_2=m1_2,
        bd=bd, bd_bf=jnp.where(bd, 1.0, 0.0).astype(BF16))


def _wkv_kernel(r_ref, lw_ref, k_ref, v_ref, kk_ref, a_ref, c1_ref, og_ref, gn_ref,
                o_ref, z_ref, *, n_chunks, n_pairs):
    @pl.when(pl.program_id(2) == 0)
    def _():
        z_ref[...] = jnp.zeros_like(z_ref)

    def chunks(ref):
        return [ref[0, ci * CHUNK:(ci + 1) * CHUNK, p * LANES:(p + 1) * LANES]
                for p in range(n_pairs) for ci in range(n_chunks)]

    ys, states = _wkv_block(chunks(r_ref), chunks(lw_ref), chunks(k_ref), chunks(v_ref),
                            chunks(kk_ref), chunks(a_ref), [z_ref[p] for p in range(n_pairs)],
                            _wkv_consts(), n_chunks)
    ones = _head_ones(LANES)
    inv_n = 1.0 / HEAD_SIZE
    y = jnp.concatenate(ys, axis=0)
    mu = _mm(y, ones) * inv_n
    yc = y - mu
    var = _mm(yc * yc, ones) * inv_n
    yn = yc * lax.rsqrt(var + GN_EPS)
    tb = n_chunks * CHUNK
    for p in range(n_pairs):
        z_ref[p] = states[p]
        ls = slice(p * LANES, (p + 1) * LANES)
        yp = yn[p * tb:(p + 1) * tb] * gn_ref[0:1, ls] + gn_ref[1:2, ls]
        o_ref[0, :, ls] = (c1_ref[0, :, ls].astype(F32)
                           + og_ref[0, :, ls].astype(F32) * yp).astype(o_ref.dtype)


def _wkv(r, lw, k, v, kk, a, c1, og, gn, *, tb, tw):
    bsz, t, dm = r.shape
    n_pairs = tw // LANES
    grid = (bsz, dm // tw, t // tb)
    spec = pl.BlockSpec((1, tb, tw), lambda b, h, i: (b, i, h))
    return pl.pallas_call(
        functools.partial(_wkv_kernel, n_chunks=tb // CHUNK, n_pairs=n_pairs),
        grid=grid,
        in_specs=[spec] * 8 + [pl.BlockSpec((SUBLANES, tw), lambda b, h, i: (0, h))],
        out_specs=spec,
        out_shape=jax.ShapeDtypeStruct((bsz, t, dm), BF16),
        scratch_shapes=[pltpu.VMEM((n_pairs, LANES, LANES), F32)],
        compiler_params=pltpu.CompilerParams(
            dimension_semantics=("arbitrary", "arbitrary", "arbitrary"),
            vmem_limit_bytes=VMEM_LIMIT_BYTES),
        name="wkv",
    )(r, lw, k, v, kk, a, c1, og, gn)


def _layer_norm(h, g, b):
    mu = jnp.mean(h, axis=-1, keepdims=True)
    hc = h - mu
    var = jnp.mean(hc * hc, axis=-1, keepdims=True)
    return hc * lax.rsqrt(var + LN_EPS) * g + b


def _mixer_out_kernel(m_ref, x_ref, wo_ref, ln_ref, o_ref, obf_ref, *, alpha):
    h = alpha * x_ref[...] + jnp.dot(m_ref[...], wo_ref[...], preferred_element_type=F32)
    y = _layer_norm(h, ln_ref[0:1, :], ln_ref[1:2, :])
    o_ref[...] = y
    obf_ref[...] = y.astype(BF16)


def _mixer_out(merged, x2d, wo, ln, *, alpha, tm):
    m, dm = x2d.shape
    return pl.pallas_call(
        functools.partial(_mixer_out_kernel, alpha=alpha),
        grid=(m // tm,),
        in_specs=[
            pl.BlockSpec((tm, dm), lambda i: (i, 0)),
            pl.BlockSpec((tm, dm), lambda i: (i, 0)),
            pl.BlockSpec((dm, dm), lambda i: (0, 0)),
            pl.BlockSpec((SUBLANES, dm), lambda i: (0, 0)),
        ],
        out_specs=[pl.BlockSpec((tm, dm), lambda i: (i, 0))] * 2,
        out_shape=[jax.ShapeDtypeStruct((m, dm), F32), jax.ShapeDtypeStruct((m, dm), BF16)],
        compiler_params=pltpu.CompilerParams(
            dimension_semantics=("arbitrary",), vmem_limit_bytes=VMEM_LIMIT_BYTES),
        name="mixer_out",
    )(merged, x2d, wo, ln)


def _ffn_kernel(xb_ref, x_ref, wg_ref, wu_ref, wd_ref, ln_ref, o_ref, acc_ref, *, alpha):
    f = pl.program_id(1)

    @pl.when(jnp.logical_and(pl.program_id(0) == 0, f == 0))
    def _():
        acc_ref[...] = jnp.zeros_like(acc_ref)

    xb = xb_ref[...]
    gate = jnp.dot(xb, wg_ref[...], preferred_element_type=F32)
    up = jnp.dot(xb, wu_ref[...], preferred_element_type=F32)
    hid = (gate * jax.nn.sigmoid(gate) * up).astype(BF16)
    part = jnp.dot(hid, wd_ref[...], preferred_element_type=F32)
    acc_ref[...] += part

    @pl.when(f == pl.num_programs(1) - 1)
    def _():
        h = alpha * x_ref[...] + acc_ref[...]
        o_ref[...] = _layer_norm(h, ln_ref[0:1, :], ln_ref[1:2, :])
        acc_ref[...] = jnp.zeros_like(acc_ref)


def _ffn(x_bf, x2d, w_gu, w_down, ln, *, alpha, tm, tf):
    m, dm = x2d.shape
    dff = w_down.shape[0]
    nf = dff // tf
    return pl.pallas_call(
        functools.partial(_ffn_kernel, alpha=alpha),
        grid=(m // tm, nf),
        in_specs=[
            pl.BlockSpec((tm, dm), lambda i, f: (i, 0)),
            pl.BlockSpec((tm, dm), lambda i, f: (i, 0)),
            pl.BlockSpec((dm, tf), lambda i, f: (0, f)),
            pl.BlockSpec((dm, tf), lambda i, f: (0, nf + f)),
            pl.BlockSpec((tf, dm), lambda i, f: (f, 0)),
            pl.BlockSpec((SUBLANES, dm), lambda i, f: (0, 0)),
        ],
        out_specs=pl.BlockSpec((tm, dm), lambda i, f: (i, 0)),
        out_shape=jax.ShapeDtypeStruct((m, dm), F32),
        scratch_shapes=[pltpu.VMEM((tm, dm), F32)],
        compiler_params=pltpu.CompilerParams(
            dimension_semantics=("arbitrary", "arbitrary"), vmem_limit_bytes=VMEM_LIMIT_BYTES),
        name="ffn",
    )(x_bf, x2d, w_gu, w_gu, w_down, ln)


def _pad_rows(a, n):
    return jnp.pad(a, ((0, n - a.shape[0]), (0, 0)))


def _pick(n, candidates):
    for c in candidates:
        if n % c == 0:
            return c
    raise ValueError(f"no tile size for extent {n}")


def kernel(x, w_in, shift_mu, conv_w, w0, w_up, a0, a_up, g_up, k_k, k_a, r_k,
           gn_g, gn_b, w_o, ln1_g, ln1_b, w_gu, w_down, ln2_g, ln2_b):
    bsz, t, dm = x.shape
    depth = w_in.shape[0]
    lw_n, la_n, lg_n = w_up.shape[1], a_up.shape[1], g_up.shape[1]
    assert dm % (2 * LANES) == 0 and t % CHUNK == 0
    assert lw_n <= LANES and la_n <= LANES and lg_n <= 2 * LANES
    assert w_in.shape[2] == 8 * dm + lw_n + la_n + lg_n
    alpha = (2.0 * depth) ** 0.25
    tn = 2 * LANES
    tm_in = _pick(t, (512, 256, 128, 64))
    tb = _pick(t, (256, 128, 64))
    tw = _pick(dm, (8 * LANES, 4 * LANES, 2 * LANES))
    m = bsz * t
    tm = _pick(m, (512, 256, 128, 64))
    tf = _pick(w_down.shape[1], (512, 256, 128))

    for l in range(depth):
        wi = w_in[l]
        c0 = 6 * dm
        g0 = c0 + lw_n + la_n + lg_n
        nj = dm // tn
        starts = (0, dm, 2 * dm, g0, 3 * dm, 4 * dm, 5 * dm, g0 + dm)
        w_tiles = jnp.concatenate(
            [wi[:, s:s + dm].astype(BF16).reshape(dm, nj, tn) for s in starts],
            axis=-1).transpose(1, 0, 2)

        def lane_pad(a, n):
            return jnp.pad(a, ((0, 0), (0, n - a.shape[1])))

        wl = jnp.concatenate([
            lane_pad(wi[:, c0:c0 + lw_n], LANES),
            lane_pad(wi[:, c0 + lw_n:c0 + lw_n + la_n], LANES),
            lane_pad(wi[:, c0 + lw_n + la_n:c0 + lw_n + la_n + lg_n], 2 * LANES)],
            axis=1).astype(BF16)
        mu = shift_mu[l]
        s0 = 3 * dm
        mu_l = jnp.concatenate([
            jnp.pad(mu[s0:s0 + lw_n], (0, LANES - lw_n)),
            jnp.pad(mu[s0 + lw_n:s0 + lw_n + la_n], (0, LANES - la_n)),
            jnp.pad(mu[s0 + lw_n + la_n:], (0, 2 * LANES - lg_n))])[None, :]
        wup = jnp.concatenate([_pad_rows(w_up[l], LANES), _pad_rows(a_up[l], LANES),
                               _pad_rows(g_up[l], 2 * LANES)], axis=0).astype(BF16)
        prm = _pad_rows(jnp.stack([
            mu[0:dm], mu[dm:2 * dm], mu[2 * dm:3 * dm],
            conv_w[l, 0], conv_w[l, 1], conv_w[l, 2],
            w0[l], a0[l], k_k[l], k_a[l], r_k[l].reshape(-1)]), 16)

        r, lw, k, v, kk, a, c1, og = _mixer_in(
            x, w_tiles, wl, wup, prm, mu_l, tm=tm_in, tn=tn)
        gn = _pad_rows(jnp.stack([gn_g[l], gn_b[l]]), SUBLANES)
        merged = _wkv(r, lw, k, v, kk, a, c1, og, gn, tb=tb, tw=tw)
        ln1 = _pad_rows(jnp.stack([ln1_g[l], ln1_b[l]]), SUBLANES)
        x1, x1_bf = _mixer_out(merged.reshape(m, dm), x.reshape(m, dm), w_o[l].astype(BF16),
                               ln1, alpha=alpha, tm=tm)
        ln2 = _pad_rows(jnp.stack([ln2_g[l], ln2_b[l]]), SUBLANES)
        x = _ffn(x1_bf, x1, w_gu[l].astype(BF16), w_down[l].astype(BF16), ln2,
                 alpha=alpha, tm=tm, tf=tf).reshape(bsz, t, dm)
    return x
```

```python
import functools
import math

import jax
import jax.numpy as jnp
from jax import lax
from jax.experimental import pallas as pl
from jax.experimental.pallas import tpu as pltpu

HEAD_SIZE = 64
LOG2_HEAD = 6
LOG_DECAY_SCALE = -math.exp(-0.5)
LN_EPS = 1e-5
GN_EPS = 64e-5
LANES = 128
SUBLANES = 8
CHUNK = 64
VMEM_LIMIT_BYTES = 56 * 1024 * 1024

F32 = jnp.float32
BF16 = jnp.bfloat16


def _mm(a, b):
    return jnp.dot(a.astype(BF16), b.astype(BF16), preferred_element_type=F32)


def _mm_nt(a, b):
    return lax.dot_general(a.astype(BF16), b.astype(BF16), (((1,), (1,)), ((), ())),
                           preferred_element_type=F32)


def _split2(a):
    hi = a.astype(BF16)
    lo = (a - hi.astype(F32)).astype(BF16)
    return hi, lo


def _mm_exact_lhs(a_exact, b):
    d = functools.partial(jnp.dot, preferred_element_type=F32)
    h, l = _split2(b)
    return d(a_exact, h) + d(a_exact, l)


def _iota_shr(shape, dim, log2_div):
    return lax.shift_right_logical(lax.broadcasted_iota(jnp.int32, shape, dim), log2_div)


def _head_ones(n):
    r = _iota_shr((n, n), 0, LOG2_HEAD)
    c = _iota_shr((n, n), 1, LOG2_HEAD)
    return jnp.where(r == c, 1.0, 0.0).astype(BF16)


N_GROUPS = 8
N_CARRY = 4
LORA_PAD = 512


def _mixer_in_kernel(x_ref, *refs, tm, tn):
    w_refs = refs[:N_GROUPS]
    wl_ref, wup_ref, p_ref, mul_ref = refs[N_GROUPS:N_GROUPS + 4]
    r_ref, lw_ref, k_ref, v_ref, kk_ref, a_ref, c1_ref, og_ref = refs[N_GROUPS + 4:N_GROUPS + 12]
    xb_ref, lora_ref, carry_ref, carry_l_ref = refs[N_GROUPS + 12:]
    i = pl.program_id(1)
    j = pl.program_id(2)
    first = i == 0
    row = lax.broadcasted_iota(jnp.int32, (tm, tn), 0)

    @pl.when(j == 0)
    def _():
        xb_ref[...] = x_ref[0].astype(BF16)

    xt = xb_ref[...]

    def prev1(p, c8):
        return jnp.where(row == 0, c8[SUBLANES - 1:SUBLANES, :], pltpu.roll(p, 1, 0))

    def prev2(p, c8):
        rolled = pltpu.roll(p, 2, 0)
        rolled = jnp.where(row == 1, c8[SUBLANES - 1:SUBLANES, :], rolled)
        return jnp.where(row == 0, c8[SUBLANES - 2:SUBLANES - 1, :], rolled)

    @pl.when(j == 0)
    def _():
        pl_ = jnp.dot(xt, wl_ref[...], preferred_element_type=F32)
        cl = jnp.where(first, 0.0, carry_l_ref[...])
        rowl = lax.broadcasted_iota(jnp.int32, (tm, LORA_PAD), 0)
        prev = jnp.where(rowl == 0, cl[SUBLANES - 1:SUBLANES, :], pltpu.roll(pl_, 1, 0))
        carry_l_ref[...] = pl_[tm - SUBLANES:, :]
        z = pl_ + mul_ref[...] * (prev - pl_)
        lora_ref[:, 0:LANES] = jnp.tanh(z[:, 0:LANES]).astype(BF16)
        lora_ref[:, LANES:2 * LANES] = z[:, LANES:2 * LANES].astype(BF16)
        lora_ref[:, 2 * LANES:] = jax.nn.sigmoid(z[:, 2 * LANES:]).astype(BF16)

    p_cb, p_cc, p_ch, p_gc, p_r, p_k, p_v, p_gr = (
        jnp.dot(xt, w_ref[...], preferred_element_type=F32) for w_ref in w_refs)
    prm = p_ref[...]
    mu_r, mu_k, mu_v = prm[0:1], prm[1:2], prm[2:3]
    cw0, cw1, cw2 = prm[3:4], prm[4:5], prm[5:6]
    w0, a0, k_k, k_a, r_k = prm[6:7], prm[7:8], prm[8:9], prm[9:10], prm[10:11]

    carry = [jnp.where(first, 0.0, carry_ref[j, q]) for q in range(N_CARRY)]

    u = p_cc * p_ch
    conv = cw2 * u + cw1 * prev1(u, carry[0]) + cw0 * prev2(u, carry[0])
    y_conv = jax.nn.sigmoid(p_gc) * (p_cb * conv)

    r = p_r + mu_r * (prev1(p_r, carry[1]) - p_r)
    k = p_k + mu_k * (prev1(p_k, carry[2]) - p_k)
    v = p_v + mu_v * (prev1(p_v, carry[3]) - p_v)
    for q, val in enumerate((u, p_r, p_k, p_v)):
        carry_ref[j, q] = val[tm - SUBLANES:, :]

    wup = wup_ref[...]
    d = functools.partial(jnp.dot, preferred_element_type=F32)
    w_pre = w0 + d(lora_ref[:, 0:LANES], wup[0:LANES])
    a_pre = a0 + d(lora_ref[:, LANES:2 * LANES], wup[LANES:2 * LANES])
    g = d(lora_ref[:, 2 * LANES:], wup[2 * LANES:])
    lw = LOG_DECAY_SCALE * jax.nn.sigmoid(w_pre)
    a = jax.nn.sigmoid(a_pre)

    ones = _head_ones(tn)
    kraw = k * k_k
    ss = _mm(kraw * kraw, ones)
    kk = kraw / jnp.maximum(jnp.sqrt(ss), 1e-12)
    k_mod = k * (1.0 + (a - 1.0) * k_a)
    bonus = _mm(r * k_mod * r_k, ones) * v
    og = g * jax.nn.sigmoid(p_gr)

    r_ref[0] = r
    lw_ref[0] = lw
    k_ref[0] = k_mod
    v_ref[0] = v.astype(v_ref.dtype)
    kk_ref[0] = kk
    a_ref[0] = a
    c1_ref[0] = (y_conv + og * bonus).astype(c1_ref.dtype)
    og_ref[0] = og.astype(og_ref.dtype)


def _mixer_in(x, w_groups, wl, wup, prm, mu_l, *, tm, tn):
    bsz, t, dm = x.shape
    nj = dm // tn
    grid = (bsz, t // tm, nj)
    out_sds = jax.ShapeDtypeStruct((bsz, t, dm), F32)
    out_bf = jax.ShapeDtypeStruct((bsz, t, dm), BF16)
    out_spec = pl.BlockSpec((1, tm, tn), lambda b, i, j: (b, i, j))
    return pl.pallas_call(
        functools.partial(_mixer_in_kernel, tm=tm, tn=tn),
        grid=grid,
        in_specs=[
            pl.BlockSpec((1, tm, dm), lambda b, i, j: (b, i, 0)),
            *[pl.BlockSpec((dm, tn), lambda b, i, j, o=o: (0, o + j)) for _, o in w_groups],
            pl.BlockSpec((dm, LORA_PAD), lambda b, i, j: (0, 0)),
            pl.BlockSpec((LORA_PAD, tn), lambda b, i, j: (0, j)),
            pl.BlockSpec((16, tn), lambda b, i, j: (0, j)),
            pl.BlockSpec((1, LORA_PAD), lambda b, i, j: (0, 0)),
        ],
        out_specs=[out_spec] * 8,
        out_shape=[out_sds, out_sds, out_sds, out_bf, out_sds, out_sds, out_bf, out_bf],
        scratch_shapes=[
            pltpu.VMEM((tm, dm), BF16),
            pltpu.VMEM((tm, LORA_PAD), BF16),
            pltpu.VMEM((nj, N_CARRY, SUBLANES, tn), F32),
            pltpu.VMEM((SUBLANES, LORA_PAD), F32),
        ],
        compiler_params=pltpu.CompilerParams(
            dimension_semantics=("arbitrary", "arbitrary", "arbitrary"),
            vmem_limit_bytes=VMEM_LIMIT_BYTES),
        name="mixer_in",
    )(x, *[w for w, _ in w_groups], wl, wup, prm, mu_l)


def _wkv_block(r, lw, k, v, kk, a, z0, consts, n_chunks):
    cs = consts
    c = CHUNK
    idx = range(len(r))

    def stack(x):
        xb = x.astype(BF16)
        m0, m1 = (cs["m0"], cs["m1"]) if x.shape[1] == LANES else (cs["m0_2"], cs["m1_2"])
        return jnp.concatenate([xb * m0, xb * m1], axis=0)

    def bdiag(xw):
        xb = xw.astype(BF16)
        return jnp.concatenate([xb, xb], axis=0) * cs["bd_bf"]

    def each(fn, *lists):
        return [fn(*args) for args in zip(*lists)]

    cum = each(lambda x: _mm_exact_lhs(cs["tri"], x), lw)
    beta = each(lambda x, y: x * y, kk, a)
    e_inv = each(lambda x: jnp.exp(-x), cum)
    e_rem = each(lambda x: jnp.exp(x[c - 1:c, :] - x), cum)
    rt = each(lambda x, y: x * jnp.exp(y), r, cum)
    at = each(lambda x, y, z_: -x * jnp.exp(y - z_), kk, cum, lw)
    kt = each(lambda x, y: x * y, k, e_inv)
    bt = each(lambda x, y: x * y, beta, e_inv)
    kh = each(lambda x, y: x * y, k, e_rem)
    bh = each(lambda x, y: x * y, beta, e_rem)

    s = each(lambda a_, r_, b_, k_: _mm_nt(jnp.concatenate([a_, r_], axis=0),
                                           jnp.concatenate([stack(b_), stack(k_)], axis=0)),
             at, rt, bt, kt)
    lab = each(lambda x: jnp.where(cs["strict"], x[0:c, 0:2 * c], 0.0), s)
    urb = each(lambda x: jnp.where(cs["incl"], x[c:, 0:2 * c], 0.0), s)

    def twice(x):
        xb = x.astype(BF16)
        return jnp.concatenate([xb, xb], axis=0)

    lab2 = each(twice, lab)
    l8 = each(lambda x: jnp.where(cs["blk8"], x, 0.0), lab)
    l8_2 = each(lambda x, x2: _mm(x, x2 * cs["blk8_bd"]), l8, lab2)
    l8_4 = each(lambda x: _mm(x, bdiag(x)), l8_2)
    tw = each(lambda x, y: _mm(cs["eye"] + x, bdiag(cs["eye"] + y)), l8, l8_2)
    tw = each(lambda x, y: _mm(x, bdiag(cs["eye"] + y)), tw, l8_4)
    for join_bd in cs["joins_bd"]:
        half = each(lambda t_, x2: _mm(t_, x2 * join_bd), tw, lab2)
        tw = each(lambda t_, h_: t_ + _mm(h_, bdiag(t_)), tw, half)

    nv_uv = each(lambda x, v_: _mm(jnp.where(cs["strict_incl"], x[:, 2 * c:], 0.0), stack(v_)),
                 s, v)
    ah_d = each(lambda t_, a_, n_: _mm(t_, stack(jnp.concatenate([a_, n_[0:c]], axis=1))),
                tw, at, nv_uv)
    u2 = each(lambda u_, x: _mm(u_, stack(x)), urb, ah_d)
    rh = each(lambda r_, u_: r_ + u_[:, 0:LANES], rt, u2)
    e = each(lambda u_, w_: u_[:, LANES:] + w_[c:], u2, nv_uv)
    zeros = jnp.zeros((c, LANES), BF16)
    gf = each(lambda b_, k_, x, v_: _mm(
        jnp.concatenate([b_, k_], axis=0).T,
        jnp.concatenate([x.astype(BF16), jnp.concatenate([zeros, v_.astype(BF16)], axis=1)],
                        axis=0)), bh, kh, ah_d, v)
    gl = each(lambda x: jnp.where(cs["bd"], x[:, 0:LANES], 0.0), gf)
    f = each(lambda x: jnp.where(cs["bd"], x[:, LANES:], 0.0), gf)
    decay_col = each(lambda x: jnp.exp(x.T[:, c - 1:c]), cum)

    ys = [None] * len(r)
    states = list(z0)
    for ci in range(n_chunks):
        for p in range(len(states)):
            n = p * n_chunks + ci
            z = states[p]
            rz_gz = _mm(jnp.concatenate([rh[n], gl[n]], axis=0), z)
            ys[n] = rz_gz[0:c] + e[n]
            states[p] = decay_col[n] * z + rz_gz[c:] + f[n]
    return ys, states


def _wkv_consts():
    c = CHUNK
    assert c == HEAD_SIZE and 2 * c == LANES
    t = lax.broadcasted_iota(jnp.int32, (c, 2 * c), 0)
    i = jnp.bitwise_and(lax.broadcasted_iota(jnp.int32, (c, 2 * c), 1), c - 1)
    rr = lax.broadcasted_iota(jnp.int32, (c, c), 0)
    cc = lax.broadcasted_iota(jnp.int32, (c, c), 1)
    def head_lane_masks(width):
        lane = jnp.bitwise_and(lax.broadcasted_iota(jnp.int32, (c, width), 1), LANES - 1)
        return (jnp.where(lane < HEAD_SIZE, 1.0, 0.0).astype(BF16),
                jnp.where(lane < HEAD_SIZE, 0.0, 1.0).astype(BF16))

    m0, m1 = head_lane_masks(LANES)
    m0_2, m1_2 = head_lane_masks(2 * LANES)
    bd = _iota_shr((2 * c, 2 * c), 0, LOG2_HEAD) == _iota_shr((2 * c, 2 * c), 1, LOG2_HEAD)
    t2 = lax.broadcasted_iota(jnp.int32, (2 * c, 2 * c), 0)
    i2 = jnp.bitwise_and(lax.broadcasted_iota(jnp.int32, (2 * c, 2 * c), 1), c - 1)
    strict_incl = jnp.logical_or(jnp.logical_and(t2 < c, i2 < t2),
                                 jnp.logical_and(t2 >= c, i2 <= t2 - c))
    tt = jnp.bitwise_and(t2, c - 1)
    blk2 = [lax.shift_right_logical(tt, s) == lax.shift_right_logical(i2, s) for s in (3, 4, 5)]
    out2 = [jnp.logical_not(b) for b in blk2]
    joins = [jnp.logical_and(blk2[1], out2[0]), jnp.logical_and(blk2[2], out2[1]), out2[2]]

    def bd_mask(m):
        return jnp.where(jnp.logical_and(bd, m), 1.0, 0.0).astype(BF16)

    return dict(
        strict=i < t, incl=i <= t, strict_incl=strict_incl,
        eye=jnp.where(i == t, 1.0, 0.0).astype(F32),
        blk8=lax.shift_right_logical(t, 3) == lax.shift_right_logical(i, 3),
        blk8_bd=bd_mask(blk2[0]), joins_bd=[bd_mask(m) for m in joins],
        tri=jnp.where(cc <= rr, 1.0, 0.0).astype(BF16),
        m0=m0, m1=m1, m0_2=m0_2, m1_2=m1_2,
        bd=bd, bd_bf=jnp.where(bd, 1.0, 0.0).astype(BF16))


def _wkv_kernel(r_ref, lw_ref, k_ref, v_ref, kk_ref, a_ref, c1_ref, og_ref, gn_ref,
                x_ref, wo_ref, ln_ref, o_ref, obf_ref, z_ref, *, n_chunks, n_pairs, alpha):
    @pl.when(pl.program_id(1) == 0)
    def _():
        z_ref[...] = jnp.zeros_like(z_ref)

    def chunks(ref):
        return [ref[0, ci * CHUNK:(ci + 1) * CHUNK, p * LANES:(p + 1) * LANES]
                for p in range(n_pairs) for ci in range(n_chunks)]

    ys, states = _wkv_block(chunks(r_ref), chunks(lw_ref), chunks(k_ref), chunks(v_ref),
                            chunks(kk_ref), chunks(a_ref), [z_ref[p] for p in range(n_pairs)],
                            _wkv_consts(), n_chunks)
    ones = _head_ones(LANES)
    inv_n = 1.0 / HEAD_SIZE
    y = jnp.concatenate(ys, axis=0)
    mu = _mm(y, ones) * inv_n
    yc = y - mu
    var = _mm(yc * yc, ones) * inv_n
    yn = yc * lax.rsqrt(var + GN_EPS)
    tb = n_chunks * CHUNK
    merged = []
    for p in range(n_pairs):
        z_ref[p] = states[p]
        ls = slice(p * LANES, (p + 1) * LANES)
        yp = yn[p * tb:(p + 1) * tb] * gn_ref[0:1, ls] + gn_ref[1:2, ls]
        merged.append((c1_ref[0, :, ls].astype(F32)
                       + og_ref[0, :, ls].astype(F32) * yp).astype(BF16))
    h = alpha * x_ref[0] + jnp.dot(jnp.concatenate(merged, axis=1), wo_ref[...],
                                   preferred_element_type=F32)
    x1 = _layer_norm(h, ln_ref[0:1, :], ln_ref[1:2, :])
    o_ref[0] = x1
    obf_ref[0] = x1.astype(BF16)


def _layer_norm(h, g, b):
    mu = jnp.mean(h, axis=-1, keepdims=True)
    hc = h - mu
    var = jnp.mean(hc * hc, axis=-1, keepdims=True)
    return hc * lax.rsqrt(var + LN_EPS) * g + b


def _wkv(r, lw, k, v, kk, a, c1, og, gn, x, wo, ln, *, alpha, tb):
    bsz, t, dm = r.shape
    n_pairs = dm // LANES
    spec = pl.BlockSpec((1, tb, dm), lambda b, i: (b, i, 0))
    row_spec = pl.BlockSpec((SUBLANES, dm), lambda b, i: (0, 0))
    return pl.pallas_call(
        functools.partial(_wkv_kernel, n_chunks=tb // CHUNK, n_pairs=n_pairs, alpha=alpha),
        grid=(bsz, t // tb),
        in_specs=[spec] * 8 + [row_spec, spec, pl.BlockSpec((dm, dm), lambda b, i: (0, 0)),
                               row_spec],
        out_specs=[spec, spec],
        out_shape=[jax.ShapeDtypeStruct((bsz, t, dm), F32),
                   jax.ShapeDtypeStruct((bsz, t, dm), BF16)],
        scratch_shapes=[pltpu.VMEM((n_pairs, LANES, LANES), F32)],
        compiler_params=pltpu.CompilerParams(
            dimension_semantics=("arbitrary", "arbitrary"),
            vmem_limit_bytes=VMEM_LIMIT_BYTES),
        name="wkv_out",
    )(r, lw, k, v, kk, a, c1, og, gn, x, wo, ln)


def _ffn_kernel(xb_ref, x_ref, wg_ref, wu_ref, wd_ref, ln_ref, o_ref, acc_ref, *, alpha):
    f = pl.program_id(1)

    @pl.when(jnp.logical_and(pl.program_id(0) == 0, f == 0))
    def _():
        acc_ref[...] = jnp.zeros_like(acc_ref)

    xb = xb_ref[...]
    gate = jnp.dot(xb, wg_ref[...], preferred_element_type=F32)
    up = jnp.dot(xb, wu_ref[...], preferred_element_type=F32)
    hid = (gate * jax.nn.sigmoid(gate) * up).astype(BF16)
    part = jnp.dot(hid, wd_ref[...], preferred_element_type=F32)
    acc_ref[...] += part

    @pl.when(f == pl.num_programs(1) - 1)
    def _():
        h = alpha * x_ref[...] + acc_ref[...]
        o_ref[...] = _layer_norm(h, ln_ref[0:1, :], ln_ref[1:2, :])
        acc_ref[...] = jnp.zeros_like(acc_ref)


def _ffn(x_bf, x2d, w_gu, w_down, ln, *, alpha, tm, tf):
    m, dm = x2d.shape
    dff = w_down.shape[0]
    nf = dff // tf
    return pl.pallas_call(
        functools.partial(_ffn_kernel, alpha=alpha),
        grid=(m // tm, nf),
        in_specs=[
            pl.BlockSpec((tm, dm), lambda i, f: (i, 0)),
            pl.BlockSpec((tm, dm), lambda i, f: (i, 0)),
            pl.BlockSpec((dm, tf), lambda i, f: (0, f)),
            pl.BlockSpec((dm, tf), lambda i, f: (0, nf + f)),
            pl.BlockSpec((tf, dm), lambda i, f: (f, 0)),
            pl.BlockSpec((SUBLANES, dm), lambda i, f: (0, 0)),
        ],
        out_specs=pl.BlockSpec((tm, dm), lambda i, f: (i, 0)),
        out_shape=jax.ShapeDtypeStruct((m, dm), F32),
        scratch_shapes=[pltpu.VMEM((tm, dm), F32)],
        compiler_params=pltpu.CompilerParams(
            dimension_semantics=("arbitrary", "arbitrary"), vmem_limit_bytes=VMEM_LIMIT_BYTES),
        name="ffn",
    )(x_bf, x2d, w_gu, w_gu, w_down, ln)


def _pad_rows(a, n):
    return jnp.pad(a, ((0, n - a.shape[0]), (0, 0)))


def _pick(n, candidates):
    for c in candidates:
        if n % c == 0:
            return c
    raise ValueError(f"no tile size for extent {n}")


def kernel(x, w_in, shift_mu, conv_w, w0, w_up, a0, a_up, g_up, k_k, k_a, r_k,
           gn_g, gn_b, w_o, ln1_g, ln1_b, w_gu, w_down, ln2_g, ln2_b):
    bsz, t, dm = x.shape
    depth = w_in.shape[0]
    lw_n, la_n, lg_n = w_up.shape[1], a_up.shape[1], g_up.shape[1]
    assert dm % (2 * LANES) == 0 and t % CHUNK == 0
    assert lw_n <= LANES and la_n <= LANES and lg_n <= 2 * LANES
    assert w_in.shape[2] == 8 * dm + lw_n + la_n + lg_n
    alpha = (2.0 * depth) ** 0.25
    tn = 2 * LANES
    tm_in = _pick(t, (512, 256, 128, 64))
    tb = _pick(t, (128, 64))
    m = bsz * t
    tm = _pick(m, (512, 256, 128, 64))
    tf = _pick(w_down.shape[1], (512, 256, 128))

    for l in range(depth):
        wi = w_in[l]
        c0 = 6 * dm
        wi_bf = wi.astype(BF16)
        g0 = c0 + lw_n + la_n + lg_n
        nj = dm // tn
        w_groups = [(wi_bf, 0), (wi_bf, nj), (wi_bf, 2 * nj), (wi_bf[:, g0:g0 + dm], 0),
                    (wi_bf, 3 * nj), (wi_bf, 4 * nj), (wi_bf, 5 * nj), (wi_bf[:, g0 + dm:], 0)]

        def lane_pad(a, n):
            return jnp.pad(a, ((0, 0), (0, n - a.shape[1])))

        wl = jnp.concatenate([
            lane_pad(wi[:, c0:c0 + lw_n], LANES),
            lane_pad(wi[:, c0 + lw_n:c0 + lw_n + la_n], LANES),
            lane_pad(wi[:, c0 + lw_n + la_n:c0 + lw_n + la_n + lg_n], 2 * LANES)],
            axis=1).astype(BF16)
        mu = shift_mu[l]
        s0 = 3 * dm
        mu_l = jnp.concatenate([
            jnp.pad(mu[s0:s0 + lw_n], (0, LANES - lw_n)),
            jnp.pad(mu[s0 + lw_n:s0 + lw_n + la_n], (0, LANES - la_n)),
            jnp.pad(mu[s0 + lw_n + la_n:], (0, 2 * LANES - lg_n))])[None, :]
        wup = jnp.concatenate([_pad_rows(w_up[l], LANES), _pad_rows(a_up[l], LANES),
                               _pad_rows(g_up[l], 2 * LANES)], axis=0).astype(BF16)
        prm = _pad_rows(jnp.stack([
            mu[0:dm], mu[dm:2 * dm], mu[2 * dm:3 * dm],
            conv_w[l, 0], conv_w[l, 1], conv_w[l, 2],
            w0[l], a0[l], k_k[l], k_a[l], r_k[l].reshape(-1)]), 16)

        r, lw, k, v, kk, a, c1, og = _mixer_in(
            x, w_groups, wl, wup, prm, mu_l, tm=tm_in, tn=tn)
        gn = _pad_rows(jnp.stack([gn_g[l], gn_b[l]]), SUBLANES)
        ln1 = _pad_rows(jnp.stack([ln1_g[l], ln1_b[l]]), SUBLANES)
        x1, x1_bf = _wkv(r, lw, k, v, kk, a, c1, og, gn, x, w_o[l].astype(BF16), ln1,
                         alpha=alpha, tb=tb)
        ln2 = _pad_rows(jnp.stack([ln2_g[l], ln2_b[l]]), SUBLANES)
        x = _ffn(x1_bf.reshape(m, dm), x1.reshape(m, dm), w_gu[l].astype(BF16),
                 w_down[l].astype(BF16), ln2, alpha=alpha, tm=tm, tf=tf).reshape(bsz, t, dm)
    return x
```

```python
import functools
import math

import jax
import jax.numpy as jnp
from jax import lax
from jax.experimental import pallas as pl
from jax.experimental.pallas import tpu as pltpu

HEAD_SIZE = 64
LOG2_HEAD = 6
LOG_DECAY_SCALE = -math.exp(-0.5)
LN_EPS = 1e-5
GN_EPS = 64e-5
LANES = 128
SUBLANES = 8
CHUNK = 64
VMEM_LIMIT_BYTES = 56 * 1024 * 1024

F32 = jnp.float32
BF16 = jnp.bfloat16


def _mm(a, b):
    return jnp.dot(a.astype(BF16), b.astype(BF16), preferred_element_type=F32)


def _mm_nt(a, b):
    return lax.dot_general(a.astype(BF16), b.astype(BF16), (((1,), (1,)), ((), ())),
                           preferred_element_type=F32)


def _split2(a):
    hi = a.astype(BF16)
    lo = (a - hi.astype(F32)).astype(BF16)
    return hi, lo


def _mm_exact_lhs(a_exact, b):
    d = functools.partial(jnp.dot, preferred_element_type=F32)
    h, l = _split2(b)
    return d(a_exact, h) + d(a_exact, l)


def _iota_shr(shape, dim, log2_div):
    return lax.shift_right_logical(lax.broadcasted_iota(jnp.int32, shape, dim), log2_div)


def _head_ones(n):
    r = _iota_shr((n, n), 0, LOG2_HEAD)
    c = _iota_shr((n, n), 1, LOG2_HEAD)
    return jnp.where(r == c, 1.0, 0.0).astype(BF16)


N_GROUPS = 8
N_CARRY = 4
LORA_PAD = 512


def _mixer_in_kernel(x_ref, *refs, tm, tn):
    w_refs = refs[:N_GROUPS]
    wl_ref, wup_ref, p_ref, mul_ref = refs[N_GROUPS:N_GROUPS + 4]
    r_ref, lw_ref, k_ref, v_ref, kk_ref, a_ref, c1_ref, og_ref = refs[N_GROUPS + 4:N_GROUPS + 12]
    xb_ref, lora_ref, carry_ref, carry_l_ref = refs[N_GROUPS + 12:]
    i = pl.program_id(1)
    j = pl.program_id(2)
    first = i == 0
    row = lax.broadcasted_iota(jnp.int32, (tm, tn), 0)

    @pl.when(j == 0)
    def _():
        xb_ref[...] = x_ref[0].astype(BF16)

    xt = xb_ref[...]

    def prev1(p, c8):
        return jnp.where(row == 0, c8[SUBLANES - 1:SUBLANES, :], pltpu.roll(p, 1, 0))

    def prev2(p, c8):
        rolled = pltpu.roll(p, 2, 0)
        rolled = jnp.where(row == 1, c8[SUBLANES - 1:SUBLANES, :], rolled)
        return jnp.where(row == 0, c8[SUBLANES - 2:SUBLANES - 1, :], rolled)

    @pl.when(j == 0)
    def _():
        pl_ = jnp.dot(xt, wl_ref[...], preferred_element_type=F32)
        cl = jnp.where(first, 0.0, carry_l_ref[...])
        rowl = lax.broadcasted_iota(jnp.int32, (tm, LORA_PAD), 0)
        prev = jnp.where(rowl == 0, cl[SUBLANES - 1:SUBLANES, :], pltpu.roll(pl_, 1, 0))
        carry_l_ref[...] = pl_[tm - SUBLANES:, :]
        z = pl_ + mul_ref[...] * (prev - pl_)
        lora_ref[:, 0:LANES] = jnp.tanh(z[:, 0:LANES]).astype(BF16)
        lora_ref[:, LANES:2 * LANES] = z[:, LANES:2 * LANES].astype(BF16)
        lora_ref[:, 2 * LANES:] = jax.nn.sigmoid(z[:, 2 * LANES:]).astype(BF16)

    p_cb, p_cc, p_ch, p_gc, p_r, p_k, p_v, p_gr = (
        jnp.dot(xt, w_ref[...], preferred_element_type=F32) for w_ref in w_refs)
    prm = p_ref[...]
    mu_r, mu_k, mu_v = prm[0:1], prm[1:2], prm[2:3]
    cw0, cw1, cw2 = prm[3:4], prm[4:5], prm[5:6]
    w0, a0, k_k, k_a, r_k = prm[6:7], prm[7:8], prm[8:9], prm[9:10], prm[10:11]

    carry = [jnp.where(first, 0.0, carry_ref[j, q]) for q in range(N_CARRY)]

    u = p_cc * p_ch
    conv = cw2 * u + cw1 * prev1(u, carry[0]) + cw0 * prev2(u, carry[0])
    y_conv = jax.nn.sigmoid(p_gc) * (p_cb * conv)

    r = p_r + mu_r * (prev1(p_r, carry[1]) - p_r)
    k = p_k + mu_k * (prev1(p_k, carry[2]) - p_k)
    v = p_v + mu_v * (prev1(p_v, carry[3]) - p_v)
    for q, val in enumerate((u, p_r, p_k, p_v)):
        carry_ref[j, q] = val[tm - SUBLANES:, :]

    wup = wup_ref[...]
    d = functools.partial(jnp.dot, preferred_element_type=F32)
    w_pre = w0 + d(lora_ref[:, 0:LANES], wup[0:LANES])
    a_pre = a0 + d(lora_ref[:, LANES:2 * LANES], wup[LANES:2 * LANES])
    g = d(lora_ref[:, 2 * LANES:], wup[2 * LANES:])
    lw = LOG_DECAY_SCALE * jax.nn.sigmoid(w_pre)
    a = jax.nn.sigmoid(a_pre)

    ones = _head_ones(tn)
    kraw = k * k_k
    ss = _mm(kraw * kraw, ones)
    kk = kraw / jnp.maximum(jnp.sqrt(ss), 1e-12)
    k_mod = k * (1.0 + (a - 1.0) * k_a)
    bonus = _mm(r * k_mod * r_k, ones) * v
    og = g * jax.nn.sigmoid(p_gr)

    r_ref[0] = r
    lw_ref[0] = lw
    k_ref[0] = k_mod
    v_ref[0] = v.astype(v_ref.dtype)
    kk_ref[0] = kk
    a_ref[0] = a
    c1_ref[0] = (y_conv + og * bonus).astype(c1_ref.dtype)
    og_ref[0] = og.astype(og_ref.dtype)


def _mixer_in(x, w_groups, wl, wup, prm, mu_l, *, tm, tn):
    bsz, t, dm = x.shape
    nj = dm // tn
    grid = (bsz, t // tm, nj)
    out_sds = jax.ShapeDtypeStruct((bsz, t, dm), F32)
    out_bf = jax.ShapeDtypeStruct((bsz, t, dm), BF16)
    out_spec = pl.BlockSpec((1, tm, tn), lambda b, i, j: (b, i, j))
    return pl.pallas_call(
        functools.partial(_mixer_in_kernel, tm=tm, tn=tn),
        grid=grid,
        in_specs=[
            pl.BlockSpec((1, tm, dm), lambda b, i, j: (b, i, 0), pipeline_mode=pl.Buffered(1)),
            *[pl.BlockSpec((dm, tn), lambda b, i, j, o=o: (0, o + j)) for _, o in w_groups],
            pl.BlockSpec((dm, LORA_PAD), lambda b, i, j: (0, 0)),
            pl.BlockSpec((LORA_PAD, tn), lambda b, i, j: (0, j)),
            pl.BlockSpec((16, tn), lambda b, i, j: (0, j)),
            pl.BlockSpec((1, LORA_PAD), lambda b, i, j: (0, 0)),
        ],
        out_specs=[out_spec] * 8,
        out_shape=[out_sds, out_sds, out_sds, out_bf, out_sds, out_sds, out_bf, out_bf],
        scratch_shapes=[
            pltpu.VMEM((tm, dm), BF16),
            pltpu.VMEM((tm, LORA_PAD), BF16),
            pltpu.VMEM((nj, N_CARRY, SUBLANES, tn), F32),
            pltpu.VMEM((SUBLANES, LORA_PAD), F32),
        ],
        compiler_params=pltpu.CompilerParams(
            dimension_semantics=("arbitrary", "arbitrary", "arbitrary"),
            vmem_limit_bytes=VMEM_LIMIT_BYTES),
        name="mixer_in",
    )(x, *[w for w, _ in w_groups], wl, wup, prm, mu_l)


def _wkv_block(r, lw, k, v, kk, a, z0, consts, n_chunks):
    cs = consts
    c = CHUNK
    idx = range(len(r))

    def stack(x):
        xb = x.astype(BF16)
        m0, m1 = (cs["m0"], cs["m1"]) if x.shape[1] == LANES else (cs["m0_2"], cs["m1_2"])
        return jnp.concatenate([xb * m0, xb * m1], axis=0)

    def bdiag(xw):
        xb = xw.astype(BF16)
        return jnp.concatenate([xb, xb], axis=0) * cs["bd_bf"]

    def each(fn, *lists):
        return [fn(*args) for args in zip(*lists)]

    cum = each(lambda x: _mm_exact_lhs(cs["tri"], x), lw)
    beta = each(lambda x, y: x * y, kk, a)
    e_inv = each(lambda x: jnp.exp(-x), cum)
    e_rem = each(lambda x: jnp.exp(x[c - 1:c, :] - x), cum)
    rt = each(lambda x, y: x * jnp.exp(y), r, cum)
    at = each(lambda x, y, z_: -x * jnp.exp(y - z_), kk, cum, lw)
    kt = each(lambda x, y: x * y, k, e_inv)
    bt = each(lambda x, y: x * y, beta, e_inv)
    kh = each(lambda x, y: x * y, k, e_rem)
    bh = each(lambda x, y: x * y, beta, e_rem)

    s = each(lambda a_, r_, b_, k_: _mm_nt(jnp.concatenate([a_, r_], axis=0),
                                           jnp.concatenate([stack(b_), stack(k_)], axis=0)),
             at, rt, bt, kt)
    lab = each(lambda x: jnp.where(cs["strict"], x[0:c, 0:2 * c], 0.0), s)
    urb = each(lambda x: jnp.where(cs["incl"], x[c:, 0:2 * c], 0.0), s)

    def twice(x):
        xb = x.astype(BF16)
        return jnp.concatenate([xb, xb], axis=0)

    lab2 = each(twice, lab)
    l8 = each(lambda x: jnp.where(cs["blk8"], x, 0.0), lab)
    l8_2 = each(lambda x, x2: _mm(x, x2 * cs["blk8_bd"]), l8, lab2)
    l8_4 = each(lambda x: _mm(x, bdiag(x)), l8_2)
    tw = each(lambda x, y: _mm(cs["eye"] + x, bdiag(cs["eye"] + y)), l8, l8_2)
    tw = each(lambda x, y: _mm(x, bdiag(cs["eye"] + y)), tw, l8_4)
    for join_bd in cs["joins_bd"]:
        half = each(lambda t_, x2: _mm(t_, x2 * join_bd), tw, lab2)
        tw = each(lambda t_, h_: t_ + _mm(h_, bdiag(t_)), tw, half)

    nv_uv = each(lambda x, v_: _mm(jnp.where(cs["strict_incl"], x[:, 2 * c:], 0.0), stack(v_)),
                 s, v)
    ah_d = each(lambda t_, a_, n_: _mm(t_, stack(jnp.concatenate([a_, n_[0:c]], axis=1))),
                tw, at, nv_uv)
    u2 = each(lambda u_, x: _mm(u_, stack(x)), urb, ah_d)
    rh = each(lambda r_, u_: r_ + u_[:, 0:LANES], rt, u2)
    e = each(lambda u_, w_: u_[:, LANES:] + w_[c:], u2, nv_uv)
    zeros = jnp.zeros((c, LANES), BF16)
    gf = each(lambda b_, k_, x, v_: _mm(
        jnp.concatenate([b_, k_], axis=0).T,
        jnp.concatenate([x.astype(BF16), jnp.concatenate([zeros, v_.astype(BF16)], axis=1)],
                        axis=0)), bh, kh, ah_d, v)
    gl = each(lambda x: jnp.where(cs["bd"], x[:, 0:LANES], 0.0), gf)
    f = each(lambda x: jnp.where(cs["bd"], x[:, LANES:], 0.0), gf)
    decay_col = each(lambda x: jnp.exp(x.T[:, c - 1:c]), cum)

    ys = [None] * len(r)
    states = list(z0)
    for ci in range(n_chunks):
        for p in range(len(states)):
            n = p * n_chunks + ci
            z = states[p]
            rz_gz = _mm(jnp.concatenate([rh[n], gl[n]], axis=0), z)
            ys[n] = rz_gz[0:c] + e[n]
            states[p] = decay_col[n] * z + rz_gz[c:] + f[n]
    return ys, states


def _wkv_consts():
    c = CHUNK
    assert c == HEAD_SIZE and 2 * c == LANES
    t = lax.broadcasted_iota(jnp.int32, (c, 2 * c), 0)
    i = jnp.bitwise_and(lax.broadcasted_iota(jnp.int32, (c, 2 * c), 1), c - 1)
    rr = lax.broadcasted_iota(jnp.int32, (c, c), 0)
    cc = lax.broadcasted_iota(jnp.int32, (c, c), 1)
    def head_lane_masks(width):
        lane = jnp.bitwise_and(lax.broadcasted_iota(jnp.int32, (c, width), 1), LANES - 1)
        return (jnp.where(lane < HEAD_SIZE, 1.0, 0.0).astype(BF16),
                jnp.where(lane < HEAD_SIZE, 0.0, 1.0).astype(BF16))

    m0, m1 = head_lane_masks(LANES)
    m0_2, m1_2 = head_lane_masks(2 * LANES)
    bd = _iota_shr((2 * c, 2 * c), 0, LOG2_HEAD) == _iota_shr((2 * c, 2 * c), 1, LOG2_HEAD)
    t2 = lax.broadcasted_iota(jnp.int32, (2 * c, 2 * c), 0)
    i2 = jnp.bitwise_and(lax.broadcasted_iota(jnp.int32, (2 * c, 2 * c), 1), c - 1)
    strict_incl = jnp.logical_or(jnp.logical_and(t2 < c, i2 < t2),
                                 jnp.logical_and(t2 >= c, i2 <= t2 - c))
    tt = jnp.bitwise_and(t2, c - 1)
    blk2 = [lax.shift_right_logical(tt, s) == lax.shift_right_logical(i2, s) for s in (3, 4, 5)]
    out2 = [jnp.logical_not(b) for b in blk2]
    joins = [jnp.logical_and(blk2[1], out2[0]), jnp.logical_and(blk2[2], out2[1]), out2[2]]

    def bd_mask(m):
        return jnp.where(jnp.logical_and(bd, m), 1.0, 0.0).astype(BF16)

    return dict(
        strict=i < t, incl=i <= t, strict_incl=strict_incl,
        eye=jnp.where(i == t, 1.0, 0.0).astype(F32),
        blk8=lax.shift_right_logical(t, 3) == lax.shift_right_logical(i, 3),
        blk8_bd=bd_mask(blk2[0]), joins_bd=[bd_mask(m) for m in joins],
        tri=jnp.where(cc <= rr, 1.0, 0.0).astype(BF16),
        m0=m0, m1=m1, m0_2=m0_2, m1_2=m1_2,
        bd=bd, bd_bf=jnp.where(bd, 1.0, 0.0).astype(BF16))


def _wkv_kernel(r_ref, lw_ref, k_ref, v_ref, kk_ref, a_ref, c1_ref, og_ref, gn_ref,
                x_ref, wo_ref, ln_ref, o_ref, obf_ref, z_ref, *, n_chunks, n_pairs, alpha):
    @pl.when(pl.program_id(1) == 0)
    def _():
        z_ref[...] = jnp.zeros_like(z_ref)

    def chunks(ref):
        return [ref[0, ci * CHUNK:(ci + 1) * CHUNK, p * LANES:(p + 1) * LANES]
                for p in range(n_pairs) for ci in range(n_chunks)]

    ys, states = _wkv_block(chunks(r_ref), chunks(lw_ref), chunks(k_ref), chunks(v_ref),
                            chunks(kk_ref), chunks(a_ref), [z_ref[p] for p in range(n_pairs)],
                            _wkv_consts(), n_chunks)
    ones = _head_ones(LANES)
    inv_n = 1.0 / HEAD_SIZE
    y = jnp.concatenate(ys, axis=0)
    mu = _mm(y, ones) * inv_n
    yc = y - mu
    var = _mm(yc * yc, ones) * inv_n
    yn = yc * lax.rsqrt(var + GN_EPS)
    tb = n_chunks * CHUNK
    merged = []
    for p in range(n_pairs):
        z_ref[p] = states[p]
        ls = slice(p * LANES, (p + 1) * LANES)
        yp = yn[p * tb:(p + 1) * tb] * gn_ref[0:1, ls] + gn_ref[1:2, ls]
        merged.append((c1_ref[0, :, ls].astype(F32)
                       + og_ref[0, :, ls].astype(F32) * yp).astype(BF16))
    h = alpha * x_ref[0] + jnp.dot(jnp.concatenate(merged, axis=1), wo_ref[...],
                                   preferred_element_type=F32)
    x1 = _layer_norm(h, ln_ref[0:1, :], ln_ref[1:2, :])
    o_ref[0] = x1
    obf_ref[0] = x1.astype(BF16)


def _layer_norm(h, g, b):
    mu = jnp.mean(h, axis=-1, keepdims=True)
    hc = h - mu
    var = jnp.mean(hc * hc, axis=-1, keepdims=True)
    return hc * lax.rsqrt(var + LN_EPS) * g + b


def _wkv(r, lw, k, v, kk, a, c1, og, gn, x, wo, ln, *, alpha, tb):
    bsz, t, dm = r.shape
    n_pairs = dm // LANES
    spec = pl.BlockSpec((1, tb, dm), lambda b, i: (b, i, 0))
    row_spec = pl.BlockSpec((SUBLANES, dm), lambda b, i: (0, 0))
    return pl.pallas_call(
        functools.partial(_wkv_kernel, n_chunks=tb // CHUNK, n_pairs=n_pairs, alpha=alpha),
        grid=(bsz, t // tb),
        in_specs=[spec] * 8 + [row_spec, spec, pl.BlockSpec((dm, dm), lambda b, i: (0, 0)),
                               row_spec],
        out_specs=[spec, spec],
        out_shape=[jax.ShapeDtypeStruct((bsz, t, dm), F32),
                   jax.ShapeDtypeStruct((bsz, t, dm), BF16)],
        scratch_shapes=[pltpu.VMEM((n_pairs, LANES, LANES), F32)],
        compiler_params=pltpu.CompilerParams(
            dimension_semantics=("arbitrary", "arbitrary"),
            vmem_limit_bytes=VMEM_LIMIT_BYTES),
        name="wkv_out",
    )(r, lw, k, v, kk, a, c1, og, gn, x, wo, ln)


def _ffn_kernel(xb_ref, x_ref, wg_ref, wu_ref, wd_ref, ln_ref, o_ref, acc_ref, *, alpha):
    f = pl.program_id(1)

    @pl.when(jnp.logical_and(pl.program_id(0) == 0, f == 0))
    def _():
        acc_ref[...] = jnp.zeros_like(acc_ref)

    xb = xb_ref[...]
    gate = jnp.dot(xb, wg_ref[...], preferred_element_type=F32)
    up = jnp.dot(xb, wu_ref[...], preferred_element_type=F32)
    hid = (gate * jax.nn.sigmoid(gate) * up).astype(BF16)
    part = jnp.dot(hid, wd_ref[...], preferred_element_type=F32)
    acc_ref[...] += part

    @pl.when(f == pl.num_programs(1) - 1)
    def _():
        h = alpha * x_ref[...] + acc_ref[...]
        o_ref[...] = _layer_norm(h, ln_ref[0:1, :], ln_ref[1:2, :])
        acc_ref[...] = jnp.zeros_like(acc_ref)


def _ffn(x_bf, x2d, w_gu, w_down, ln, *, alpha, tm, tf):
    m, dm = x2d.shape
    dff = w_down.shape[0]
    nf = dff // tf
    return pl.pallas_call(
        functools.partial(_ffn_kernel, alpha=alpha),
        grid=(m // tm, nf),
        in_specs=[
            pl.BlockSpec((tm, dm), lambda i, f: (i, 0)),
            pl.BlockSpec((tm, dm), lambda i, f: (i, 0)),
            pl.BlockSpec((dm, tf), lambda i, f: (0, f)),
            pl.BlockSpec((dm, tf), lambda i, f: (0, nf + f)),
            pl.BlockSpec((tf, dm), lambda i, f: (f, 0)),
            pl.BlockSpec((SUBLANES, dm), lambda i, f: (0, 0)),
        ],
        out_specs=pl.BlockSpec((tm, dm), lambda i, f: (i, 0)),
        out_shape=jax.ShapeDtypeStruct((m, dm), F32),
        scratch_shapes=[pltpu.VMEM((tm, dm), F32)],
        compiler_params=pltpu.CompilerParams(
            dimension_semantics=("arbitrary", "arbitrary"), vmem_limit_bytes=VMEM_LIMIT_BYTES),
        name="ffn",
    )(x_bf, x2d, w_gu, w_gu, w_down, ln)


def _pad_rows(a, n):
    return jnp.pad(a, ((0, n - a.shape[0]), (0, 0)))


def _pick(n, candidates):
    for c in candidates:
        if n % c == 0:
            return c
    raise ValueError(f"no tile size for extent {n}")


def kernel(x, w_in, shift_mu, conv_w, w0, w_up, a0, a_up, g_up, k_k, k_a, r_k,
           gn_g, gn_b, w_o, ln1_g, ln1_b, w_gu, w_down, ln2_g, ln2_b):
    bsz, t, dm = x.shape
    depth = w_in.shape[0]
    lw_n, la_n, lg_n = w_up.shape[1], a_up.shape[1], g_up.shape[1]
    assert dm % (2 * LANES) == 0 and t % CHUNK == 0
    assert lw_n <= LANES and la_n <= LANES and lg_n <= 2 * LANES
    assert w_in.shape[2] == 8 * dm + lw_n + la_n + lg_n
    alpha = (2.0 * depth) ** 0.25
    tn = 2 * LANES
    tm_in = _pick(t, (1024, 512, 256, 128, 64))
    tb = _pick(t, (128, 64))
    m = bsz * t
    tm = _pick(m, (512, 256, 128, 64))
    tf = _pick(w_down.shape[1], (512, 256, 128))

    for l in range(depth):
        wi = w_in[l]
        c0 = 6 * dm
        wi_bf = wi.astype(BF16)
        g0 = c0 + lw_n + la_n + lg_n
        nj = dm // tn
        w_groups = [(wi_bf, 0), (wi_bf, nj), (wi_bf, 2 * nj), (wi_bf[:, g0:g0 + dm], 0),
                    (wi_bf, 3 * nj), (wi_bf, 4 * nj), (wi_bf, 5 * nj), (wi_bf[:, g0 + dm:], 0)]

        def lane_pad(a, n):
            return jnp.pad(a, ((0, 0), (0, n - a.shape[1])))

        wl = jnp.concatenate([
            lane_pad(wi[:, c0:c0 + lw_n], LANES),
            lane_pad(wi[:, c0 + lw_n:c0 + lw_n + la_n], LANES),
            lane_pad(wi[:, c0 + lw_n + la_n:c0 + lw_n + la_n + lg_n], 2 * LANES)],
            axis=1).astype(BF16)
        mu = shift_mu[l]
        s0 = 3 * dm
        mu_l = jnp.concatenate([
            jnp.pad(mu[s0:s0 + lw_n], (0, LANES - lw_n)),
            jnp.pad(mu[s0 + lw_n:s0 + lw_n + la_n], (0, LANES - la_n)),
            jnp.pad(mu[s0 + lw_n + la_n:], (0, 2 * LANES - lg_n))])[None, :]
        wup = jnp.concatenate([_pad_rows(w_up[l], LANES), _pad_rows(a_up[l], LANES),
                               _pad_rows(g_up[l], 2 * LANES)], axis=0).astype(BF16)
        prm = _pad_rows(jnp.stack([
            mu[0:dm], mu[dm:2 * dm], mu[2 * dm:3 * dm],
            conv_w[l, 0], conv_w[l, 1], conv_w[l, 2],
            w0[l], a0[l], k_k[l], k_a[l], r_k[l].reshape(-1)]), 16)

        r, lw, k, v, kk, a, c1, og = _mixer_in(
            x, w_groups, wl, wup, prm, mu_l, tm=tm_in, tn=tn)
        gn = _pad_rows(jnp.stack([gn_g[l], gn_b[l]]), SUBLANES)
        ln1 = _pad_rows(jnp.stack([ln1_g[l], ln1_b[l]]), SUBLANES)
        x1, x1_bf = _wkv(r, lw, k, v, kk, a, c1, og, gn, x, w_o[l].astype(BF16), ln1,
                         alpha=alpha, tb=tb)
        ln2 = _pad_rows(jnp.stack([ln2_g[l], ln2_b[l]]), SUBLANES)
        x = _ffn(x1_bf.reshape(m, dm), x1.reshape(m, dm), w_gu[l].astype(BF16),
                 w_down[l].astype(BF16), ln2, alpha=alpha, tm=tm, tf=tf).reshape(bsz, t, dm)
    return x
```

```python
import functools
import math

import jax
import jax.numpy as jnp
from jax import lax
from jax.experimental import pallas as pl
from jax.experimental.pallas import tpu as pltpu

HEAD_SIZE = 64
LOG2_HEAD = 6
LOG_DECAY_SCALE = -math.exp(-0.5)
LN_EPS = 1e-5
GN_EPS = 64e-5
LANES = 128
SUBLANES = 8
CHUNK = 64
VMEM_LIMIT_BYTES = 56 * 1024 * 1024

F32 = jnp.float32
BF16 = jnp.bfloat16


def _mm(a, b):
    return jnp.dot(a.astype(BF16), b.astype(BF16), preferred_element_type=F32)


def _mm_nt(a, b):
    return lax.dot_general(a.astype(BF16), b.astype(BF16), (((1,), (1,)), ((), ())),
                           preferred_element_type=F32)


def _split2(a):
    hi = a.astype(BF16)
    lo = (a - hi.astype(F32)).astype(BF16)
    return hi, lo


def _mm_exact_lhs(a_exact, b):
    d = functools.partial(jnp.dot, preferred_element_type=F32)
    h, l = _split2(b)
    return d(a_exact, h) + d(a_exact, l)


def _iota_shr(shape, dim, log2_div):
    return lax.shift_right_logical(lax.broadcasted_iota(jnp.int32, shape, dim), log2_div)


def _head_ones(n):
    r = _iota_shr((n, n), 0, LOG2_HEAD)
    c = _iota_shr((n, n), 1, LOG2_HEAD)
    return jnp.where(r == c, 1.0, 0.0).astype(BF16)


N_GROUPS = 8
N_CARRY = 4
LORA_PAD = 512


def _mixer_in_kernel(x_ref, *refs, tm, tn):
    w_refs = refs[:N_GROUPS]
    wl_ref, wup_ref, p_ref, mul_ref = refs[N_GROUPS:N_GROUPS + 4]
    r_ref, lw_ref, k_ref, v_ref, kk_ref, a_ref, c1_ref, og_ref = refs[N_GROUPS + 4:N_GROUPS + 12]
    xb_ref, lora_ref, carry_ref, carry_l_ref = refs[N_GROUPS + 12:]
    i = pl.program_id(1)
    j = pl.program_id(2)
    first = i == 0
    row = lax.broadcasted_iota(jnp.int32, (tm, tn), 0)

    @pl.when(j == 0)
    def _():
        xb_ref[...] = x_ref[0].astype(BF16)

    xt = xb_ref[...]

    def prev1(p, c8):
        return jnp.where(row == 0, c8[SUBLANES - 1:SUBLANES, :], pltpu.roll(p, 1, 0))

    def prev2(p, c8):
        rolled = pltpu.roll(p, 2, 0)
        rolled = jnp.where(row == 1, c8[SUBLANES - 1:SUBLANES, :], rolled)
        return jnp.where(row == 0, c8[SUBLANES - 2:SUBLANES - 1, :], rolled)

    @pl.when(j == 0)
    def _():
        pl_ = jnp.dot(xt, wl_ref[...], preferred_element_type=F32)
        cl = jnp.where(first, 0.0, carry_l_ref[...])
        rowl = lax.broadcasted_iota(jnp.int32, (tm, LORA_PAD), 0)
        prev = jnp.where(rowl == 0, cl[SUBLANES - 1:SUBLANES, :], pltpu.roll(pl_, 1, 0))
        carry_l_ref[...] = pl_[tm - SUBLANES:, :]
        z = pl_ + mul_ref[...] * (prev - pl_)
        lora_ref[:, 0:LANES] = jnp.tanh(z[:, 0:LANES]).astype(BF16)
        lora_ref[:, LANES:2 * LANES] = z[:, LANES:2 * LANES].astype(BF16)
        lora_ref[:, 2 * LANES:] = jax.nn.sigmoid(z[:, 2 * LANES:]).astype(BF16)

    prm = p_ref[...]
    mu_r, mu_k, mu_v = prm[0:1], prm[1:2], prm[2:3]
    cw0, cw1, cw2 = prm[3:4], prm[4:5], prm[5:6]
    w0, a0, k_k, k_a, r_k = prm[6:7], prm[7:8], prm[8:9], prm[9:10], prm[10:11]
    carry = [jnp.where(first, 0.0, carry_ref[j, q]) for q in range(N_CARRY)]
    w_cb, w_cc, w_ch, w_gc, w_r, w_k, w_v, w_gr = w_refs
    d = functools.partial(jnp.dot, preferred_element_type=F32)

    wup = wup_ref[...]
    w_pre = w0 + d(lora_ref[:, 0:LANES], wup[0:LANES])
    a_pre = a0 + d(lora_ref[:, LANES:2 * LANES], wup[LANES:2 * LANES])
    g = d(lora_ref[:, 2 * LANES:], wup[2 * LANES:])
    lw = LOG_DECAY_SCALE * jax.nn.sigmoid(w_pre)
    a = jax.nn.sigmoid(a_pre)

    p_r, p_k, p_v = d(xt, w_r[...]), d(xt, w_k[...]), d(xt, w_v[...])
    r = p_r + mu_r * (prev1(p_r, carry[1]) - p_r)
    k = p_k + mu_k * (prev1(p_k, carry[2]) - p_k)
    v = p_v + mu_v * (prev1(p_v, carry[3]) - p_v)
    p_gr, p_cb = d(xt, w_gr[...]), d(xt, w_cb[...])

    ones = _head_ones(tn)
    kraw = k * k_k
    ss = _mm(kraw * kraw, ones)
    kk = kraw / jnp.maximum(jnp.sqrt(ss), 1e-12)
    k_mod = k * (1.0 + (a - 1.0) * k_a)
    bonus = _mm(r * k_mod * r_k, ones) * v
    og = g * jax.nn.sigmoid(p_gr)

    p_cc, p_ch, p_gc = d(xt, w_cc[...]), d(xt, w_ch[...]), d(xt, w_gc[...])
    u = p_cc * p_ch
    conv = cw2 * u + cw1 * prev1(u, carry[0]) + cw0 * prev2(u, carry[0])
    y_conv = jax.nn.sigmoid(p_gc) * (p_cb * conv)
    for q, val in enumerate((u, p_r, p_k, p_v)):
        carry_ref[j, q] = val[tm - SUBLANES:, :]

    r_ref[0] = r
    lw_ref[0] = lw
    k_ref[0] = k_mod
    v_ref[0] = v.astype(v_ref.dtype)
    kk_ref[0] = kk
    a_ref[0] = a
    c1_ref[0] = (y_conv + og * bonus).astype(c1_ref.dtype)
    og_ref[0] = og.astype(og_ref.dtype)


def _mixer_in(x, w_groups, wl, wup, prm, mu_l, *, tm, tn):
    bsz, t, dm = x.shape
    nj = dm // tn
    grid = (bsz, t // tm, nj)
    out_sds = jax.ShapeDtypeStruct((bsz, t, dm), F32)
    out_bf = jax.ShapeDtypeStruct((bsz, t, dm), BF16)
    out_spec = pl.BlockSpec((1, tm, tn), lambda b, i, j: (b, i, j))
    return pl.pallas_call(
        functools.partial(_mixer_in_kernel, tm=tm, tn=tn),
        grid=grid,
        in_specs=[
            pl.BlockSpec((1, tm, dm), lambda b, i, j: (b, i, 0)),
            *[pl.BlockSpec((dm, tn), lambda b, i, j, o=o: (0, o + j)) for _, o in w_groups],
            pl.BlockSpec((dm, LORA_PAD), lambda b, i, j: (0, 0)),
            pl.BlockSpec((LORA_PAD, tn), lambda b, i, j: (0, j)),
            pl.BlockSpec((16, tn), lambda b, i, j: (0, j)),
            pl.BlockSpec((1, LORA_PAD), lambda b, i, j: (0, 0)),
        ],
        out_specs=[out_spec] * 8,
        out_shape=[out_sds, out_sds, out_sds, out_bf, out_sds, out_sds, out_bf, out_bf],
        scratch_shapes=[
            pltpu.VMEM((tm, dm), BF16),
            pltpu.VMEM((tm, LORA_PAD), BF16),
            pltpu.VMEM((nj, N_CARRY, SUBLANES, tn), F32),
            pltpu.VMEM((SUBLANES, LORA_PAD), F32),
        ],
        compiler_params=pltpu.CompilerParams(
            dimension_semantics=("arbitrary", "arbitrary", "arbitrary"),
            vmem_limit_bytes=VMEM_LIMIT_BYTES),
        name="mixer_in",
    )(x, *[w for w, _ in w_groups], wl, wup, prm, mu_l)


def _wkv_block(r, lw, k, v, kk, a, z0, consts, n_chunks):
    cs = consts
    c = CHUNK
    idx = range(len(r))

    def stack(x):
        xb = x.astype(BF16)
        m0, m1 = (cs["m0"], cs["m1"]) if x.shape[1] == LANES else (cs["m0_2"], cs["m1_2"])
        return jnp.concatenate([xb * m0, xb * m1], axis=0)

    def bdiag(xw):
        xb = xw.astype(BF16)
        return jnp.concatenate([xb, xb], axis=0) * cs["bd_bf"]

    def each(fn, *lists):
        return [fn(*args) for args in zip(*lists)]

    cum = each(lambda x: _mm_exact_lhs(cs["tri"], x), lw)
    beta = each(lambda x, y: x * y, kk, a)
    e_inv = each(lambda x: jnp.exp(-x), cum)
    e_rem = each(lambda x: jnp.exp(x[c - 1:c, :] - x), cum)
    rt = each(lambda x, y: x * jnp.exp(y), r, cum)
    at = each(lambda x, y, z_: -x * jnp.exp(y - z_), kk, cum, lw)
    kt = each(lambda x, y: x * y, k, e_inv)
    bt = each(lambda x, y: x * y, beta, e_inv)
    kh = each(lambda x, y: x * y, k, e_rem)
    bh = each(lambda x, y: x * y, beta, e_rem)

    s = each(lambda a_, r_, b_, k_: _mm_nt(jnp.concatenate([a_, r_], axis=0),
                                           jnp.concatenate([stack(b_), stack(k_)], axis=0)),
             at, rt, bt, kt)
    lab = each(lambda x: jnp.where(cs["strict"], x[0:c, 0:2 * c], 0.0), s)
    urb = each(lambda x: jnp.where(cs["incl"], x[c:, 0:2 * c], 0.0), s)

    def twice(x):
        xb = x.astype(BF16)
        return jnp.concatenate([xb, xb], axis=0)

    lab2 = each(twice, lab)
    l8 = each(lambda x: jnp.where(cs["blk8"], x, 0.0), lab)
    l8_2 = each(lambda x, x2: _mm(x, x2 * cs["blk8_bd"]), l8, lab2)
    l8_4 = each(lambda x: _mm(x, bdiag(x)), l8_2)
    tw = each(lambda x, y: _mm(cs["eye"] + x, bdiag(cs["eye"] + y)), l8, l8_2)
    tw = each(lambda x, y: _mm(x, bdiag(cs["eye"] + y)), tw, l8_4)
    for join_bd in cs["joins_bd"]:
        half = each(lambda t_, x2: _mm(t_, x2 * join_bd), tw, lab2)
        tw = each(lambda t_, h_: t_ + _mm(h_, bdiag(t_)), tw, half)

    nv_uv = each(lambda x, v_: _mm(jnp.where(cs["strict_incl"], x[:, 2 * c:], 0.0), stack(v_)),
                 s, v)
    ah_d = each(lambda t_, a_, n_: _mm(t_, stack(jnp.concatenate([a_, n_[0:c]], axis=1))),
                tw, at, nv_uv)
    u2 = each(lambda u_, x: _mm(u_, stack(x)), urb, ah_d)
    rh = each(lambda r_, u_: r_ + u_[:, 0:LANES], rt, u2)
    e = each(lambda u_, w_: u_[:, LANES:] + w_[c:], u2, nv_uv)
    zeros = jnp.zeros((c, LANES), BF16)
    gf = each(lambda b_, k_, x, v_: _mm(
        jnp.concatenate([b_, k_], axis=0).T,
        jnp.concatenate([x.astype(BF16), jnp.concatenate([zeros, v_.astype(BF16)], axis=1)],
                        axis=0)), bh, kh, ah_d, v)
    gl = each(lambda x: jnp.where(cs["bd"], x[:, 0:LANES], 0.0), gf)
    f = each(lambda x: jnp.where(cs["bd"], x[:, LANES:], 0.0), gf)
    decay_col = each(lambda x: jnp.exp(x.T[:, c - 1:c]), cum)

    ys = [None] * len(r)
    states = list(z0)
    for ci in range(n_chunks):
        for p in range(len(states)):
            n = p * n_chunks + ci
            z = states[p]
            rz_gz = _mm(jnp.concatenate([rh[n], gl[n]], axis=0), z)
            ys[n] = rz_gz[0:c] + e[n]
            states[p] = decay_col[n] * z + rz_gz[c:] + f[n]
    return ys, states


def _wkv_consts():
    c = CHUNK
    assert c == HEAD_SIZE and 2 * c == LANES
    t = lax.broadcasted_iota(jnp.int32, (c, 2 * c), 0)
    i = jnp.bitwise_and(lax.broadcasted_iota(jnp.int32, (c, 2 * c), 1), c - 1)
    rr = lax.broadcasted_iota(jnp.int32, (c, c), 0)
    cc = lax.broadcasted_iota(jnp.int32, (c, c), 1)
    def head_lane_masks(width):
        lane = jnp.bitwise_and(lax.broadcasted_iota(jnp.int32, (c, width), 1), LANES - 1)
        return (jnp.where(lane < HEAD_SIZE, 1.0, 0.0).astype(BF16),
                jnp.where(lane < HEAD_SIZE, 0.0, 1.0).astype(BF16))

    m0, m1 = head_lane_masks(LANES)
    m0_2, m1_2 = head_lane_masks(2 * LANES)
    bd = _iota_shr((2 * c, 2 * c), 0, LOG2_HEAD) == _iota_shr((2 * c, 2 * c), 1, LOG2_HEAD)
    t2 = lax.broadcasted_iota(jnp.int32, (2 * c, 2 * c), 0)
    i2 = jnp.bitwise_and(lax.broadcasted_iota(jnp.int32, (2 * c, 2 * c), 1), c - 1)
    strict_incl = jnp.logical_or(jnp.logical_and(t2 < c, i2 < t2),
                                 jnp.logical_and(t2 >= c, i2 <= t2 - c))
    tt = jnp.bitwise_and(t2, c - 1)
    blk2 = [lax.shift_right_logical(tt, s) == lax.shift_right_logical(i2, s) for s in (3, 4, 5)]
    out2 = [jnp.logical_not(b) for b in blk2]
    joins = [jnp.logical_and(blk2[1], out2[0]), jnp.logical_and(blk2[2], out2[1]), out2[2]]

    def bd_mask(m):
        return jnp.where(jnp.logical_and(bd, m), 1.0, 0.0).astype(BF16)

    return dict(
        strict=i < t, incl=i <= t, strict_incl=strict_incl,
        eye=jnp.where(i == t, 1.0, 0.0).astype(F32),
        blk8=lax.shift_right_logical(t, 3) == lax.shift_right_logical(i, 3),
        blk8_bd=bd_mask(blk2[0]), joins_bd=[bd_mask(m) for m in joins],
        tri=jnp.where(cc <= rr, 1.0, 0.0).astype(BF16),
        m0=m0, m1=m1, m0_2=m0_2, m1_2=m1_2,
        bd=bd, bd_bf=jnp.where(bd, 1.0, 0.0).astype(BF16))


def _wkv_kernel(r_ref, lw_ref, k_ref, v_ref, kk_ref, a_ref, c1_ref, og_ref, gn_ref,
                x_ref, wo_ref, ln_ref, o_ref, obf_ref, z_ref, *, n_chunks, n_pairs, alpha):
    @pl.when(pl.program_id(1) == 0)
    def _():
        z_ref[...] = jnp.zeros_like(z_ref)

    def chunks(ref):
        return [ref[0, ci * CHUNK:(ci + 1) * CHUNK, p * LANES:(p + 1) * LANES]
                for p in range(n_pairs) for ci in range(n_chunks)]

    ys, states = _wkv_block(chunks(r_ref), chunks(lw_ref), chunks(k_ref), chunks(v_ref),
                            chunks(kk_ref), chunks(a_ref), [z_ref[p] for p in range(n_pairs)],
                            _wkv_consts(), n_chunks)
    ones = _head_ones(LANES)
    inv_n = 1.0 / HEAD_SIZE
    y = jnp.concatenate(ys, axis=0)
    mu = _mm(y, ones) * inv_n
    yc = y - mu
    var = _mm(yc * yc, ones) * inv_n
    yn = yc * lax.rsqrt(var + GN_EPS)
    tb = n_chunks * CHUNK
    merged = []
    for p in range(n_pairs):
        z_ref[p] = states[p]
        ls = slice(p * LANES, (p + 1) * LANES)
        yp = yn[p * tb:(p + 1) * tb] * gn_ref[0:1, ls] + gn_ref[1:2, ls]
        merged.append((c1_ref[0, :, ls].astype(F32)
                       + og_ref[0, :, ls].astype(F32) * yp).astype(BF16))
    h = alpha * x_ref[0] + jnp.dot(jnp.concatenate(merged, axis=1), wo_ref[...],
                                   preferred_element_type=F32)
    x1 = _layer_norm(h, ln_ref[0:1, :], ln_ref[1:2, :])
    o_ref[0] = x1
    obf_ref[0] = x1.astype(BF16)


def _layer_norm(h, g, b):
    mu = jnp.mean(h, axis=-1, keepdims=True)
    hc = h - mu
    var = jnp.mean(hc * hc, axis=-1, keepdims=True)
    return hc * lax.rsqrt(var + LN_EPS) * g + b


def _wkv(r, lw, k, v, kk, a, c1, og, gn, x, wo, ln, *, alpha, tb):
    bsz, t, dm = r.shape
    n_pairs = dm // LANES
    spec = pl.BlockSpec((1, tb, dm), lambda b, i: (b, i, 0))
    row_spec = pl.BlockSpec((SUBLANES, dm), lambda b, i: (0, 0))
    return pl.pallas_call(
        functools.partial(_wkv_kernel, n_chunks=tb // CHUNK, n_pairs=n_pairs, alpha=alpha),
        grid=(bsz, t // tb),
        in_specs=[spec] * 8 + [row_spec, spec, pl.BlockSpec((dm, dm), lambda b, i: (0, 0)),
                               row_spec],
        out_specs=[spec, spec],
        out_shape=[jax.ShapeDtypeStruct((bsz, t, dm), F32),
                   jax.ShapeDtypeStruct((bsz, t, dm), BF16)],
        scratch_shapes=[pltpu.VMEM((n_pairs, LANES, LANES), F32)],
        compiler_params=pltpu.CompilerParams(
            dimension_semantics=("arbitrary", "arbitrary"),
            vmem_limit_bytes=VMEM_LIMIT_BYTES),
        name="wkv_out",
    )(r, lw, k, v, kk, a, c1, og, gn, x, wo, ln)


def _ffn_kernel(xb_ref, x_ref, wg_ref, wu_ref, wd_ref, ln_ref, o_ref, acc_ref, *, alpha):
    f = pl.program_id(1)

    @pl.when(jnp.logical_and(pl.program_id(0) == 0, f == 0))
    def _():
        acc_ref[...] = jnp.zeros_like(acc_ref)

    xb = xb_ref[...]
    gate = jnp.dot(xb, wg_ref[...], preferred_element_type=F32)
    up = jnp.dot(xb, wu_ref[...], preferred_element_type=F32)
    hid = (gate * jax.nn.sigmoid(gate) * up).astype(BF16)
    part = jnp.dot(hid, wd_ref[...], preferred_element_type=F32)
    acc_ref[...] += part

    @pl.when(f == pl.num_programs(1) - 1)
    def _():
        h = alpha * x_ref[...] + acc_ref[...]
        o_ref[...] = _layer_norm(h, ln_ref[0:1, :], ln_ref[1:2, :])
        acc_ref[...] = jnp.zeros_like(acc_ref)


def _ffn(x_bf, x2d, w_gu, w_down, ln, *, alpha, tm, tf):
    m, dm = x2d.shape
    dff = w_down.shape[0]
    nf = dff // tf
    return pl.pallas_call(
        functools.partial(_ffn_kernel, alpha=alpha),
        grid=(m // tm, nf),
        in_specs=[
            pl.BlockSpec((tm, dm), lambda i, f: (i, 0)),
            pl.BlockSpec((tm, dm), lambda i, f: (i, 0)),
            pl.BlockSpec((dm, tf), lambda i, f: (0, f)),
            pl.BlockSpec((dm, tf), lambda i, f: (0, nf + f)),
            pl.BlockSpec((tf, dm), lambda i, f: (f, 0)),
            pl.BlockSpec((SUBLANES, dm), lambda i, f: (0, 0)),
        ],
        out_specs=pl.BlockSpec((tm, dm), lambda i, f: (i, 0)),
        out_shape=jax.ShapeDtypeStruct((m, dm), F32),
        scratch_shapes=[pltpu.VMEM((tm, dm), F32)],
        compiler_params=pltpu.CompilerParams(
            dimension_semantics=("arbitrary", "arbitrary"), vmem_limit_bytes=VMEM_LIMIT_BYTES),
        name="ffn",
    )(x_bf, x2d, w_gu, w_gu, w_down, ln)


def _pad_rows(a, n):
    return jnp.pad(a, ((0, n - a.shape[0]), (0, 0)))


def _pick(n, candidates):
    for c in candidates:
        if n % c == 0:
            return c
    raise ValueError(f"no tile size for extent {n}")


def kernel(x, w_in, shift_mu, conv_w, w0, w_up, a0, a_up, g_up, k_k, k_a, r_k,
           gn_g, gn_b, w_o, ln1_g, ln1_b, w_gu, w_down, ln2_g, ln2_b):
    bsz, t, dm = x.shape
    depth = w_in.shape[0]
    lw_n, la_n, lg_n = w_up.shape[1], a_up.shape[1], g_up.shape[1]
    assert dm % (2 * LANES) == 0 and t % CHUNK == 0
    assert lw_n <= LANES and la_n <= LANES and lg_n <= 2 * LANES
    assert w_in.shape[2] == 8 * dm + lw_n + la_n + lg_n
    alpha = (2.0 * depth) ** 0.25
    tn = 2 * LANES
    tm_in = _pick(t, (512, 256, 128, 64))
    tb = _pick(t, (128, 64))
    m = bsz * t
    tm = _pick(m, (512, 256, 128, 64))
    tf = _pick(w_down.shape[1], (512, 256, 128))

    for l in range(depth):
        wi = w_in[l]
        c0 = 6 * dm
        wi_bf = wi.astype(BF16)
        g0 = c0 + lw_n + la_n + lg_n
        nj = dm // tn
        w_groups = [(wi_bf, 0), (wi_bf, nj), (wi_bf, 2 * nj), (wi_bf[:, g0:g0 + dm], 0),
                    (wi_bf, 3 * nj), (wi_bf, 4 * nj), (wi_bf, 5 * nj), (wi_bf[:, g0 + dm:], 0)]

        def lane_pad(a, n):
            return jnp.pad(a, ((0, 0), (0, n - a.shape[1])))

        wl = jnp.concatenate([
            lane_pad(wi[:, c0:c0 + lw_n], LANES),
            lane_pad(wi[:, c0 + lw_n:c0 + lw_n + la_n], LANES),
            lane_pad(wi[:, c0 + lw_n + la_n:c0 + lw_n + la_n + lg_n], 2 * LANES)],
            axis=1).astype(BF16)
        mu = shift_mu[l]
        s0 = 3 * dm
        mu_l = jnp.concatenate([
            jnp.pad(mu[s0:s0 + lw_n], (0, LANES - lw_n)),
            jnp.pad(mu[s0 + lw_n:s0 + lw_n + la_n], (0, LANES - la_n)),
            jnp.pad(mu[s0 + lw_n + la_n:], (0, 2 * LANES - lg_n))])[None, :]
        wup = jnp.concatenate([_pad_rows(w_up[l], LANES), _pad_rows(a_up[l], LANES),
                               _pad_rows(g_up[l], 2 * LANES)], axis=0).astype(BF16)
        prm = _pad_rows(jnp.stack([
            mu[0:dm], mu[dm:2 * dm], mu[2 * dm:3 * dm],
            conv_w[l, 0], conv_w[l, 1], conv_w[l, 2],
            w0[l], a0[l], k_k[l], k_a[l], r_k[l].reshape(-1)]), 16)

        r, lw, k, v, kk, a, c1, og = _mixer_in(
            x, w_groups, wl, wup, prm, mu_l, tm=tm_in, tn=tn)
        gn = _pad_rows(jnp.stack([gn_g[l], gn_b[l]]), SUBLANES)
        ln1 = _pad_rows(jnp.stack([ln1_g[l], ln1_b[l]]), SUBLANES)
        x1, x1_bf = _wkv(r, lw, k, v, kk, a, c1, og, gn, x, w_o[l].astype(BF16), ln1,
                         alpha=alpha, tb=tb)
        ln2 = _pad_rows(jnp.stack([ln2_g[l], ln2_b[l]]), SUBLANES)
        x = _ffn(x1_bf.reshape(m, dm), x1.reshape(m, dm), w_gu[l].astype(BF16),
                 w_down[l].astype(BF16), ln2, alpha=alpha, tm=tm, tf=tf).reshape(bsz, t, dm)
    return x
```

```python
import functools
import math

import jax
import jax.numpy as jnp
from jax import lax
from jax.experimental import pallas as pl
from jax.experimental.pallas import tpu as pltpu

HEAD_SIZE = 64
LOG2_HEAD = 6
LOG_DECAY_SCALE = -math.exp(-0.5)
LN_EPS = 1e-5
GN_EPS = 64e-5
LANES = 128
SUBLANES = 8
CHUNK = 64
VMEM_LIMIT_BYTES = 56 * 1024 * 1024

F32 = jnp.float32
BF16 = jnp.bfloat16


def _mm(a, b):
    return jnp.dot(a.astype(BF16), b.astype(BF16), preferred_element_type=F32)


def _mm_nt(a, b):
    return lax.dot_general(a.astype(BF16), b.astype(BF16), (((1,), (1,)), ((), ())),
                           preferred_element_type=F32)


def _split2(a):
    hi = a.astype(BF16)
    lo = (a - hi.astype(F32)).astype(BF16)
    return hi, lo


def _mm_exact_lhs(a_exact, b):
    d = functools.partial(jnp.dot, preferred_element_type=F32)
    h, l = _split2(b)
    return d(a_exact, h) + d(a_exact, l)


def _iota_shr(shape, dim, log2_div):
    return lax.shift_right_logical(lax.broadcasted_iota(jnp.int32, shape, dim), log2_div)


def _head_ones(n):
    r = _iota_shr((n, n), 0, LOG2_HEAD)
    c = _iota_shr((n, n), 1, LOG2_HEAD)
    return jnp.where(r == c, 1.0, 0.0).astype(BF16)


N_GROUPS = 8
N_CARRY = 4
LORA_PAD = 512


def _mixer_in_kernel(x_ref, *refs, tm, tn):
    w_refs = refs[:N_GROUPS]
    wl_ref, wup_ref, p_ref, mul_ref = refs[N_GROUPS:N_GROUPS + 4]
    r_ref, lw_ref, k_ref, v_ref, kk_ref, a_ref, c1_ref, og_ref = refs[N_GROUPS + 4:N_GROUPS + 12]
    xb_ref, lora_ref, carry_ref, carry_l_ref = refs[N_GROUPS + 12:]
    i = pl.program_id(1)
    j = pl.program_id(2)
    first = i == 0
    row = lax.broadcasted_iota(jnp.int32, (tm, tn), 0)

    @pl.when(j == 0)
    def _():
        xb_ref[...] = x_ref[0].astype(BF16)

    xt = xb_ref[...]

    def prev1(p, c8):
        return jnp.where(row == 0, c8[SUBLANES - 1:SUBLANES, :], pltpu.roll(p, 1, 0))

    def prev2(p, c8):
        rolled = pltpu.roll(p, 2, 0)
        rolled = jnp.where(row == 1, c8[SUBLANES - 1:SUBLANES, :], rolled)
        return jnp.where(row == 0, c8[SUBLANES - 2:SUBLANES - 1, :], rolled)

    @pl.when(j == 0)
    def _():
        pl_ = jnp.dot(xt, wl_ref[...], preferred_element_type=F32)
        cl = jnp.where(first, 0.0, carry_l_ref[...])
        rowl = lax.broadcasted_iota(jnp.int32, (tm, LORA_PAD), 0)
        prev = jnp.where(rowl == 0, cl[SUBLANES - 1:SUBLANES, :], pltpu.roll(pl_, 1, 0))
        carry_l_ref[...] = pl_[tm - SUBLANES:, :]
        z = pl_ + mul_ref[...] * (prev - pl_)
        lora_ref[:, 0:LANES] = jnp.tanh(z[:, 0:LANES]).astype(BF16)
        lora_ref[:, LANES:2 * LANES] = z[:, LANES:2 * LANES].astype(BF16)
        lora_ref[:, 2 * LANES:] = jax.nn.sigmoid(z[:, 2 * LANES:]).astype(BF16)

    prm = p_ref[...]
    mu_r, mu_k, mu_v = prm[0:1], prm[1:2], prm[2:3]
    cw0, cw1, cw2 = prm[3:4], prm[4:5], prm[5:6]
    w0, a0, k_k, k_a, r_k = prm[6:7], prm[7:8], prm[8:9], prm[9:10], prm[10:11]
    carry = [jnp.where(first, 0.0, carry_ref[j, q]) for q in range(N_CARRY)]
    w_cb, w_cc, w_ch, w_gc, w_r, w_k, w_v, w_gr = w_refs
    d = functools.partial(jnp.dot, preferred_element_type=F32)

    wup = wup_ref[...]
    w_pre = w0 + d(lora_ref[:, 0:LANES], wup[0:LANES])
    a_pre = a0 + d(lora_ref[:, LANES:2 * LANES], wup[LANES:2 * LANES])
    g = d(lora_ref[:, 2 * LANES:], wup[2 * LANES:])
    lw = LOG_DECAY_SCALE * jax.nn.sigmoid(w_pre)
    a = jax.nn.sigmoid(a_pre)

    p_r, p_k, p_v = d(xt, w_r[...]), d(xt, w_k[...]), d(xt, w_v[...])
    r = p_r + mu_r * (prev1(p_r, carry[1]) - p_r)
    k = p_k + mu_k * (prev1(p_k, carry[2]) - p_k)
    v = p_v + mu_v * (prev1(p_v, carry[3]) - p_v)
    p_gr, p_cc, p_ch = d(xt, w_gr[...]), d(xt, w_cc[...]), d(xt, w_ch[...])

    ones = _head_ones(tn)
    kraw = k * k_k
    ss = _mm(kraw * kraw, ones)
    kk = kraw / jnp.maximum(jnp.sqrt(ss), 1e-12)
    k_mod = k * (1.0 + (a - 1.0) * k_a)
    bonus = _mm(r * k_mod * r_k, ones) * v
    og = g * jax.nn.sigmoid(p_gr)

    u = p_cc * p_ch
    conv = cw2 * u + cw1 * prev1(u, carry[0]) + cw0 * prev2(u, carry[0])
    p_gc, p_cb = d(xt, w_gc[...]), d(xt, w_cb[...])
    y_conv = (jax.nn.sigmoid(p_gc) * conv) * p_cb
    for q, val in enumerate((u, p_r, p_k, p_v)):
        carry_ref[j, q] = val[tm - SUBLANES:, :]

    r_ref[0] = r
    lw_ref[0] = lw
    k_ref[0] = k_mod
    v_ref[0] = v.astype(v_ref.dtype)
    kk_ref[0] = kk
    a_ref[0] = a
    c1_ref[0] = (y_conv + og * bonus).astype(c1_ref.dtype)
    og_ref[0] = og.astype(og_ref.dtype)


def _mixer_in(x, w_groups, wl, wup, prm, mu_l, *, tm, tn):
    bsz, t, dm = x.shape
    nj = dm // tn
    grid = (bsz, t // tm, nj)
    out_sds = jax.ShapeDtypeStruct((bsz, t, dm), F32)
    out_bf = jax.ShapeDtypeStruct((bsz, t, dm), BF16)
    out_spec = pl.BlockSpec((1, tm, tn), lambda b, i, j: (b, i, j))
    return pl.pallas_call(
        functools.partial(_mixer_in_kernel, tm=tm, tn=tn),
        grid=grid,
        in_specs=[
            pl.BlockSpec((1, tm, dm), lambda b, i, j: (b, i, 0)),
            *[pl.BlockSpec((dm, tn), lambda b, i, j, o=o: (0, o + j)) for _, o in w_groups],
            pl.BlockSpec((dm, LORA_PAD), lambda b, i, j: (0, 0)),
            pl.BlockSpec((LORA_PAD, tn), lambda b, i, j: (0, j)),
            pl.BlockSpec((16, tn), lambda b, i, j: (0, j)),
            pl.BlockSpec((1, LORA_PAD), lambda b, i, j: (0, 0)),
        ],
        out_specs=[out_spec] * 8,
        out_shape=[out_sds, out_sds, out_sds, out_bf, out_sds, out_sds, out_bf, out_bf],
        scratch_shapes=[
            pltpu.VMEM((tm, dm), BF16),
            pltpu.VMEM((tm, LORA_PAD), BF16),
            pltpu.VMEM((nj, N_CARRY, SUBLANES, tn), F32),
            pltpu.VMEM((SUBLANES, LORA_PAD), F32),
        ],
        compiler_params=pltpu.CompilerParams(
            dimension_semantics=("arbitrary", "arbitrary", "arbitrary"),
            vmem_limit_bytes=VMEM_LIMIT_BYTES),
        name="mixer_in",
    )(x, *[w for w, _ in w_groups], wl, wup, prm, mu_l)


def _wkv_block(r, lw, k, v, kk, a, z0, consts, n_chunks):
    cs = consts
    c = CHUNK
    idx = range(len(r))

    def stack(x):
        xb = x.astype(BF16)
        m0, m1 = (cs["m0"], cs["m1"]) if x.shape[1] == LANES else (cs["m0_2"], cs["m1_2"])
        return jnp.concatenate([xb * m0, xb * m1], axis=0)

    def bdiag(xw):
        xb = xw.astype(BF16)
        return jnp.concatenate([xb, xb], axis=0) * cs["bd_bf"]

    def each(fn, *lists):
        return [fn(*args) for args in zip(*lists)]

    cum = each(lambda x: _mm_exact_lhs(cs["tri"], x), lw)
    beta = each(lambda x, y: x * y, kk, a)
    e_inv = each(lambda x: jnp.exp(-x), cum)
    e_rem = each(lambda x: jnp.exp(x[c - 1:c, :] - x), cum)
    rt = each(lambda x, y: x * jnp.exp(y), r, cum)
    at = each(lambda x, y, z_: -x * jnp.exp(y - z_), kk, cum, lw)
    kt = each(lambda x, y: x * y, k, e_inv)
    bt = each(lambda x, y: x * y, beta, e_inv)
    kh = each(lambda x, y: x * y, k, e_rem)
    bh = each(lambda x, y: x * y, beta, e_rem)

    s = each(lambda a_, r_, b_, k_: _mm_nt(jnp.concatenate([a_, r_], axis=0),
                                           jnp.concatenate([stack(b_), stack(k_)], axis=0)),
             at, rt, bt, kt)
    lab = each(lambda x: jnp.where(cs["strict"], x[0:c, 0:2 * c], 0.0), s)
    urb = each(lambda x: jnp.where(cs["incl"], x[c:, 0:2 * c], 0.0), s)

    def twice(x):
        xb = x.astype(BF16)
        return jnp.concatenate([xb, xb], axis=0)

    lab2 = each(twice, lab)
    l8 = each(lambda x: jnp.where(cs["blk8"], x, 0.0), lab)
    l8_2 = each(lambda x, x2: _mm(x, x2 * cs["blk8_bd"]), l8, lab2)
    l8_4 = each(lambda x: _mm(x, bdiag(x)), l8_2)
    tw = each(lambda x, y: _mm(cs["eye"] + x, bdiag(cs["eye"] + y)), l8, l8_2)
    tw = each(lambda x, y: _mm(x, bdiag(cs["eye"] + y)), tw, l8_4)
    for join_bd in cs["joins_bd"]:
        half = each(lambda t_, x2: _mm(t_, x2 * join_bd), tw, lab2)
        tw = each(lambda t_, h_: t_ + _mm(h_, bdiag(t_)), tw, half)

    nv_uv = each(lambda x, v_: _mm(jnp.where(cs["strict_incl"], x[:, 2 * c:], 0.0), stack(v_)),
                 s, v)
    ah_d = each(lambda t_, a_, n_: _mm(t_, stack(jnp.concatenate([a_, n_[0:c]], axis=1))),
                tw, at, nv_uv)
    u2 = each(lambda u_, x: _mm(u_, stack(x)), urb, ah_d)
    rh = each(lambda r_, u_: r_ + u_[:, 0:LANES], rt, u2)
    e = each(lambda u_, w_: u_[:, LANES:] + w_[c:], u2, nv_uv)
    zeros = jnp.zeros((c, LANES), BF16)
    gf = each(lambda b_, k_, x, v_: _mm(
        jnp.concatenate([b_, k_], axis=0).T,
        jnp.concatenate([x.astype(BF16), jnp.concatenate([zeros, v_.astype(BF16)], axis=1)],
                        axis=0)), bh, kh, ah_d, v)
    gl = each(lambda x: jnp.where(cs["bd"], x[:, 0:LANES], 0.0), gf)
    f = each(lambda x: jnp.where(cs["bd"], x[:, LANES:], 0.0), gf)
    decay_col = each(lambda x: jnp.exp(x.T[:, c - 1:c]), cum)

    ys = [None] * len(r)
    states = list(z0)
    for ci in range(n_chunks):
        for p in range(len(states)):
            n = p * n_chunks + ci
            z = states[p]
            rz_gz = _mm(jnp.concatenate([rh[n], gl[n]], axis=0), z)
            ys[n] = rz_gz[0:c] + e[n]
            states[p] = decay_col[n] * z + rz_gz[c:] + f[n]
    return ys, states


def _wkv_consts():
    c = CHUNK
    assert c == HEAD_SIZE and 2 * c == LANES
    t = lax.broadcasted_iota(jnp.int32, (c, 2 * c), 0)
    i = jnp.bitwise_and(lax.broadcasted_iota(jnp.int32, (c, 2 * c), 1), c - 1)
    rr = lax.broadcasted_iota(jnp.int32, (c, c), 0)
    cc = lax.broadcasted_iota(jnp.int32, (c, c), 1)
    def head_lane_masks(width):
        lane = jnp.bitwise_and(lax.broadcasted_iota(jnp.int32, (c, width), 1), LANES - 1)
        return (jnp.where(lane < HEAD_SIZE, 1.0, 0.0).astype(BF16),
                jnp.where(lane < HEAD_SIZE, 0.0, 1.0).astype(BF16))

    m0, m1 = head_lane_masks(LANES)
    m0_2, m1_2 = head_lane_masks(2 * LANES)
    bd = _iota_shr((2 * c, 2 * c), 0, LOG2_HEAD) == _iota_shr((2 * c, 2 * c), 1, LOG2_HEAD)
    t2 = lax.broadcasted_iota(jnp.int32, (2 * c, 2 * c), 0)
    i2 = jnp.bitwise_and(lax.broadcasted_iota(jnp.int32, (2 * c, 2 * c), 1), c - 1)
    strict_incl = jnp.logical_or(jnp.logical_and(t2 < c, i2 < t2),
                                 jnp.logical_and(t2 >= c, i2 <= t2 - c))
    tt = jnp.bitwise_and(t2, c - 1)
    blk2 = [lax.shift_right_logical(tt, s) == lax.shift_right_logical(i2, s) for s in (3, 4, 5)]
    out2 = [jnp.logical_not(b) for b in blk2]
    joins = [jnp.logical_and(blk2[1], out2[0]), jnp.logical_and(blk2[2], out2[1]), out2[2]]

    def bd_mask(m):
        return jnp.where(jnp.logical_and(bd, m), 1.0, 0.0).astype(BF16)

    return dict(
        strict=i < t, incl=i <= t, strict_incl=strict_incl,
        eye=jnp.where(i == t, 1.0, 0.0).astype(F32),
        blk8=lax.shift_right_logical(t, 3) == lax.shift_right_logical(i, 3),
        blk8_bd=bd_mask(blk2[0]), joins_bd=[bd_mask(m) for m in joins],
        tri=jnp.where(cc <= rr, 1.0, 0.0).astype(BF16),
        m0=m0, m1=m1, m0_2=m0_2, m1_2=m1_2,
        bd=bd, bd_bf=jnp.where(bd, 1.0, 0.0).astype(BF16))


def _wkv_kernel(r_ref, lw_ref, k_ref, v_ref, kk_ref, a_ref, c1_ref, og_ref, gn_ref,
                x_ref, wo_ref, ln_ref, o_ref, obf_ref, z_ref, *, n_chunks, n_pairs, alpha):
    @pl.when(pl.program_id(1) == 0)
    def _():
        z_ref[...] = jnp.zeros_like(z_ref)

    def chunks(ref):
        return [ref[0, ci * CHUNK:(ci + 1) * CHUNK, p * LANES:(p + 1) * LANES]
                for p in range(n_pairs) for ci in range(n_chunks)]

    ys, states = _wkv_block(chunks(r_ref), chunks(lw_ref), chunks(k_ref), chunks(v_ref),
                            chunks(kk_ref), chunks(a_ref), [z_ref[p] for p in range(n_pairs)],
                            _wkv_consts(), n_chunks)
    ones = _head_ones(LANES)
    inv_n = 1.0 / HEAD_SIZE
    y = jnp.concatenate(ys, axis=0)
    mu = _mm(y, ones) * inv_n
    yc = y - mu
    var = _mm(yc * yc, ones) * inv_n
    yn = yc * lax.rsqrt(var + GN_EPS)
    tb = n_chunks * CHUNK
    merged = []
    for p in range(n_pairs):
        z_ref[p] = states[p]
        ls = slice(p * LANES, (p + 1) * LANES)
        yp = yn[p * tb:(p + 1) * tb] * gn_ref[0:1, ls] + gn_ref[1:2, ls]
        merged.append((c1_ref[0, :, ls].astype(F32)
                       + og_ref[0, :, ls].astype(F32) * yp).astype(BF16))
    h = alpha * x_ref[0] + jnp.dot(jnp.concatenate(merged, axis=1), wo_ref[...],
                                   preferred_element_type=F32)
    x1 = _layer_norm(h, ln_ref[0:1, :], ln_ref[1:2, :])
    o_ref[0] = x1
    obf_ref[0] = x1.astype(BF16)


def _layer_norm(h, g, b):
    mu = jnp.mean(h, axis=-1, keepdims=True)
    hc = h - mu
    var = jnp.mean(hc * hc, axis=-1, keepdims=True)
    return hc * lax.rsqrt(var + LN_EPS) * g + b


def _wkv(r, lw, k, v, kk, a, c1, og, gn, x, wo, ln, *, alpha, tb):
    bsz, t, dm = r.shape
    n_pairs = dm // LANES
    spec = pl.BlockSpec((1, tb, dm), lambda b, i: (b, i, 0))
    row_spec = pl.BlockSpec((SUBLANES, dm), lambda b, i: (0, 0))
    return pl.pallas_call(
        functools.partial(_wkv_kernel, n_chunks=tb // CHUNK, n_pairs=n_pairs, alpha=alpha),
        grid=(bsz, t // tb),
        in_specs=[spec] * 8 + [row_spec, spec, pl.BlockSpec((dm, dm), lambda b, i: (0, 0)),
                               row_spec],
        out_specs=[spec, spec],
        out_shape=[jax.ShapeDtypeStruct((bsz, t, dm), F32),
                   jax.ShapeDtypeStruct((bsz, t, dm), BF16)],
        scratch_shapes=[pltpu.VMEM((n_pairs, LANES, LANES), F32)],
        compiler_params=pltpu.CompilerParams(
            dimension_semantics=("arbitrary", "arbitrary"),
            vmem_limit_bytes=VMEM_LIMIT_BYTES),
        name="wkv_out",
    )(r, lw, k, v, kk, a, c1, og, gn, x, wo, ln)


def _ffn_kernel(xb_ref, x_ref, wg_ref, wu_ref, wd_ref, ln_ref, o_ref, acc_ref, *, alpha):
    f = pl.program_id(1)

    @pl.when(jnp.logical_and(pl.program_id(0) == 0, f == 0))
    def _():
        acc_ref[...] = jnp.zeros_like(acc_ref)

    xb = xb_ref[...]
    gate = jnp.dot(xb, wg_ref[...], preferred_element_type=F32)
    up = jnp.dot(xb, wu_ref[...], preferred_element_type=F32)
    hid = (gate * jax.nn.sigmoid(gate) * up).astype(BF16)
    part = jnp.dot(hid, wd_ref[...], preferred_element_type=F32)
    acc_ref[...] += part

    @pl.when(f == pl.num_programs(1) - 1)
    def _():
        h = alpha * x_ref[...] + acc_ref[...]
        o_ref[...] = _layer_norm(h, ln_ref[0:1, :], ln_ref[1:2, :])
        acc_ref[...] = jnp.zeros_like(acc_ref)


def _ffn(x_bf, x2d, w_gu, w_down, ln, *, alpha, tm, tf):
    m, dm = x2d.shape
    dff = w_down.shape[0]
    nf = dff // tf
    return pl.pallas_call(
        functools.partial(_ffn_kernel, alpha=alpha),
        grid=(m // tm, nf),
        in_specs=[
            pl.BlockSpec((tm, dm), lambda i, f: (i, 0)),
            pl.BlockSpec((tm, dm), lambda i, f: (i, 0)),
            pl.BlockSpec((dm, tf), lambda i, f: (0, f)),
            pl.BlockSpec((dm, tf), lambda i, f: (0, nf + f)),
            pl.BlockSpec((tf, dm), lambda i, f: (f, 0)),
            pl.BlockSpec((SUBLANES, dm), lambda i, f: (0, 0)),
        ],
        out_specs=pl.BlockSpec((tm, dm), lambda i, f: (i, 0)),
        out_shape=jax.ShapeDtypeStruct((m, dm), F32),
        scratch_shapes=[pltpu.VMEM((tm, dm), F32)],
        compiler_params=pltpu.CompilerParams(
            dimension_semantics=("arbitrary", "arbitrary"), vmem_limit_bytes=VMEM_LIMIT_BYTES),
        name="ffn",
    )(x_bf, x2d, w_gu, w_gu, w_down, ln)


def _pad_rows(a, n):
    return jnp.pad(a, ((0, n - a.shape[0]), (0, 0)))


def _pick(n, candidates):
    for c in candidates:
        if n % c == 0:
            return c
    raise ValueError(f"no tile size for extent {n}")


def kernel(x, w_in, shift_mu, conv_w, w0, w_up, a0, a_up, g_up, k_k, k_a, r_k,
           gn_g, gn_b, w_o, ln1_g, ln1_b, w_gu, w_down, ln2_g, ln2_b):
    bsz, t, dm = x.shape
    depth = w_in.shape[0]
    lw_n, la_n, lg_n = w_up.shape[1], a_up.shape[1], g_up.shape[1]
    assert dm % (2 * LANES) == 0 and t % CHUNK == 0
    assert lw_n <= LANES and la_n <= LANES and lg_n <= 2 * LANES
    assert w_in.shape[2] == 8 * dm + lw_n + la_n + lg_n
    alpha = (2.0 * depth) ** 0.25
    tn = 2 * LANES
    tm_in = _pick(t, (512, 256, 128, 64))
    tb = _pick(t, (128, 64))
    m = bsz * t
    tm = _pick(m, (512, 256, 128, 64))
    tf = _pick(w_down.shape[1], (512, 256, 128))

    for l in range(depth):
        wi = w_in[l]
        c0 = 6 * dm
        wi_bf = wi.astype(BF16)
        g0 = c0 + lw_n + la_n + lg_n
        nj = dm // tn
        w_groups = [(wi_bf, 0), (wi_bf, nj), (wi_bf, 2 * nj), (wi_bf[:, g0:g0 + dm], 0),
                    (wi_bf, 3 * nj), (wi_bf, 4 * nj), (wi_bf, 5 * nj), (wi_bf[:, g0 + dm:], 0)]

        def lane_pad(a, n):
            return jnp.pad(a, ((0, 0), (0, n - a.shape[1])))

        wl = jnp.concatenate([
            lane_pad(wi[:, c0:c0 + lw_n], LANES),
            lane_pad(wi[:, c0 + lw_n:c0 + lw_n + la_n], LANES),
            lane_pad(wi[:, c0 + lw_n + la_n:c0 + lw_n + la_n + lg_n], 2 * LANES)],
            axis=1).astype(BF16)
        mu = shift_mu[l]
        s0 = 3 * dm
        mu_l = jnp.concatenate([
            jnp.pad(mu[s0:s0 + lw_n], (0, LANES - lw_n)),
            jnp.pad(mu[s0 + lw_n:s0 + lw_n + la_n], (0, LANES - la_n)),
            jnp.pad(mu[s0 + lw_n + la_n:], (0, 2 * LANES - lg_n))])[None, :]
        wup = jnp.concatenate([_pad_rows(w_up[l], LANES), _pad_rows(a_up[l], LANES),
                               _pad_rows(g_up[l], 2 * LANES)], axis=0).astype(BF16)
        prm = _pad_rows(jnp.stack([
            mu[0:dm], mu[dm:2 * dm], mu[2 * dm:3 * dm],
            conv_w[l, 0], conv_w[l, 1], conv_w[l, 2],
            w0[l], a0[l], k_k[l], k_a[l], r_k[l].reshape(-1)]), 16)

        r, lw, k, v, kk, a, c1, og = _mixer_in(
            x, w_groups, wl, wup, prm, mu_l, tm=tm_in, tn=tn)
        gn = _pad_rows(jnp.stack([gn_g[l], gn_b[l]]), SUBLANES)
        ln1 = _pad_rows(jnp.stack([ln1_g[l], ln1_b[l]]), SUBLANES)
        x1, x1_bf = _wkv(r, lw, k, v, kk, a, c1, og, gn, x, w_o[l].astype(BF16), ln1,
                         alpha=alpha, tb=tb)
        ln2 = _pad_rows(jnp.stack([ln2_g[l], ln2_b[l]]), SUBLANES)
        x = _ffn(x1_bf.reshape(m, dm), x1.reshape(m, dm), w_gu[l].astype(BF16),
                 w_down[l].astype(BF16), ln2, alpha=alpha, tm=tm, tf=tf).reshape(bsz, t, dm)
    return x
```

```python
import functools
import math

import jax
import jax.numpy as jnp
from jax import lax
from jax.experimental import pallas as pl
from jax.experimental.pallas import tpu as pltpu

HEAD_SIZE = 64
LOG2_HEAD = 6
LOG_DECAY_SCALE = -math.exp(-0.5)
LN_EPS = 1e-5
GN_EPS = 64e-5
LANES = 128
SUBLANES = 8
CHUNK = 64
VMEM_LIMIT_BYTES = 56 * 1024 * 1024

F32 = jnp.float32
BF16 = jnp.bfloat16


def _mm(a, b):
    return jnp.dot(a.astype(BF16), b.astype(BF16), preferred_element_type=F32)


def _mm_nt(a, b):
    return lax.dot_general(a.astype(BF16), b.astype(BF16), (((1,), (1,)), ((), ())),
                           preferred_element_type=F32)


def _split2(a):
    hi = a.astype(BF16)
    lo = (a - hi.astype(F32)).astype(BF16)
    return hi, lo


def _mm_exact_lhs(a_exact, b):
    d = functools.partial(jnp.dot, preferred_element_type=F32)
    h, l = _split2(b)
    return d(a_exact, h) + d(a_exact, l)


def _iota_shr(shape, dim, log2_div):
    return lax.shift_right_logical(lax.broadcasted_iota(jnp.int32, shape, dim), log2_div)


def _head_ones(n):
    r = _iota_shr((n, n), 0, LOG2_HEAD)
    c = _iota_shr((n, n), 1, LOG2_HEAD)
    return jnp.where(r == c, 1.0, 0.0).astype(BF16)


N_GROUPS = 8
N_CARRY = 4
LORA_PAD = 512


def _mixer_in_kernel(x_ref, *refs, tm, tn):
    w_refs = refs[:N_GROUPS]
    wl_ref, wup_ref, p_ref, mul_ref = refs[N_GROUPS:N_GROUPS + 4]
    r_ref, lw_ref, k_ref, v_ref, kk_ref, a_ref, c1_ref, og_ref = refs[N_GROUPS + 4:N_GROUPS + 12]
    xb_ref, lora_ref, carry_ref, carry_l_ref = refs[N_GROUPS + 12:]
    i = pl.program_id(1)
    j = pl.program_id(2)
    first = i == 0
    row = lax.broadcasted_iota(jnp.int32, (tm, tn), 0)

    @pl.when(j == 0)
    def _():
        xb_ref[...] = x_ref[0].astype(BF16)

    def xdot(w_ref):
        return jnp.dot(xb_ref[...], w_ref[...], preferred_element_type=F32)

    def prev1(p, c8):
        return jnp.where(row == 0, c8[SUBLANES - 1:SUBLANES, :], pltpu.roll(p, 1, 0))

    def prev2(p, c8):
        rolled = pltpu.roll(p, 2, 0)
        rolled = jnp.where(row == 1, c8[SUBLANES - 1:SUBLANES, :], rolled)
        return jnp.where(row == 0, c8[SUBLANES - 2:SUBLANES - 1, :], rolled)

    @pl.when(j == 0)
    def _():
        pl_ = xdot(wl_ref)
        cl = jnp.where(first, 0.0, carry_l_ref[...])
        rowl = lax.broadcasted_iota(jnp.int32, (tm, LORA_PAD), 0)
        prev = jnp.where(rowl == 0, cl[SUBLANES - 1:SUBLANES, :], pltpu.roll(pl_, 1, 0))
        carry_l_ref[...] = pl_[tm - SUBLANES:, :]
        z = pl_ + mul_ref[...] * (prev - pl_)
        lora_ref[:, 0:LANES] = jnp.tanh(z[:, 0:LANES]).astype(BF16)
        lora_ref[:, LANES:2 * LANES] = z[:, LANES:2 * LANES].astype(BF16)
        lora_ref[:, 2 * LANES:] = jax.nn.sigmoid(z[:, 2 * LANES:]).astype(BF16)

    prm = p_ref[...]
    mu_r, mu_k, mu_v = prm[0:1], prm[1:2], prm[2:3]
    cw0, cw1, cw2 = prm[3:4], prm[4:5], prm[5:6]
    w0, a0, k_k, k_a, r_k = prm[6:7], prm[7:8], prm[8:9], prm[9:10], prm[10:11]
    carry = [jnp.where(first, 0.0, carry_ref[j, q]) for q in range(N_CARRY)]
    w_cb, w_cc, w_ch, w_gc, w_r, w_k, w_v, w_gr = w_refs
    d = functools.partial(jnp.dot, preferred_element_type=F32)

    wup = wup_ref[...]
    w_pre = w0 + d(lora_ref[:, 0:LANES], wup[0:LANES])
    a_pre = a0 + d(lora_ref[:, LANES:2 * LANES], wup[LANES:2 * LANES])
    g = d(lora_ref[:, 2 * LANES:], wup[2 * LANES:])
    lw = LOG_DECAY_SCALE * jax.nn.sigmoid(w_pre)
    a = jax.nn.sigmoid(a_pre)

    p_r, p_k, p_v = xdot(w_r), xdot(w_k), xdot(w_v)
    r = p_r + mu_r * (prev1(p_r, carry[1]) - p_r)
    k = p_k + mu_k * (prev1(p_k, carry[2]) - p_k)
    v = p_v + mu_v * (prev1(p_v, carry[3]) - p_v)
    p_gr, p_cc, p_ch = xdot(w_gr), xdot(w_cc), xdot(w_ch)

    ones = _head_ones(tn)
    kraw = k * k_k
    ss = _mm(kraw * kraw, ones)
    kk = kraw / jnp.maximum(jnp.sqrt(ss), 1e-12)
    k_mod = k * (1.0 + (a - 1.0) * k_a)
    bonus = _mm(r * k_mod * r_k, ones) * v
    og = g * jax.nn.sigmoid(p_gr)

    u = p_cc * p_ch
    conv = cw2 * u + cw1 * prev1(u, carry[0]) + cw0 * prev2(u, carry[0])
    p_gc, p_cb = xdot(w_gc), xdot(w_cb)
    y_conv = (jax.nn.sigmoid(p_gc) * conv) * p_cb
    for q, val in enumerate((u, p_r, p_k, p_v)):
        carry_ref[j, q] = val[tm - SUBLANES:, :]

    r_ref[0] = r
    lw_ref[0] = lw
    k_ref[0] = k_mod
    v_ref[0] = v.astype(v_ref.dtype)
    kk_ref[0] = kk
    a_ref[0] = a
    c1_ref[0] = (y_conv + og * bonus).astype(c1_ref.dtype)
    og_ref[0] = og.astype(og_ref.dtype)


def _mixer_in(x, w_groups, wl, wup, prm, mu_l, *, tm, tn):
    bsz, t, dm = x.shape
    nj = dm // tn
    grid = (bsz, t // tm, nj)
    out_sds = jax.ShapeDtypeStruct((bsz, t, dm), F32)
    out_bf = jax.ShapeDtypeStruct((bsz, t, dm), BF16)
    out_spec = pl.BlockSpec((1, tm, tn), lambda b, i, j: (b, i, j))
    return pl.pallas_call(
        functools.partial(_mixer_in_kernel, tm=tm, tn=tn),
        grid=grid,
        in_specs=[
            pl.BlockSpec((1, tm, dm), lambda b, i, j: (b, i, 0)),
            *[pl.BlockSpec((dm, tn), lambda b, i, j, o=o: (0, o + j)) for _, o in w_groups],
            pl.BlockSpec((dm, LORA_PAD), lambda b, i, j: (0, 0)),
            pl.BlockSpec((LORA_PAD, tn), lambda b, i, j: (0, j)),
            pl.BlockSpec((16, tn), lambda b, i, j: (0, j)),
            pl.BlockSpec((1, LORA_PAD), lambda b, i, j: (0, 0)),
        ],
        out_specs=[out_spec] * 8,
        out_shape=[out_sds, out_sds, out_sds, out_bf, out_sds, out_sds, out_bf, out_bf],
        scratch_shapes=[
            pltpu.VMEM((tm, dm), BF16),
            pltpu.VMEM((tm, LORA_PAD), BF16),
            pltpu.VMEM((nj, N_CARRY, SUBLANES, tn), F32),
            pltpu.VMEM((SUBLANES, LORA_PAD), F32),
        ],
        compiler_params=pltpu.CompilerParams(
            dimension_semantics=("arbitrary", "arbitrary", "arbitrary"),
            vmem_limit_bytes=VMEM_LIMIT_BYTES),
        name="mixer_in",
    )(x, *[w for w, _ in w_groups], wl, wup, prm, mu_l)


def _wkv_block(r, lw, k, v, kk, a, z0, consts, n_chunks):
    cs = consts
    c = CHUNK
    idx = range(len(r))

    def stack(x):
        xb = x.astype(BF16)
        m0, m1 = (cs["m0"], cs["m1"]) if x.shape[1] == LANES else (cs["m0_2"], cs["m1_2"])
        return jnp.concatenate([xb * m0, xb * m1], axis=0)

    def bdiag(xw):
        xb = xw.astype(BF16)
        return jnp.concatenate([xb, xb], axis=0) * cs["bd_bf"]

    def each(fn, *lists):
        return [fn(*args) for args in zip(*lists)]

    cum = each(lambda x: _mm_exact_lhs(cs["tri"], x), lw)
    beta = each(lambda x, y: x * y, kk, a)
    e_inv = each(lambda x: jnp.exp(-x), cum)
    e_rem = each(lambda x: jnp.exp(x[c - 1:c, :] - x), cum)
    rt = each(lambda x, y: x * jnp.exp(y), r, cum)
    at = each(lambda x, y, z_: -x * jnp.exp(y - z_), kk, cum, lw)
    kt = each(lambda x, y: x * y, k, e_inv)
    bt = each(lambda x, y: x * y, beta, e_inv)
    kh = each(lambda x, y: x * y, k, e_rem)
    bh = each(lambda x, y: x * y, beta, e_rem)

    s = each(lambda a_, r_, b_, k_: _mm_nt(jnp.concatenate([a_, r_], axis=0),
                                           jnp.concatenate([stack(b_), stack(k_)], axis=0)),
             at, rt, bt, kt)
    lab = each(lambda x: jnp.where(cs["strict"], x[0:c, 0:2 * c], 0.0), s)
    urb = each(lambda x: jnp.where(cs["incl"], x[c:, 0:2 * c], 0.0), s)

    def twice(x):
        xb = x.astype(BF16)
        return jnp.concatenate([xb, xb], axis=0)

    lab2 = each(twice, lab)
    l8 = each(lambda x: jnp.where(cs["blk8"], x, 0.0), lab)
    l8_2 = each(lambda x, x2: _mm(x, x2 * cs["blk8_bd"]), l8, lab2)
    l8_4 = each(lambda x: _mm(x, bdiag(x)), l8_2)
    tw = each(lambda x, y: _mm(cs["eye"] + x, bdiag(cs["eye"] + y)), l8, l8_2)
    tw = each(lambda x, y: _mm(x, bdiag(cs["eye"] + y)), tw, l8_4)
    for join_bd in cs["joins_bd"]:
        half = each(lambda t_, x2: _mm(t_, x2 * join_bd), tw, lab2)
        tw = each(lambda t_, h_: t_ + _mm(h_, bdiag(t_)), tw, half)

    nv_uv = each(lambda x, v_: _mm(jnp.where(cs["strict_incl"], x[:, 2 * c:], 0.0), stack(v_)),
                 s, v)
    ah_d = each(lambda t_, a_, n_: _mm(t_, stack(jnp.concatenate([a_, n_[0:c]], axis=1))),
                tw, at, nv_uv)
    u2 = each(lambda u_, x: _mm(u_, stack(x)), urb, ah_d)
    rh = each(lambda r_, u_: r_ + u_[:, 0:LANES], rt, u2)
    e = each(lambda u_, w_: u_[:, LANES:] + w_[c:], u2, nv_uv)
    zeros = jnp.zeros((c, LANES), BF16)
    gf = each(lambda b_, k_, x, v_: _mm(
        jnp.concatenate([b_, k_], axis=0).T,
        jnp.concatenate([x.astype(BF16), jnp.concatenate([zeros, v_.astype(BF16)], axis=1)],
                        axis=0)), bh, kh, ah_d, v)
    gl = each(lambda x: jnp.where(cs["bd"], x[:, 0:LANES], 0.0), gf)
    f = each(lambda x: jnp.where(cs["bd"], x[:, LANES:], 0.0), gf)
    decay_col = each(lambda x: jnp.exp(x.T[:, c - 1:c]), cum)

    ys = [None] * len(r)
    states = list(z0)
    for ci in range(n_chunks):
        for p in range(len(states)):
            n = p * n_chunks + ci
            z = states[p]
            rz_gz = _mm(jnp.concatenate([rh[n], gl[n]], axis=0), z)
            ys[n] = rz_gz[0:c] + e[n]
            states[p] = decay_col[n] * z + rz_gz[c:] + f[n]
    return ys, states


def _wkv_consts():
    c = CHUNK
    assert c == HEAD_SIZE and 2 * c == LANES
    t = lax.broadcasted_iota(jnp.int32, (c, 2 * c), 0)
    i = jnp.bitwise_and(lax.broadcasted_iota(jnp.int32, (c, 2 * c), 1), c - 1)
    rr = lax.broadcasted_iota(jnp.int32, (c, c), 0)
    cc = lax.broadcasted_iota(jnp.int32, (c, c), 1)
    def head_lane_masks(width):
        lane = jnp.bitwise_and(lax.broadcasted_iota(jnp.int32, (c, width), 1), LANES - 1)
        return (jnp.where(lane < HEAD_SIZE, 1.0, 0.0).astype(BF16),
                jnp.where(lane < HEAD_SIZE, 0.0, 1.0).astype(BF16))

    m0, m1 = head_lane_masks(LANES)
    m0_2, m1_2 = head_lane_masks(2 * LANES)
    bd = _iota_shr((2 * c, 2 * c), 0, LOG2_HEAD) == _iota_shr((2 * c, 2 * c), 1, LOG2_HEAD)
    t2 = lax.broadcasted_iota(jnp.int32, (2 * c, 2 * c), 0)
    i2 = jnp.bitwise_and(lax.broadcasted_iota(jnp.int32, (2 * c, 2 * c), 1), c - 1)
    strict_incl = jnp.logical_or(jnp.logical_and(t2 < c, i2 < t2),
                                 jnp.logical_and(t2 >= c, i2 <= t2 - c))
    tt = jnp.bitwise_and(t2, c - 1)
    blk2 = [lax.shift_right_logical(tt, s) == lax.shift_right_logical(i2, s) for s in (3, 4, 5)]
    out2 = [jnp.logical_not(b) for b in blk2]
    joins = [jnp.logical_and(blk2[1], out2[0]), jnp.logical_and(blk2[2], out2[1]), out2[2]]

    def bd_mask(m):
        return jnp.where(jnp.logical_and(bd, m), 1.0, 0.0).astype(BF16)

    return dict(
        strict=i < t, incl=i <= t, strict_incl=strict_incl,
        eye=jnp.where(i == t, 1.0, 0.0).astype(F32),
        blk8=lax.shift_right_logical(t, 3) == lax.shift_right_logical(i, 3),
        blk8_bd=bd_mask(blk2[0]), joins_bd=[bd_mask(m) for m in joins],
        tri=jnp.where(cc <= rr, 1.0, 0.0).astype(BF16),
        m0=m0, m1=m1, m0_2=m0_2, m1_2=m1_2,
        bd=bd, bd_bf=jnp.where(bd, 1.0, 0.0).astype(BF16))


def _wkv_kernel(r_ref, lw_ref, k_ref, v_ref, kk_ref, a_ref, c1_ref, og_ref, gn_ref,
                x_ref, wo_ref, ln_ref, o_ref, obf_ref, z_ref, *, n_chunks, n_pairs, alpha):
    @pl.when(pl.program_id(1) == 0)
    def _():
        z_ref[...] = jnp.zeros_like(z_ref)

    def chunks(ref):
        return [ref[0, ci * CHUNK:(ci + 1) * CHUNK, p * LANES:(p + 1) * LANES]
                for p in range(n_pairs) for ci in range(n_chunks)]

    ys, states = _wkv_block(chunks(r_ref), chunks(lw_ref), chunks(k_ref), chunks(v_ref),
                            chunks(kk_ref), chunks(a_ref), [z_ref[p] for p in range(n_pairs)],
                            _wkv_consts(), n_chunks)
    ones = _head_ones(LANES)
    inv_n = 1.0 / HEAD_SIZE
    y = jnp.concatenate(ys, axis=0)
    mu = _mm(y, ones) * inv_n
    yc = y - mu
    var = _mm(yc * yc, ones) * inv_n
    yn = yc * lax.rsqrt(var + GN_EPS)
    tb = n_chunks * CHUNK
    merged = []
    for p in range(n_pairs):
        z_ref[p] = states[p]
        ls = slice(p * LANES, (p + 1) * LANES)
        yp = yn[p * tb:(p + 1) * tb] * gn_ref[0:1, ls] + gn_ref[1:2, ls]
        merged.append((c1_ref[0, :, ls].astype(F32)
                       + og_ref[0, :, ls].astype(F32) * yp).astype(BF16))
    h = alpha * x_ref[0] + jnp.dot(jnp.concatenate(merged, axis=1), wo_ref[...],
                                   preferred_element_type=F32)
    x1 = _layer_norm(h, ln_ref[0:1, :], ln_ref[1:2, :])
    o_ref[0] = x1
    obf_ref[0] = x1.astype(BF16)


def _layer_norm(h, g, b):
    mu = jnp.mean(h, axis=-1, keepdims=True)
    hc = h - mu
    var = jnp.mean(hc * hc, axis=-1, keepdims=True)
    return hc * lax.rsqrt(var + LN_EPS) * g + b


def _wkv(r, lw, k, v, kk, a, c1, og, gn, x, wo, ln, *, alpha, tb):
    bsz, t, dm = r.shape
    n_pairs = dm // LANES
    spec = pl.BlockSpec((1, tb, dm), lambda b, i: (b, i, 0))
    row_spec = pl.BlockSpec((SUBLANES, dm), lambda b, i: (0, 0))
    return pl.pallas_call(
        functools.partial(_wkv_kernel, n_chunks=tb // CHUNK, n_pairs=n_pairs, alpha=alpha),
        grid=(bsz, t // tb),
        in_specs=[spec] * 8 + [row_spec, spec, pl.BlockSpec((dm, dm), lambda b, i: (0, 0)),
                               row_spec],
        out_specs=[spec, spec],
        out_shape=[jax.ShapeDtypeStruct((bsz, t, dm), F32),
                   jax.ShapeDtypeStruct((bsz, t, dm), BF16)],
        scratch_shapes=[pltpu.VMEM((n_pairs, LANES, LANES), F32)],
        compiler_params=pltpu.CompilerParams(
            dimension_semantics=("arbitrary", "arbitrary"),
            vmem_limit_bytes=VMEM_LIMIT_BYTES),
        name="wkv_out",
    )(r, lw, k, v, kk, a, c1, og, gn, x, wo, ln)


def _ffn_kernel(xb_ref, x_ref, wg_ref, wu_ref, wd_ref, ln_ref, o_ref, acc_ref, *, alpha):
    f = pl.program_id(1)

    @pl.when(jnp.logical_and(pl.program_id(0) == 0, f == 0))
    def _():
        acc_ref[...] = jnp.zeros_like(acc_ref)

    xb = xb_ref[...]
    gate = jnp.dot(xb, wg_ref[...], preferred_element_type=F32)
    up = jnp.dot(xb, wu_ref[...], preferred_element_type=F32)
    hid = (gate * jax.nn.sigmoid(gate) * up).astype(BF16)
    part = jnp.dot(hid, wd_ref[...], preferred_element_type=F32)
    acc_ref[...] += part

    @pl.when(f == pl.num_programs(1) - 1)
    def _():
        h = alpha * x_ref[...] + acc_ref[...]
        o_ref[...] = _layer_norm(h, ln_ref[0:1, :], ln_ref[1:2, :])
        acc_ref[...] = jnp.zeros_like(acc_ref)


def _ffn(x_bf, x2d, w_gu, w_down, ln, *, alpha, tm, tf):
    m, dm = x2d.shape
    dff = w_down.shape[0]
    nf = dff // tf
    return pl.pallas_call(
        functools.partial(_ffn_kernel, alpha=alpha),
        grid=(m // tm, nf),
        in_specs=[
            pl.BlockSpec((tm, dm), lambda i, f: (i, 0)),
            pl.BlockSpec((tm, dm), lambda i, f: (i, 0)),
            pl.BlockSpec((dm, tf), lambda i, f: (0, f)),
            pl.BlockSpec((dm, tf), lambda i, f: (0, nf + f)),
            pl.BlockSpec((tf, dm), lambda i, f: (f, 0)),
            pl.BlockSpec((SUBLANES, dm), lambda i, f: (0, 0)),
        ],
        out_specs=pl.BlockSpec((tm, dm), lambda i, f: (i, 0)),
        out_shape=jax.ShapeDtypeStruct((m, dm), F32),
        scratch_shapes=[pltpu.VMEM((tm, dm), F32)],
        compiler_params=pltpu.CompilerParams(
            dimension_semantics=("arbitrary", "arbitrary"), vmem_limit_bytes=VMEM_LIMIT_BYTES),
        name="ffn",
    )(x_bf, x2d, w_gu, w_gu, w_down, ln)


def _pad_rows(a, n):
    return jnp.pad(a, ((0, n - a.shape[0]), (0, 0)))


def _pick(n, candidates):
    for c in candidates:
        if n % c == 0:
            return c
    raise ValueError(f"no tile size for extent {n}")


def kernel(x, w_in, shift_mu, conv_w, w0, w_up, a0, a_up, g_up, k_k, k_a, r_k,
           gn_g, gn_b, w_o, ln1_g, ln1_b, w_gu, w_down, ln2_g, ln2_b):
    bsz, t, dm = x.shape
    depth = w_in.shape[0]
    lw_n, la_n, lg_n = w_up.shape[1], a_up.shape[1], g_up.shape[1]
    assert dm % (2 * LANES) == 0 and t % CHUNK == 0
    assert lw_n <= LANES and la_n <= LANES and lg_n <= 2 * LANES
    assert w_in.shape[2] == 8 * dm + lw_n + la_n + lg_n
    alpha = (2.0 * depth) ** 0.25
    tn = 2 * LANES
    tm_in = _pick(t, (512, 256, 128, 64))
    tb = _pick(t, (128, 64))
    m = bsz * t
    tm = _pick(m, (512, 256, 128, 64))
    tf = _pick(w_down.shape[1], (512, 256, 128))

    for l in range(depth):
        wi = w_in[l]
        c0 = 6 * dm
        wi_bf = wi.astype(BF16)
        g0 = c0 + lw_n + la_n + lg_n
        nj = dm // tn
        w_groups = [(wi_bf, 0), (wi_bf, nj), (wi_bf, 2 * nj), (wi_bf[:, g0:g0 + dm], 0),
                    (wi_bf, 3 * nj), (wi_bf, 4 * nj), (wi_bf, 5 * nj), (wi_bf[:, g0 + dm:], 0)]

        def lane_pad(a, n):
            return jnp.pad(a, ((0, 0), (0, n - a.shape[1])))

        wl = jnp.concatenate([
            lane_pad(wi[:, c0:c0 + lw_n], LANES),
            lane_pad(wi[:, c0 + lw_n:c0 + lw_n + la_n], LANES),
            lane_pad(wi[:, c0 + lw_n + la_n:c0 + lw_n + la_n + lg_n], 2 * LANES)],
            axis=1).astype(BF16)
        mu = shift_mu[l]
        s0 = 3 * dm
        mu_l = jnp.concatenate([
            jnp.pad(mu[s0:s0 + lw_n], (0, LANES - lw_n)),
            jnp.pad(mu[s0 + lw_n:s0 + lw_n + la_n], (0, LANES - la_n)),
            jnp.pad(mu[s0 + lw_n + la_n:], (0, 2 * LANES - lg_n))])[None, :]
        wup = jnp.concatenate([_pad_rows(w_up[l], LANES), _pad_rows(a_up[l], LANES),
                               _pad_rows(g_up[l], 2 * LANES)], axis=0).astype(BF16)
        prm = _pad_rows(jnp.stack([
            mu[0:dm], mu[dm:2 * dm], mu[2 * dm:3 * dm],
            conv_w[l, 0], conv_w[l, 1], conv_w[l, 2],
            w0[l], a0[l], k_k[l], k_a[l], r_k[l].reshape(-1)]), 16)

        r, lw, k, v, kk, a, c1, og = _mixer_in(
            x, w_groups, wl, wup, prm, mu_l, tm=tm_in, tn=tn)
        gn = _pad_rows(jnp.stack([gn_g[l], gn_b[l]]), SUBLANES)
        ln1 = _pad_rows(jnp.stack([ln1_g[l], ln1_b[l]]), SUBLANES)
        x1, x1_bf = _wkv(r, lw, k, v, kk, a, c1, og, gn, x, w_o[l].astype(BF16), ln1,
                         alpha=alpha, tb=tb)
        ln2 = _pad_rows(jnp.stack([ln2_g[l], ln2_b[l]]), SUBLANES)
        x = _ffn(x1_bf.reshape(m, dm), x1.reshape(m, dm), w_gu[l].astype(BF16),
                 w_down[l].astype(BF16), ln2, alpha=alpha, tm=tm, tf=tf).reshape(bsz, t, dm)
    return x
```

```python
import functools
import math

import jax
import jax.numpy as jnp
from jax import lax
from jax.experimental import pallas as pl
from jax.experimental.pallas import tpu as pltpu

HEAD_SIZE = 64
LOG2_HEAD = 6
LOG_DECAY_SCALE = -math.exp(-0.5)
LN_EPS = 1e-5
GN_EPS = 64e-5
LANES = 128
SUBLANES = 8
CHUNK = 64
VMEM_LIMIT_BYTES = 56 * 1024 * 1024

F32 = jnp.float32
BF16 = jnp.bfloat16


def _mm(a, b):
    return jnp.dot(a.astype(BF16), b.astype(BF16), preferred_element_type=F32)


def _mm_nt(a, b):
    return lax.dot_general(a.astype(BF16), b.astype(BF16), (((1,), (1,)), ((), ())),
                           preferred_element_type=F32)


def _split2(a):
    hi = a.astype(BF16)
    lo = (a - hi.astype(F32)).astype(BF16)
    return hi, lo


def _mm_exact_lhs(a_exact, b):
    d = functools.partial(jnp.dot, preferred_element_type=F32)
    h, l = _split2(b)
    return d(a_exact, h) + d(a_exact, l)


def _iota_shr(shape, dim, log2_div):
    return lax.shift_right_logical(lax.broadcasted_iota(jnp.int32, shape, dim), log2_div)


def _head_ones(n):
    r = _iota_shr((n, n), 0, LOG2_HEAD)
    c = _iota_shr((n, n), 1, LOG2_HEAD)
    return jnp.where(r == c, 1.0, 0.0).astype(BF16)


N_GROUPS = 8
N_CARRY = 4
LORA_PAD = 512


def _mixer_in_kernel(x_ref, *refs, tm, tn):
    w_refs = refs[:N_GROUPS]
    wl_ref, wup_ref, p_ref, mul_ref = refs[N_GROUPS:N_GROUPS + 4]
    r_ref, lw_ref, k_ref, v_ref, kk_ref, a_ref, c1_ref, og_ref = refs[N_GROUPS + 4:N_GROUPS + 12]
    xb_ref, lora_ref, carry_ref, carry_l_ref = refs[N_GROUPS + 12:]
    i = pl.program_id(1)
    j = pl.program_id(2)
    first = i == 0
    row = lax.broadcasted_iota(jnp.int32, (tm, tn), 0)

    @pl.when(j == 0)
    def _():
        xb_ref[...] = x_ref[0].astype(BF16)

    def xdot(w_ref):
        return jnp.dot(xb_ref[...], w_ref[...], preferred_element_type=F32)

    def prev1(p, c8):
        return jnp.where(row == 0, c8[SUBLANES - 1:SUBLANES, :], pltpu.roll(p, 1, 0))

    def prev2(p, c8):
        rolled = pltpu.roll(p, 2, 0)
        rolled = jnp.where(row == 1, c8[SUBLANES - 1:SUBLANES, :], rolled)
        return jnp.where(row == 0, c8[SUBLANES - 2:SUBLANES - 1, :], rolled)

    @pl.when(j == 0)
    def _():
        pl_ = xdot(wl_ref)
        cl = jnp.where(first, 0.0, carry_l_ref[...])
        rowl = lax.broadcasted_iota(jnp.int32, (tm, LORA_PAD), 0)
        prev = jnp.where(rowl == 0, cl[SUBLANES - 1:SUBLANES, :], pltpu.roll(pl_, 1, 0))
        carry_l_ref[...] = pl_[tm - SUBLANES:, :]
        z = pl_ + mul_ref[...] * (prev - pl_)
        lora_ref[:, 0:LANES] = jnp.tanh(z[:, 0:LANES]).astype(BF16)
        lora_ref[:, LANES:2 * LANES] = z[:, LANES:2 * LANES].astype(BF16)
        lora_ref[:, 2 * LANES:] = jax.nn.sigmoid(z[:, 2 * LANES:]).astype(BF16)

    prm = p_ref[...]
    mu_r, mu_k, mu_v = prm[0:1], prm[1:2], prm[2:3]
    cw0, cw1, cw2 = prm[3:4], prm[4:5], prm[5:6]
    w0, a0, k_k, k_a, r_k = prm[6:7], prm[7:8], prm[8:9], prm[9:10], prm[10:11]
    carry = [jnp.where(first, 0.0, carry_ref[j, q]) for q in range(N_CARRY)]
    w_cb, w_cc, w_ch, w_gc, w_r, w_k, w_v, w_gr = w_refs
    d = functools.partial(jnp.dot, preferred_element_type=F32)

    wup = wup_ref[...]
    w_pre = w0 + d(lora_ref[:, 0:LANES], wup[0:LANES])
    a_pre = a0 + d(lora_ref[:, LANES:2 * LANES], wup[LANES:2 * LANES])
    g = d(lora_ref[:, 2 * LANES:], wup[2 * LANES:])
    lw = LOG_DECAY_SCALE * jax.nn.sigmoid(w_pre)
    a = jax.nn.sigmoid(a_pre)

    p_r, p_k, p_v = xdot(w_r), xdot(w_k), xdot(w_v)
    r = p_r + mu_r * (prev1(p_r, carry[1]) - p_r)
    k = p_k + mu_k * (prev1(p_k, carry[2]) - p_k)
    v = p_v + mu_v * (prev1(p_v, carry[3]) - p_v)
    p_gr, p_cc, p_ch = xdot(w_gr), xdot(w_cc), xdot(w_ch)

    ones = _head_ones(tn)
    kraw = k * k_k
    ss = _mm(kraw * kraw, ones)
    kk = kraw / jnp.maximum(jnp.sqrt(ss), 1e-12)
    k_mod = k * (1.0 + (a - 1.0) * k_a)
    bonus = _mm(r * k_mod * r_k, ones) * v
    og = g * jax.nn.sigmoid(p_gr)

    u = p_cc * p_ch
    conv = cw2 * u + cw1 * prev1(u, carry[0]) + cw0 * prev2(u, carry[0])
    p_gc, p_cb = xdot(w_gc), xdot(w_cb)
    y_conv = (jax.nn.sigmoid(p_gc) * conv) * p_cb
    for q, val in enumerate((u, p_r, p_k, p_v)):
        carry_ref[j, q] = val[tm - SUBLANES:, :]

    r_ref[0] = r
    lw_ref[0] = lw
    k_ref[0] = k_mod
    v_ref[0] = v.astype(v_ref.dtype)
    kk_ref[0] = kk
    a_ref[0] = a
    c1_ref[0] = (y_conv + og * bonus).astype(c1_ref.dtype)
    og_ref[0] = og.astype(og_ref.dtype)


def _mixer_in(x, w_groups, wl, wup, prm, mu_l, *, tm, tn):
    bsz, t, dm = x.shape
    nj = dm // tn
    grid = (bsz, t // tm, nj)
    out_sds = jax.ShapeDtypeStruct((bsz, t, dm), F32)
    out_bf = jax.ShapeDtypeStruct((bsz, t, dm), BF16)
    out_spec = pl.BlockSpec((1, tm, tn), lambda b, i, j: (b, i, j))
    return pl.pallas_call(
        functools.partial(_mixer_in_kernel, tm=tm, tn=tn),
        grid=grid,
        in_specs=[
            pl.BlockSpec((1, tm, dm), lambda b, i, j: (b, i, 0)),
            *[pl.BlockSpec((dm, tn), lambda b, i, j, o=o: (0, o + j)) for _, o in w_groups],
            pl.BlockSpec((dm, LORA_PAD), lambda b, i, j: (0, 0)),
            pl.BlockSpec((LORA_PAD, tn), lambda b, i, j: (0, j)),
            pl.BlockSpec((16, tn), lambda b, i, j: (0, j)),
            pl.BlockSpec((1, LORA_PAD), lambda b, i, j: (0, 0)),
        ],
        out_specs=[out_spec] * 8,
        out_shape=[out_sds, out_sds, out_sds, out_bf, out_sds, out_sds, out_bf, out_bf],
        scratch_shapes=[
            pltpu.VMEM((tm, dm), BF16),
            pltpu.VMEM((tm, LORA_PAD), BF16),
            pltpu.VMEM((nj, N_CARRY, SUBLANES, tn), F32),
            pltpu.VMEM((SUBLANES, LORA_PAD), F32),
        ],
        compiler_params=pltpu.CompilerParams(
            dimension_semantics=("arbitrary", "arbitrary", "arbitrary"),
            vmem_limit_bytes=VMEM_LIMIT_BYTES),
        name="mixer_in",
    )(x, *[w for w, _ in w_groups], wl, wup, prm, mu_l)


def _wkv_block(r, lw, k, v, kk, a, z0, consts, n_chunks):
    cs = consts
    c = CHUNK

    def stack(x):
        xb = x.astype(BF16)
        m0, m1 = (cs["m0"], cs["m1"]) if x.shape[1] == LANES else (cs["m0_2"], cs["m1_2"])
        return jnp.concatenate([xb * m0, xb * m1], axis=0)

    def bdiag(xw):
        xb = xw.astype(BF16)
        return jnp.concatenate([xb, xb], axis=0) * cs["bd_bf"]

    def each(fn, *lists):
        return [fn(*args) for args in zip(*lists)]

    cum = each(lambda x: _mm_exact_lhs(cs["tri"], x), lw)
    beta = each(lambda x, y: x * y, kk, a)
    e_inv = each(lambda x: jnp.exp(-x), cum)
    e_rem = each(lambda x: jnp.exp(x[c - 1:c, :] - x), cum)
    rt = each(lambda x, y: x * jnp.exp(y), r, cum)
    at = each(lambda x, y, z_: -x * jnp.exp(y - z_), kk, cum, lw)
    kt = each(lambda x, y: x * y, k, e_inv)
    bt = each(lambda x, y: x * y, beta, e_inv)
    kh = each(lambda x, y: x * y, k, e_rem)
    bh = each(lambda x, y: x * y, beta, e_rem)

    s = each(lambda a_, r_, b_, k_: _mm_nt(jnp.concatenate([a_, r_], axis=0),
                                           jnp.concatenate([stack(b_), stack(k_)], axis=0)),
             at, rt, bt, kt)
    lab = each(lambda x: jnp.where(cs["strict"], x[0:c, 0:2 * c], 0.0), s)
    urb = each(lambda x: jnp.where(cs["incl"], x[c:, 0:2 * c], 0.0), s)

    def twice(x):
        xb = x.astype(BF16)
        return jnp.concatenate([xb, xb], axis=0)

    lab2 = each(twice, lab)
    l8 = each(lambda x: jnp.where(cs["blk8"], x, 0.0), lab)
    l8_2 = each(lambda x, x2: _mm(x, x2 * cs["blk8_bd"]), l8, lab2)
    l8_4 = each(lambda x: _mm(x, bdiag(x)), l8_2)
    tw = each(lambda x, y: _mm(cs["eye"] + x, bdiag(cs["eye"] + y)), l8, l8_2)
    tw = each(lambda x, y: _mm(x, bdiag(cs["eye"] + y)), tw, l8_4)
    for join_bd in cs["joins_bd"]:
        half = each(lambda t_, x2: _mm(t_, x2 * join_bd), tw, lab2)
        tw = each(lambda t_, h_: t_ + _mm(h_, bdiag(t_)), tw, half)

    nv_uv = each(lambda x, v_: _mm(jnp.where(cs["strict_incl"], x[:, 2 * c:], 0.0), stack(v_)),
                 s, v)
    ah_d = each(lambda t_, a_, n_: _mm(t_, stack(jnp.concatenate([a_, n_[0:c]], axis=1))),
                tw, at, nv_uv)
    u2 = each(lambda u_, x: _mm(u_, stack(x)), urb, ah_d)
    rh = each(lambda r_, u_: r_ + u_[:, 0:LANES], rt, u2)
    e = each(lambda u_, w_: u_[:, LANES:] + w_[c:], u2, nv_uv)
    zeros = jnp.zeros((c, LANES), BF16)
    gf = each(lambda b_, k_, x, v_: _mm(
        jnp.concatenate([b_, k_], axis=0).T,
        jnp.concatenate([x.astype(BF16), jnp.concatenate([zeros, v_.astype(BF16)], axis=1)],
                        axis=0)), bh, kh, ah_d, v)
    gl = each(lambda x: jnp.where(cs["bd"], x[:, 0:LANES], 0.0), gf)
    f = each(lambda x: jnp.where(cs["bd"], x[:, LANES:], 0.0), gf)
    decay_col = each(lambda x: jnp.exp(x.T[:, c - 1:c]), cum)

    ys = [None] * len(r)
    states = list(z0)
    for ci in range(n_chunks):
        for p in range(len(states)):
            n = p * n_chunks + ci
            z = states[p]
            rz_gz = _mm(jnp.concatenate([rh[n], gl[n]], axis=0), z)
            ys[n] = rz_gz[0:c] + e[n]
            states[p] = decay_col[n] * z + rz_gz[c:] + f[n]
    return ys, states


def _wkv_consts():
    c = CHUNK
    assert c == HEAD_SIZE and 2 * c == LANES
    t = lax.broadcasted_iota(jnp.int32, (c, 2 * c), 0)
    i = jnp.bitwise_and(lax.broadcasted_iota(jnp.int32, (c, 2 * c), 1), c - 1)
    rr = lax.broadcasted_iota(jnp.int32, (c, c), 0)
    cc = lax.broadcasted_iota(jnp.int32, (c, c), 1)
    def head_lane_masks(width):
        lane = jnp.bitwise_and(lax.broadcasted_iota(jnp.int32, (c, width), 1), LANES - 1)
        return (jnp.where(lane < HEAD_SIZE, 1.0, 0.0).astype(BF16),
                jnp.where(lane < HEAD_SIZE, 0.0, 1.0).astype(BF16))

    m0, m1 = head_lane_masks(LANES)
    m0_2, m1_2 = head_lane_masks(2 * LANES)
    bd = _iota_shr((2 * c, 2 * c), 0, LOG2_HEAD) == _iota_shr((2 * c, 2 * c), 1, LOG2_HEAD)
    t2 = lax.broadcasted_iota(jnp.int32, (2 * c, 2 * c), 0)
    i2 = jnp.bitwise_and(lax.broadcasted_iota(jnp.int32, (2 * c, 2 * c), 1), c - 1)
    strict_incl = jnp.logical_or(jnp.logical_and(t2 < c, i2 < t2),
                                 jnp.logical_and(t2 >= c, i2 <= t2 - c))
    tt = jnp.bitwise_and(t2, c - 1)
    blk2 = [lax.shift_right_logical(tt, s) == lax.shift_right_logical(i2, s) for s in (3, 4, 5)]
    out2 = [jnp.logical_not(b) for b in blk2]
    joins = [jnp.logical_and(blk2[1], out2[0]), jnp.logical_and(blk2[2], out2[1]), out2[2]]

    def bd_mask(m):
        return jnp.where(jnp.logical_and(bd, m), 1.0, 0.0).astype(BF16)

    return dict(
        strict=i < t, incl=i <= t, strict_incl=strict_incl,
        eye=jnp.where(i == t, 1.0, 0.0).astype(F32),
        blk8=lax.shift_right_logical(t, 3) == lax.shift_right_logical(i, 3),
        blk8_bd=bd_mask(blk2[0]), joins_bd=[bd_mask(m) for m in joins],
        tri=jnp.where(cc <= rr, 1.0, 0.0).astype(BF16),
        m0=m0, m1=m1, m0_2=m0_2, m1_2=m1_2,
        bd=bd, bd_bf=jnp.where(bd, 1.0, 0.0).astype(BF16))


def _wkv_kernel(r_ref, lw_ref, k_ref, v_ref, kk_ref, a_ref, c1_ref, og_ref, gn_ref,
                x_ref, wo_ref, ln_ref, o_ref, obf_ref, z_ref, *, n_chunks, n_pairs, alpha):
    @pl.when(pl.program_id(1) == 0)
    def _():
        z_ref[...] = jnp.zeros_like(z_ref)

    def chunks(ref):
        return [ref[0, ci * CHUNK:(ci + 1) * CHUNK, p * LANES:(p + 1) * LANES]
                for p in range(n_pairs) for ci in range(n_chunks)]

    ys, states = _wkv_block(chunks(r_ref), chunks(lw_ref), chunks(k_ref), chunks(v_ref),
                            chunks(kk_ref), chunks(a_ref), [z_ref[p] for p in range(n_pairs)],
                            _wkv_consts(), n_chunks)
    ones = _head_ones(LANES)
    inv_n = 1.0 / HEAD_SIZE
    y = jnp.concatenate(ys, axis=0)
    mu = _mm(y, ones) * inv_n
    yc = y - mu
    var = _mm(yc * yc, ones) * inv_n
    yn = yc * lax.rsqrt(var + GN_EPS)
    tb = n_chunks * CHUNK
    merged = []
    for p in range(n_pairs):
        z_ref[p] = states[p]
        ls = slice(p * LANES, (p + 1) * LANES)
        yp = yn[p * tb:(p + 1) * tb] * gn_ref[0:1, ls] + gn_ref[1:2, ls]
        merged.append((c1_ref[0, :, ls].astype(F32)
                       + og_ref[0, :, ls].astype(F32) * yp).astype(BF16))
    h = alpha * x_ref[0] + jnp.dot(jnp.concatenate(merged, axis=1), wo_ref[...],
                                   preferred_element_type=F32)
    x1 = _layer_norm(h, ln_ref[0:1, :], ln_ref[1:2, :])
    o_ref[0] = x1
    obf_ref[0] = x1.astype(BF16)


def _layer_norm(h, g, b):
    mu = jnp.mean(h, axis=-1, keepdims=True)
    hc = h - mu
    var = jnp.mean(hc * hc, axis=-1, keepdims=True)
    return hc * lax.rsqrt(var + LN_EPS) * g + b


def _wkv(r, lw, k, v, kk, a, c1, og, gn, x, wo, ln, *, alpha, tb):
    bsz, t, dm = r.shape
    n_pairs = dm // LANES
    spec = pl.BlockSpec((1, tb, dm), lambda b, i: (b, i, 0))
    row_spec = pl.BlockSpec((SUBLANES, dm), lambda b, i: (0, 0))
    return pl.pallas_call(
        functools.partial(_wkv_kernel, n_chunks=tb // CHUNK, n_pairs=n_pairs, alpha=alpha),
        grid=(bsz, t // tb),
        in_specs=[spec] * 8 + [row_spec, spec, pl.BlockSpec((dm, dm), lambda b, i: (0, 0)),
                               row_spec],
        out_specs=[spec, spec],
        out_shape=[jax.ShapeDtypeStruct((bsz, t, dm), F32),
                   jax.ShapeDtypeStruct((bsz, t, dm), BF16)],
        scratch_shapes=[pltpu.VMEM((n_pairs, LANES, LANES), F32)],
        compiler_params=pltpu.CompilerParams(
            dimension_semantics=("arbitrary", "arbitrary"),
            vmem_limit_bytes=VMEM_LIMIT_BYTES),
        name="wkv_out",
    )(r, lw, k, v, kk, a, c1, og, gn, x, wo, ln)


def _ffn_kernel(xb_ref, x_ref, wg_ref, wu_ref, wd_ref, ln_ref, o_ref, acc_ref, *, alpha):
    f = pl.program_id(1)

    @pl.when(jnp.logical_and(pl.program_id(0) == 0, f == 0))
    def _():
        acc_ref[...] = jnp.zeros_like(acc_ref)

    xb = xb_ref[...]
    gate = jnp.dot(xb, wg_ref[...], preferred_element_type=F32)
    up = jnp.dot(xb, wu_ref[...], preferred_element_type=F32)
    hid = (gate * jax.nn.sigmoid(gate) * up).astype(BF16)
    part = jnp.dot(hid, wd_ref[...], preferred_element_type=F32)
    acc_ref[...] += part

    @pl.when(f == pl.num_programs(1) - 1)
    def _():
        h = alpha * x_ref[...] + acc_ref[...]
        o_ref[...] = _layer_norm(h, ln_ref[0:1, :], ln_ref[1:2, :])
        acc_ref[...] = jnp.zeros_like(acc_ref)


def _ffn(x_bf, x2d, w_gu, w_down, ln, *, alpha, tm, tf):
    m, dm = x2d.shape
    dff = w_down.shape[0]
    nf = dff // tf
    return pl.pallas_call(
        functools.partial(_ffn_kernel, alpha=alpha),
        grid=(m // tm, nf),
        in_specs=[
            pl.BlockSpec((tm, dm), lambda i, f: (i, 0)),
            pl.BlockSpec((tm, dm), lambda i, f: (i, 0)),
            pl.BlockSpec((dm, tf), lambda i, f: (0, f)),
            pl.BlockSpec((dm, tf), lambda i, f: (0, nf + f)),
            pl.BlockSpec((tf, dm), lambda i, f: (f, 0)),
            pl.BlockSpec((SUBLANES, dm), lambda i, f: (0, 0)),
        ],
        out_specs=pl.BlockSpec((tm, dm), lambda i, f: (i, 0)),
        out_shape=jax.ShapeDtypeStruct((m, dm), F32),
        scratch_shapes=[pltpu.VMEM((tm, dm), F32)],
        compiler_params=pltpu.CompilerParams(
            dimension_semantics=("arbitrary", "arbitrary"), vmem_limit_bytes=VMEM_LIMIT_BYTES),
        name="ffn",
    )(x_bf, x2d, w_gu, w_gu, w_down, ln)


def _pad_rows(a, n):
    return jnp.pad(a, ((0, n - a.shape[0]), (0, 0)))


def _pick(n, candidates):
    for c in candidates:
        if n % c == 0:
            return c
    raise ValueError(f"no tile size for extent {n}")


def kernel(x, w_in, shift_mu, conv_w, w0, w_up, a0, a_up, g_up, k_k, k_a, r_k,
           gn_g, gn_b, w_o, ln1_g, ln1_b, w_gu, w_down, ln2_g, ln2_b):
    bsz, t, dm = x.shape
    depth = w_in.shape[0]
    lw_n, la_n, lg_n = w_up.shape[1], a_up.shape[1], g_up.shape[1]
    assert dm % (2 * LANES) == 0 and t % CHUNK == 0
    assert lw_n <= LANES and la_n <= LANES and lg_n <= 2 * LANES
    assert w_in.shape[2] == 8 * dm + lw_n + la_n + lg_n
    alpha = (2.0 * depth) ** 0.25
    tn = 2 * LANES
    tm_in = _pick(t, (512, 256, 128, 64))
    tb = _pick(t, (128, 64))
    m = bsz * t
    tm = _pick(m, (512, 256, 128, 64))
    tf = _pick(w_down.shape[1], (512, 256, 128))

    for l in range(depth):
        wi = w_in[l]
        c0 = 6 * dm
        wi_bf = wi.astype(BF16)
        g0 = c0 + lw_n + la_n + lg_n
        nj = dm // tn
        w_groups = [(wi_bf, 0), (wi_bf, nj), (wi_bf, 2 * nj), (wi_bf[:, g0:g0 + dm], 0),
                    (wi_bf, 3 * nj), (wi_bf, 4 * nj), (wi_bf, 5 * nj), (wi_bf[:, g0 + dm:], 0)]

        def lane_pad(a, n):
            return jnp.pad(a, ((0, 0), (0, n - a.shape[1])))

        wl = jnp.concatenate([
            lane_pad(wi[:, c0:c0 + lw_n], LANES),
            lane_pad(wi[:, c0 + lw_n:c0 + lw_n + la_n], LANES),
            lane_pad(wi[:, c0 + lw_n + la_n:c0 + lw_n + la_n + lg_n], 2 * LANES)],
            axis=1).astype(BF16)
        mu = shift_mu[l]
        s0 = 3 * dm
        mu_l = jnp.concatenate([
            jnp.pad(mu[s0:s0 + lw_n], (0, LANES - lw_n)),
            jnp.pad(mu[s0 + lw_n:s0 + lw_n + la_n], (0, LANES - la_n)),
            jnp.pad(mu[s0 + lw_n + la_n:], (0, 2 * LANES - lg_n))])[None, :]
        wup = jnp.concatenate([_pad_rows(w_up[l], LANES), _pad_rows(a_up[l], LANES),
                               _pad_rows(g_up[l], 2 * LANES)], axis=0).astype(BF16)
        prm = _pad_rows(jnp.stack([
            mu[0:dm], mu[dm:2 * dm], mu[2 * dm:3 * dm],
            conv_w[l, 0], conv_w[l, 1], conv_w[l, 2],
            w0[l], a0[l], k_k[l], k_a[l], r_k[l].reshape(-1)]), 16)

        r, lw, k, v, kk, a, c1, og = _mixer_in(
            x, w_groups, wl, wup, prm, mu_l, tm=tm_in, tn=tn)
        gn = _pad_rows(jnp.stack([gn_g[l], gn_b[l]]), SUBLANES)
        ln1 = _pad_rows(jnp.stack([ln1_g[l], ln1_b[l]]), SUBLANES)
        x1, x1_bf = _wkv(r, lw, k, v, kk, a, c1, og, gn, x, w_o[l].astype(BF16), ln1,
                         alpha=alpha, tb=tb)
        ln2 = _pad_rows(jnp.stack([ln2_g[l], ln2_b[l]]), SUBLANES)
        x = _ffn(x1_bf.reshape(m, dm), x1.reshape(m, dm), w_gu[l].astype(BF16),
                 w_down[l].astype(BF16), ln2, alpha=alpha, tm=tm, tf=tf).reshape(bsz, t, dm)
    return x
```

```python
import functools
import math

import jax
import jax.numpy as jnp
from jax import lax
from jax.experimental import pallas as pl
from jax.experimental.pallas import tpu as pltpu

HEAD_SIZE = 64
LOG2_HEAD = 6
LOG_DECAY_SCALE = -math.exp(-0.5)
LN_EPS = 1e-5
GN_EPS = 64e-5
LANES = 128
SUBLANES = 8
CHUNK = 64
VMEM_LIMIT_BYTES = 56 * 1024 * 1024

F32 = jnp.float32
BF16 = jnp.bfloat16


def _mm(a, b):
    return jnp.dot(a.astype(BF16), b.astype(BF16), preferred_element_type=F32)


def _mm_nt(a, b):
    return lax.dot_general(a.astype(BF16), b.astype(BF16), (((1,), (1,)), ((), ())),
                           preferred_element_type=F32)


def _split2(a):
    hi = a.astype(BF16)
    lo = (a - hi.astype(F32)).astype(BF16)
    return hi, lo


def _mm_exact_lhs(a_exact, b):
    d = functools.partial(jnp.dot, preferred_element_type=F32)
    h, l = _split2(b)
    return d(a_exact, h) + d(a_exact, l)


def _iota_shr(shape, dim, log2_div):
    return lax.shift_right_logical(lax.broadcasted_iota(jnp.int32, shape, dim), log2_div)


def _head_ones(n):
    r = _iota_shr((n, n), 0, LOG2_HEAD)
    c = _iota_shr((n, n), 1, LOG2_HEAD)
    return jnp.where(r == c, 1.0, 0.0).astype(BF16)


N_GROUPS = 8
N_CARRY = 4
LORA_PAD = 512


def _mixer_in_kernel(x_ref, *refs, tm, tn):
    w_refs = refs[:N_GROUPS]
    wl_ref, wup_ref, p_ref, mul_ref = refs[N_GROUPS:N_GROUPS + 4]
    r_ref, lw_ref, k_ref, v_ref, kk_ref, a_ref, c1_ref, og_ref = refs[N_GROUPS + 4:N_GROUPS + 12]
    xb_ref, lora_ref, carry_ref, carry_l_ref = refs[N_GROUPS + 12:]
    i = pl.program_id(1)
    j = pl.program_id(2)
    first = i == 0
    row = lax.broadcasted_iota(jnp.int32, (tm, tn), 0)

    @pl.when(j == 0)
    def _():
        xb_ref[...] = x_ref[0].astype(BF16)

    def xdot(w_ref):
        return jnp.dot(xb_ref[...], w_ref[...], preferred_element_type=F32)

    def prev1(p, c8):
        return jnp.where(row == 0, c8[SUBLANES - 1:SUBLANES, :], pltpu.roll(p, 1, 0))

    def prev2(p, c8):
        rolled = pltpu.roll(p, 2, 0)
        rolled = jnp.where(row == 1, c8[SUBLANES - 1:SUBLANES, :], rolled)
        return jnp.where(row == 0, c8[SUBLANES - 2:SUBLANES - 1, :], rolled)

    @pl.when(j == 0)
    def _():
        pl_ = xdot(wl_ref)
        cl = jnp.where(first, 0.0, carry_l_ref[...])
        rowl = lax.broadcasted_iota(jnp.int32, (tm, LORA_PAD), 0)
        prev = jnp.where(rowl == 0, cl[SUBLANES - 1:SUBLANES, :], pltpu.roll(pl_, 1, 0))
        carry_l_ref[...] = pl_[tm - SUBLANES:, :]
        z = pl_ + mul_ref[...] * (prev - pl_)
        lora_ref[:, 0:LANES] = jnp.tanh(z[:, 0:LANES]).astype(BF16)
        lora_ref[:, LANES:2 * LANES] = z[:, LANES:2 * LANES].astype(BF16)
        lora_ref[:, 2 * LANES:] = jax.nn.sigmoid(z[:, 2 * LANES:]).astype(BF16)

    prm = p_ref[...]
    mu_r, mu_k, mu_v = prm[0:1], prm[1:2], prm[2:3]
    cw0, cw1, cw2 = prm[3:4], prm[4:5], prm[5:6]
    w0, a0, k_k, k_a, r_k = prm[6:7], prm[7:8], prm[8:9], prm[9:10], prm[10:11]
    carry = [jnp.where(first, 0.0, carry_ref[j, q]) for q in range(N_CARRY)]
    w_cb, w_cc, w_ch, w_gc, w_r, w_k, w_v, w_gr = w_refs
    d = functools.partial(jnp.dot, preferred_element_type=F32)

    wup = wup_ref[...]
    w_pre = w0 + d(lora_ref[:, 0:LANES], wup[0:LANES])
    a_pre = a0 + d(lora_ref[:, LANES:2 * LANES], wup[LANES:2 * LANES])
    g = d(lora_ref[:, 2 * LANES:], wup[2 * LANES:])
    lw = LOG_DECAY_SCALE * jax.nn.sigmoid(w_pre)
    a = jax.nn.sigmoid(a_pre)

    p_r, p_k, p_v = xdot(w_r), xdot(w_k), xdot(w_v)
    r = p_r + mu_r * (prev1(p_r, carry[1]) - p_r)
    k = p_k + mu_k * (prev1(p_k, carry[2]) - p_k)
    v = p_v + mu_v * (prev1(p_v, carry[3]) - p_v)
    p_gr, p_cc, p_ch = xdot(w_gr), xdot(w_cc), xdot(w_ch)

    ones = _head_ones(tn)
    kraw = k * k_k
    ss = _mm(kraw * kraw, ones)
    kk = kraw / jnp.maximum(jnp.sqrt(ss), 1e-12)
    k_mod = k * (1.0 + (a - 1.0) * k_a)
    bonus = _mm(r * k_mod * r_k, ones) * v
    og = g * jax.nn.sigmoid(p_gr)

    u = p_cc * p_ch
    conv = cw2 * u + cw1 * prev1(u, carry[0]) + cw0 * prev2(u, carry[0])
    p_gc, p_cb = xdot(w_gc), xdot(w_cb)
    y_conv = (jax.nn.sigmoid(p_gc) * conv) * p_cb
    for q, val in enumerate((u, p_r, p_k, p_v)):
        carry_ref[j, q] = val[tm - SUBLANES:, :]

    r_ref[0] = r.astype(r_ref.dtype)
    lw_ref[0] = lw
    k_ref[0] = k_mod.astype(k_ref.dtype)
    v_ref[0] = v.astype(v_ref.dtype)
    kk_ref[0] = kk.astype(kk_ref.dtype)
    a_ref[0] = a.astype(a_ref.dtype)
    c1_ref[0] = (y_conv + og * bonus).astype(c1_ref.dtype)
    og_ref[0] = og.astype(og_ref.dtype)


def _mixer_in(x, w_groups, wl, wup, prm, mu_l, *, tm, tn):
    bsz, t, dm = x.shape
    nj = dm // tn
    grid = (bsz, t // tm, nj)
    out_sds = jax.ShapeDtypeStruct((bsz, t, dm), F32)
    out_bf = jax.ShapeDtypeStruct((bsz, t, dm), BF16)
    out_spec = pl.BlockSpec((1, tm, tn), lambda b, i, j: (b, i, j))
    return pl.pallas_call(
        functools.partial(_mixer_in_kernel, tm=tm, tn=tn),
        grid=grid,
        in_specs=[
            pl.BlockSpec((1, tm, dm), lambda b, i, j: (b, i, 0)),
            *[pl.BlockSpec((dm, tn), lambda b, i, j, o=o: (0, o + j)) for _, o in w_groups],
            pl.BlockSpec((dm, LORA_PAD), lambda b, i, j: (0, 0)),
            pl.BlockSpec((LORA_PAD, tn), lambda b, i, j: (0, j)),
            pl.BlockSpec((16, tn), lambda b, i, j: (0, j)),
            pl.BlockSpec((1, LORA_PAD), lambda b, i, j: (0, 0)),
        ],
        out_specs=[out_spec] * 8,
        out_shape=[out_bf, out_sds, out_bf, out_bf, out_bf, out_bf, out_bf, out_bf],
        scratch_shapes=[
            pltpu.VMEM((tm, dm), BF16),
            pltpu.VMEM((tm, LORA_PAD), BF16),
            pltpu.VMEM((nj, N_CARRY, SUBLANES, tn), F32),
            pltpu.VMEM((SUBLANES, LORA_PAD), F32),
        ],
        compiler_params=pltpu.CompilerParams(
            dimension_semantics=("arbitrary", "arbitrary", "arbitrary"),
            vmem_limit_bytes=VMEM_LIMIT_BYTES),
        name="mixer_in",
    )(x, *[w for w, _ in w_groups], wl, wup, prm, mu_l)


def _wkv_block(r, lw, k, v, kk, a, z0, consts, n_chunks):
    cs = consts
    c = CHUNK

    def stack(x):
        xb = x.astype(BF16)
        m0, m1 = (cs["m0"], cs["m1"]) if x.shape[1] == LANES else (cs["m0_2"], cs["m1_2"])
        return jnp.concatenate([xb * m0, xb * m1], axis=0)

    def bdiag(xw):
        xb = xw.astype(BF16)
        return jnp.concatenate([xb, xb], axis=0) * cs["bd_bf"]

    def each(fn, *lists):
        return [fn(*args) for args in zip(*lists)]

    cum = each(lambda x: _mm_exact_lhs(cs["tri"], x), lw)
    beta = each(lambda x, y: x * y, kk, a)
    e_inv = each(lambda x: jnp.exp(-x), cum)
    e_rem = each(lambda x: jnp.exp(x[c - 1:c, :] - x), cum)
    rt = each(lambda x, y: x * jnp.exp(y), r, cum)
    at = each(lambda x, y, z_: -x * jnp.exp(y - z_), kk, cum, lw)
    kt = each(lambda x, y: x * y, k, e_inv)
    bt = each(lambda x, y: x * y, beta, e_inv)
    kh = each(lambda x, y: x * y, k, e_rem)
    bh = each(lambda x, y: x * y, beta, e_rem)

    s = each(lambda a_, r_, b_, k_: _mm_nt(jnp.concatenate([a_, r_], axis=0),
                                           jnp.concatenate([stack(b_), stack(k_)], axis=0)),
             at, rt, bt, kt)
    lab = each(lambda x: jnp.where(cs["strict"], x[0:c, 0:2 * c], 0.0), s)
    urb = each(lambda x: jnp.where(cs["incl"], x[c:, 0:2 * c], 0.0), s)

    def twice(x):
        xb = x.astype(BF16)
        return jnp.concatenate([xb, xb], axis=0)

    lab2 = each(twice, lab)
    l8 = each(lambda x: jnp.where(cs["blk8"], x, 0.0), lab)
    l8_2 = each(lambda x, x2: _mm(x, x2 * cs["blk8_bd"]), l8, lab2)
    l8_4 = each(lambda x: _mm(x, bdiag(x)), l8_2)
    tw = each(lambda x, y: _mm(cs["eye"] + x, bdiag(cs["eye"] + y)), l8, l8_2)
    tw = each(lambda x, y: _mm(x, bdiag(cs["eye"] + y)), tw, l8_4)
    for join_bd in cs["joins_bd"]:
        half = each(lambda t_, x2: _mm(t_, x2 * join_bd), tw, lab2)
        tw = each(lambda t_, h_: t_ + _mm(h_, bdiag(t_)), tw, half)

    nv_uv = each(lambda x, v_: _mm(jnp.where(cs["strict_incl"], x[:, 2 * c:], 0.0), stack(v_)),
                 s, v)
    ah_d = each(lambda t_, a_, n_: _mm(t_, stack(jnp.concatenate([a_, n_[0:c]], axis=1))),
                tw, at, nv_uv)
    u2 = each(lambda u_, x: _mm(u_, stack(x)), urb, ah_d)
    rh = each(lambda r_, u_: r_ + u_[:, 0:LANES], rt, u2)
    e = each(lambda u_, w_: u_[:, LANES:] + w_[c:], u2, nv_uv)
    zeros = jnp.zeros((c, LANES), BF16)
    gf = each(lambda b_, k_, x, v_: _mm(
        jnp.concatenate([b_, k_], axis=0).T,
        jnp.concatenate([x.astype(BF16), jnp.concatenate([zeros, v_.astype(BF16)], axis=1)],
                        axis=0)), bh, kh, ah_d, v)
    gl = each(lambda x: jnp.where(cs["bd"], x[:, 0:LANES], 0.0), gf)
    f = each(lambda x: jnp.where(cs["bd"], x[:, LANES:], 0.0), gf)
    decay_col = each(lambda x: jnp.exp(x.T[:, c - 1:c]), cum)

    ys = [None] * len(r)
    states = list(z0)
    for ci in range(n_chunks):
        for p in range(len(states)):
            n = p * n_chunks + ci
            z = states[p]
            rz_gz = _mm(jnp.concatenate([rh[n], gl[n]], axis=0), z)
            ys[n] = rz_gz[0:c] + e[n]
            states[p] = decay_col[n] * z + rz_gz[c:] + f[n]
    return ys, states


def _wkv_consts():
    c = CHUNK
    assert c == HEAD_SIZE and 2 * c == LANES
    t = lax.broadcasted_iota(jnp.int32, (c, 2 * c), 0)
    i = jnp.bitwise_and(lax.broadcasted_iota(jnp.int32, (c, 2 * c), 1), c - 1)
    rr = lax.broadcasted_iota(jnp.int32, (c, c), 0)
    cc = lax.broadcasted_iota(jnp.int32, (c, c), 1)
    def head_lane_masks(width):
        lane = jnp.bitwise_and(lax.broadcasted_iota(jnp.int32, (c, width), 1), LANES - 1)
        return (jnp.where(lane < HEAD_SIZE, 1.0, 0.0).astype(BF16),
                jnp.where(lane < HEAD_SIZE, 0.0, 1.0).astype(BF16))

    m0, m1 = head_lane_masks(LANES)
    m0_2, m1_2 = head_lane_masks(2 * LANES)
    bd = _iota_shr((2 * c, 2 * c), 0, LOG2_HEAD) == _iota_shr((2 * c, 2 * c), 1, LOG2_HEAD)
    t2 = lax.broadcasted_iota(jnp.int32, (2 * c, 2 * c), 0)
    i2 = jnp.bitwise_and(lax.broadcasted_iota(jnp.int32, (2 * c, 2 * c), 1), c - 1)
    strict_incl = jnp.logical_or(jnp.logical_and(t2 < c, i2 < t2),
                                 jnp.logical_and(t2 >= c, i2 <= t2 - c))
    tt = jnp.bitwise_and(t2, c - 1)
    blk2 = [lax.shift_right_logical(tt, s) == lax.shift_right_logical(i2, s) for s in (3, 4, 5)]
    out2 = [jnp.logical_not(b) for b in blk2]
    joins = [jnp.logical_and(blk2[1], out2[0]), jnp.logical_and(blk2[2], out2[1]), out2[2]]

    def bd_mask(m):
        return jnp.where(jnp.logical_and(bd, m), 1.0, 0.0).astype(BF16)

    return dict(
        strict=i < t, incl=i <= t, strict_incl=strict_incl,
        eye=jnp.where(i == t, 1.0, 0.0).astype(F32),
        blk8=lax.shift_right_logical(t, 3) == lax.shift_right_logical(i, 3),
        blk8_bd=bd_mask(blk2[0]), joins_bd=[bd_mask(m) for m in joins],
        tri=jnp.where(cc <= rr, 1.0, 0.0).astype(BF16),
        m0=m0, m1=m1, m0_2=m0_2, m1_2=m1_2,
        bd=bd, bd_bf=jnp.where(bd, 1.0, 0.0).astype(BF16))


def _wkv_kernel(r_ref, lw_ref, k_ref, v_ref, kk_ref, a_ref, c1_ref, og_ref, gn_ref,
                x_ref, wo_ref, ln_ref, o_ref, obf_ref, z_ref, *, n_chunks, n_pairs, alpha):
    @pl.when(pl.program_id(1) == 0)
    def _():
        z_ref[...] = jnp.zeros_like(z_ref)

    def chunks(ref, dtype=F32):
        return [ref[0, ci * CHUNK:(ci + 1) * CHUNK, p * LANES:(p + 1) * LANES].astype(dtype)
                for p in range(n_pairs) for ci in range(n_chunks)]

    ys, states = _wkv_block(chunks(r_ref), chunks(lw_ref), chunks(k_ref), chunks(v_ref, BF16),
                            chunks(kk_ref), chunks(a_ref), [z_ref[p] for p in range(n_pairs)],
                            _wkv_consts(), n_chunks)
    ones = _head_ones(LANES)
    inv_n = 1.0 / HEAD_SIZE
    y = jnp.concatenate(ys, axis=0)
    mu = _mm(y, ones) * inv_n
    yc = y - mu
    var = _mm(yc * yc, ones) * inv_n
    yn = yc * lax.rsqrt(var + GN_EPS)
    tb = n_chunks * CHUNK
    merged = []
    for p in range(n_pairs):
        z_ref[p] = states[p]
        ls = slice(p * LANES, (p + 1) * LANES)
        yp = yn[p * tb:(p + 1) * tb] * gn_ref[0:1, ls] + gn_ref[1:2, ls]
        merged.append((c1_ref[0, :, ls].astype(F32)
                       + og_ref[0, :, ls].astype(F32) * yp).astype(BF16))
    h = alpha * x_ref[0] + jnp.dot(jnp.concatenate(merged, axis=1), wo_ref[...],
                                   preferred_element_type=F32)
    x1 = _layer_norm(h, ln_ref[0:1, :], ln_ref[1:2, :])
    o_ref[0] = x1
    obf_ref[0] = x1.astype(BF16)


def _layer_norm(h, g, b):
    mu = jnp.mean(h, axis=-1, keepdims=True)
    hc = h - mu
    var = jnp.mean(hc * hc, axis=-1, keepdims=True)
    return hc * lax.rsqrt(var + LN_EPS) * g + b


def _wkv(r, lw, k, v, kk, a, c1, og, gn, x, wo, ln, *, alpha, tb):
    bsz, t, dm = r.shape
    n_pairs = dm // LANES
    spec = pl.BlockSpec((1, tb, dm), lambda b, i: (b, i, 0))
    row_spec = pl.BlockSpec((SUBLANES, dm), lambda b, i: (0, 0))
    return pl.pallas_call(
        functools.partial(_wkv_kernel, n_chunks=tb // CHUNK, n_pairs=n_pairs, alpha=alpha),
        grid=(bsz, t // tb),
        in_specs=[spec] * 8 + [row_spec, spec, pl.BlockSpec((dm, dm), lambda b, i: (0, 0)),
                               row_spec],
        out_specs=[spec, spec],
        out_shape=[jax.ShapeDtypeStruct((bsz, t, dm), F32),
                   jax.ShapeDtypeStruct((bsz, t, dm), BF16)],
        scratch_shapes=[pltpu.VMEM((n_pairs, LANES, LANES), F32)],
        compiler_params=pltpu.CompilerParams(
            dimension_semantics=("arbitrary", "arbitrary"),
            vmem_limit_bytes=VMEM_LIMIT_BYTES),
        name="wkv_out",
    )(r, lw, k, v, kk, a, c1, og, gn, x, wo, ln)


def _ffn_kernel(xb_ref, x_ref, wg_ref, wu_ref, wd_ref, ln_ref, o_ref, acc_ref, *, alpha):
    f = pl.program_id(1)

    @pl.when(jnp.logical_and(pl.program_id(0) == 0, f == 0))
    def _():
        acc_ref[...] = jnp.zeros_like(acc_ref)

    xb = xb_ref[...]
    gate = jnp.dot(xb, wg_ref[...], preferred_element_type=F32)
    up = jnp.dot(xb, wu_ref[...], preferred_element_type=F32)
    hid = (gate * jax.nn.sigmoid(gate) * up).astype(BF16)
    part = jnp.dot(hid, wd_ref[...], preferred_element_type=F32)
    acc_ref[...] += part

    @pl.when(f == pl.num_programs(1) - 1)
    def _():
        h = alpha * x_ref[...] + acc_ref[...]
        o_ref[...] = _layer_norm(h, ln_ref[0:1, :], ln_ref[1:2, :])
        acc_ref[...] = jnp.zeros_like(acc_ref)


def _ffn(x_bf, x2d, w_gu, w_down, ln, *, alpha, tm, tf):
    m, dm = x2d.shape
    dff = w_down.shape[0]
    nf = dff // tf
    return pl.pallas_call(
        functools.partial(_ffn_kernel, alpha=alpha),
        grid=(m // tm, nf),
        in_specs=[
            pl.BlockSpec((tm, dm), lambda i, f: (i, 0)),
            pl.BlockSpec((tm, dm), lambda i, f: (i, 0)),
            pl.BlockSpec((dm, tf), lambda i, f: (0, f)),
            pl.BlockSpec((dm, tf), lambda i, f: (0, nf + f)),
            pl.BlockSpec((tf, dm), lambda i, f: (f, 0)),
            pl.BlockSpec((SUBLANES, dm), lambda i, f: (0, 0)),
        ],
        out_specs=pl.BlockSpec((tm, dm), lambda i, f: (i, 0)),
        out_shape=jax.ShapeDtypeStruct((m, dm), F32),
        scratch_shapes=[pltpu.VMEM((tm, dm), F32)],
        compiler_params=pltpu.CompilerParams(
            dimension_semantics=("arbitrary", "arbitrary"), vmem_limit_bytes=VMEM_LIMIT_BYTES),
        name="ffn",
    )(x_bf, x2d, w_gu, w_gu, w_down, ln)


def _pad_rows(a, n):
    return jnp.pad(a, ((0, n - a.shape[0]), (0, 0)))


def _pick(n, candidates):
    for c in candidates:
        if n % c == 0:
            return c
    raise ValueError(f"no tile size for extent {n}")


def kernel(x, w_in, shift_mu, conv_w, w0, w_up, a0, a_up, g_up, k_k, k_a, r_k,
           gn_g, gn_b, w_o, ln1_g, ln1_b, w_gu, w_down, ln2_g, ln2_b):
    bsz, t, dm = x.shape
    depth = w_in.shape[0]
    lw_n, la_n, lg_n = w_up.shape[1], a_up.shape[1], g_up.shape[1]
    assert dm % (2 * LANES) == 0 and t % CHUNK == 0
    assert lw_n <= LANES and la_n <= LANES and lg_n <= 2 * LANES
    assert w_in.shape[2] == 8 * dm + lw_n + la_n + lg_n
    alpha = (2.0 * depth) ** 0.25
    tn = 2 * LANES
    tm_in = _pick(t, (512, 256, 128, 64))
    tb = _pick(t, (128, 64))
    m = bsz * t
    tm = _pick(m, (512, 256, 128, 64))
    tf = _pick(w_down.shape[1], (512, 256, 128))

    for l in range(depth):
        wi = w_in[l]
        c0 = 6 * dm
        wi_bf = wi.astype(BF16)
        g0 = c0 + lw_n + la_n + lg_n
        nj = dm // tn
        w_groups = [(wi_bf, 0), (wi_bf, nj), (wi_bf, 2 * nj), (wi_bf[:, g0:g0 + dm], 0),
                    (wi_bf, 3 * nj), (wi_bf, 4 * nj), (wi_bf, 5 * nj), (wi_bf[:, g0 + dm:], 0)]

        def lane_pad(a, n):
            return jnp.pad(a, ((0, 0), (0, n - a.shape[1])))

        wl = jnp.concatenate([
            lane_pad(wi[:, c0:c0 + lw_n], LANES),
            lane_pad(wi[:, c0 + lw_n:c0 + lw_n + la_n], LANES),
            lane_pad(wi[:, c0 + lw_n + la_n:c0 + lw_n + la_n + lg_n], 2 * LANES)],
            axis=1).astype(BF16)
        mu = shift_mu[l]
        s0 = 3 * dm
        mu_l = jnp.concatenate([
            jnp.pad(mu[s0:s0 + lw_n], (0, LANES - lw_n)),
            jnp.pad(mu[s0 + lw_n:s0 + lw_n + la_n], (0, LANES - la_n)),
            jnp.pad(mu[s0 + lw_n + la_n:], (0, 2 * LANES - lg_n))])[None, :]
        wup = jnp.concatenate([_pad_rows(w_up[l], LANES), _pad_rows(a_up[l], LANES),
                               _pad_rows(g_up[l], 2 * LANES)], axis=0).astype(BF16)
        prm = _pad_rows(jnp.stack([
            mu[0:dm], mu[dm:2 * dm], mu[2 * dm:3 * dm],
            conv_w[l, 0], conv_w[l, 1], conv_w[l, 2],
            w0[l], a0[l], k_k[l], k_a[l], r_k[l].reshape(-1)]), 16)

        r, lw, k, v, kk, a, c1, og = _mixer_in(
            x, w_groups, wl, wup, prm, mu_l, tm=tm_in, tn=tn)
        gn = _pad_rows(jnp.stack([gn_g[l], gn_b[l]]), SUBLANES)
        ln1 = _pad_rows(jnp.stack([ln1_g[l], ln1_b[l]]), SUBLANES)
        x1, x1_bf = _wkv(r, lw, k, v, kk, a, c1, og, gn, x, w_o[l].astype(BF16), ln1,
                         alpha=alpha, tb=tb)
        ln2 = _pad_rows(jnp.stack([ln2_g[l], ln2_b[l]]), SUBLANES)
        x = _ffn(x1_bf.reshape(m, dm), x1.reshape(m, dm), w_gu[l].astype(BF16),
                 w_down[l].astype(BF16), ln2, alpha=alpha, tm=tm, tf=tf).reshape(bsz, t, dm)
    return x
```

```python
import functools
import math

import jax
import jax.numpy as jnp
from jax import lax
from jax.experimental import pallas as pl
from jax.experimental.pallas import tpu as pltpu

HEAD_SIZE = 64
LOG2_HEAD = 6
LOG_DECAY_SCALE = -math.exp(-0.5)
LN_EPS = 1e-5
GN_EPS = 64e-5
LANES = 128
SUBLANES = 8
BF16_ROWS = 16
CHUNK = 64
VMEM_LIMIT_BYTES = 56 * 1024 * 1024

F32 = jnp.float32
BF16 = jnp.bfloat16


def _mm(a, b):
    return jnp.dot(a.astype(BF16), b.astype(BF16), preferred_element_type=F32)


def _mm_nt(a, b):
    return lax.dot_general(a.astype(BF16), b.astype(BF16), (((1,), (1,)), ((), ())),
                           preferred_element_type=F32)


def _split2(a):
    hi = a.astype(BF16)
    lo = (a - hi.astype(F32)).astype(BF16)
    return hi, lo


def _mm_exact_lhs(a_exact, b):
    d = functools.partial(jnp.dot, preferred_element_type=F32)
    h, l = _split2(b)
    return d(a_exact, h) + d(a_exact, l)


def _iota_shr(shape, dim, log2_div):
    return lax.shift_right_logical(lax.broadcasted_iota(jnp.int32, shape, dim), log2_div)


def _head_ones(n):
    r = _iota_shr((n, n), 0, LOG2_HEAD)
    c = _iota_shr((n, n), 1, LOG2_HEAD)
    return jnp.where(r == c, 1.0, 0.0).astype(BF16)


N_GROUPS = 8
N_CARRY = 4
LORA_PAD = 512


def _mixer_in_kernel(x_ref, *refs, tm, tn):
    w_refs = refs[:N_GROUPS]
    wl_ref, wup_ref, p_ref, mul_ref = refs[N_GROUPS:N_GROUPS + 4]
    r_ref, lw_ref, k_ref, v_ref, kk_ref, a_ref, c1_ref, og_ref = refs[N_GROUPS + 4:N_GROUPS + 12]
    xb_ref, lora_ref, carry_ref, carry_l_ref = refs[N_GROUPS + 12:]
    i = pl.program_id(1)
    j = pl.program_id(2)
    first = i == 0
    row = lax.broadcasted_iota(jnp.int32, (tm, tn), 0)

    @pl.when(j == 0)
    def _():
        xb_ref[...] = x_ref[0].astype(BF16)

    def xdot(w_ref):
        return jnp.dot(xb_ref[...], w_ref[...], preferred_element_type=F32)

    def prev1(p, c8):
        return jnp.where(row == 0, c8[SUBLANES - 1:SUBLANES, :], pltpu.roll(p, 1, 0))

    def prev2(p, c8):
        rolled = pltpu.roll(p, 2, 0)
        rolled = jnp.where(row == 1, c8[SUBLANES - 1:SUBLANES, :], rolled)
        return jnp.where(row == 0, c8[SUBLANES - 2:SUBLANES - 1, :], rolled)

    @pl.when(j == 0)
    def _():
        pl_ = xdot(wl_ref)
        cl = jnp.where(first, 0.0, carry_l_ref[...])
        rowl = lax.broadcasted_iota(jnp.int32, (tm, LORA_PAD), 0)
        prev = jnp.where(rowl == 0, cl[SUBLANES - 1:SUBLANES, :], pltpu.roll(pl_, 1, 0))
        carry_l_ref[...] = pl_[tm - SUBLANES:, :]
        z = pl_ + mul_ref[...] * (prev - pl_)
        lora_ref[:, 0:LANES] = jnp.tanh(z[:, 0:LANES]).astype(BF16)
        lora_ref[:, LANES:2 * LANES] = z[:, LANES:2 * LANES].astype(BF16)
        lora_ref[:, 2 * LANES:] = jax.nn.sigmoid(z[:, 2 * LANES:]).astype(BF16)

    prm = p_ref[...]
    mu_r, mu_k, mu_v = prm[0:1], prm[1:2], prm[2:3]
    cw0, cw1, cw2 = prm[3:4], prm[4:5], prm[5:6]
    w0, a0, k_k, k_a, r_k = prm[6:7], prm[7:8], prm[8:9], prm[9:10], prm[10:11]
    carry = [jnp.where(first, 0.0, carry_ref[j, q]) for q in range(N_CARRY)]
    w_cb, w_cc, w_ch, w_gc, w_r, w_k, w_v, w_gr = w_refs
    d = functools.partial(jnp.dot, preferred_element_type=F32)

    wup = wup_ref[...]
    w_pre = w0 + d(lora_ref[:, 0:LANES], wup[0:LANES])
    a_pre = a0 + d(lora_ref[:, LANES:2 * LANES], wup[LANES:2 * LANES])
    g = d(lora_ref[:, 2 * LANES:], wup[2 * LANES:])
    lw = LOG_DECAY_SCALE * jax.nn.sigmoid(w_pre)
    a = jax.nn.sigmoid(a_pre)

    p_r, p_k, p_v = xdot(w_r), xdot(w_k), xdot(w_v)
    r = p_r + mu_r * (prev1(p_r, carry[1]) - p_r)
    k = p_k + mu_k * (prev1(p_k, carry[2]) - p_k)
    v = p_v + mu_v * (prev1(p_v, carry[3]) - p_v)
    p_gr, p_cc, p_ch = xdot(w_gr), xdot(w_cc), xdot(w_ch)

    ones = _head_ones(tn)
    kraw = k * k_k
    ss = _mm(kraw * kraw, ones)
    kk = kraw / jnp.maximum(jnp.sqrt(ss), 1e-12)
    k_mod = k * (1.0 + (a - 1.0) * k_a)
    bonus = _mm(r * k_mod * r_k, ones) * v
    og = g * jax.nn.sigmoid(p_gr)

    u = p_cc * p_ch
    conv = cw2 * u + cw1 * prev1(u, carry[0]) + cw0 * prev2(u, carry[0])
    p_gc, p_cb = xdot(w_gc), xdot(w_cb)
    y_conv = (jax.nn.sigmoid(p_gc) * conv) * p_cb
    for q, val in enumerate((u, p_r, p_k, p_v)):
        carry_ref[j, q] = val[tm - SUBLANES:, :]

    r_ref[0] = r.astype(r_ref.dtype)
    lw_ref[0] = lw
    k_ref[0] = k_mod.astype(k_ref.dtype)
    v_ref[0] = v.astype(v_ref.dtype)
    kk_ref[0] = kk.astype(kk_ref.dtype)
    a_ref[0] = a.astype(a_ref.dtype)
    c1_ref[0] = (y_conv + og * bonus).astype(c1_ref.dtype)
    og_ref[0] = og.astype(og_ref.dtype)


def _mixer_in(x, w_groups, wl, wup, prm, mu_l, *, tm, tn):
    bsz, t, dm = x.shape
    nj = dm // tn
    grid = (bsz, t // tm, nj)
    out_sds = jax.ShapeDtypeStruct((bsz, t, dm), F32)
    out_bf = jax.ShapeDtypeStruct((bsz, t, dm), BF16)
    out_spec = pl.BlockSpec((1, tm, tn), lambda b, i, j: (b, i, j))
    return pl.pallas_call(
        functools.partial(_mixer_in_kernel, tm=tm, tn=tn),
        grid=grid,
        in_specs=[
            pl.BlockSpec((1, tm, dm), lambda b, i, j: (b, i, 0)),
            *[pl.BlockSpec((dm, tn), lambda b, i, j, o=o: (0, o + j)) for _, o in w_groups],
            pl.BlockSpec((dm, LORA_PAD), lambda b, i, j: (0, 0)),
            pl.BlockSpec((LORA_PAD, tn), lambda b, i, j: (0, j)),
            pl.BlockSpec((16, tn), lambda b, i, j: (0, j)),
            pl.BlockSpec((1, LORA_PAD), lambda b, i, j: (0, 0)),
        ],
        out_specs=[out_spec] * 8,
        out_shape=[out_bf, out_sds, out_bf, out_bf, out_bf, out_bf, out_bf, out_bf],
        scratch_shapes=[
            pltpu.VMEM((tm, dm), BF16),
            pltpu.VMEM((tm, LORA_PAD), BF16),
            pltpu.VMEM((nj, N_CARRY, SUBLANES, tn), F32),
            pltpu.VMEM((SUBLANES, LORA_PAD), F32),
        ],
        compiler_params=pltpu.CompilerParams(
            dimension_semantics=("arbitrary", "arbitrary", "arbitrary"),
            vmem_limit_bytes=VMEM_LIMIT_BYTES),
        name="mixer_in",
    )(x, *[w for w, _ in w_groups], wl, wup, prm, mu_l)


def _wkv_block(r, lw, k, v, kk, a, z0, consts, n_chunks):
    cs = consts
    c = CHUNK

    def stack(x):
        xb = x.astype(BF16)
        m0, m1 = (cs["m0"], cs["m1"]) if x.shape[1] == LANES else (cs["m0_2"], cs["m1_2"])
        return jnp.concatenate([xb * m0, xb * m1], axis=0)

    def bdiag(xw):
        xb = xw.astype(BF16)
        return jnp.concatenate([xb, xb], axis=0) * cs["bd_bf"]

    def each(fn, *lists):
        return [fn(*args) for args in zip(*lists)]

    cum = each(lambda x: _mm_exact_lhs(cs["tri"], x), lw)
    beta = each(lambda x, y: x * y, kk, a)
    e_inv = each(lambda x: jnp.exp(-x), cum)
    e_rem = each(lambda x: jnp.exp(x[c - 1:c, :] - x), cum)
    rt = each(lambda x, y: x * jnp.exp(y), r, cum)
    at = each(lambda x, y, z_: -x * jnp.exp(y - z_), kk, cum, lw)
    kt = each(lambda x, y: x * y, k, e_inv)
    bt = each(lambda x, y: x * y, beta, e_inv)
    kh = each(lambda x, y: x * y, k, e_rem)
    bh = each(lambda x, y: x * y, beta, e_rem)

    s = each(lambda a_, r_, b_, k_: _mm_nt(jnp.concatenate([a_, r_], axis=0),
                                           jnp.concatenate([stack(b_), stack(k_)], axis=0)),
             at, rt, bt, kt)
    lab = each(lambda x: jnp.where(cs["strict"], x[0:c, 0:2 * c], 0.0), s)
    urb = each(lambda x: jnp.where(cs["incl"], x[c:, 0:2 * c], 0.0), s)

    def twice(x):
        xb = x.astype(BF16)
        return jnp.concatenate([xb, xb], axis=0)

    lab2 = each(twice, lab)
    l8 = each(lambda x: jnp.where(cs["blk8"], x, 0.0), lab)
    l8_2 = each(lambda x, x2: _mm(x, x2 * cs["blk8_bd"]), l8, lab2)
    l8_4 = each(lambda x: _mm(x, bdiag(x)), l8_2)
    tw = each(lambda x, y: _mm(cs["eye"] + x, bdiag(cs["eye"] + y)), l8, l8_2)
    tw = each(lambda x, y: _mm(x, bdiag(cs["eye"] + y)), tw, l8_4)
    for join_bd in cs["joins_bd"]:
        half = each(lambda t_, x2: _mm(t_, x2 * join_bd), tw, lab2)
        tw = each(lambda t_, h_: t_ + _mm(h_, bdiag(t_)), tw, half)

    nv_uv = each(lambda x, v_: _mm(jnp.where(cs["strict_incl"], x[:, 2 * c:], 0.0), stack(v_)),
                 s, v)
    ah_d = each(lambda t_, a_, n_: _mm(t_, stack(jnp.concatenate([a_, n_[0:c]], axis=1))),
                tw, at, nv_uv)
    u2 = each(lambda u_, x: _mm(u_, stack(x)), urb, ah_d)
    rh = each(lambda r_, u_: r_ + u_[:, 0:LANES], rt, u2)
    e = each(lambda u_, w_: u_[:, LANES:] + w_[c:], u2, nv_uv)
    zeros = jnp.zeros((c, LANES), BF16)
    gf = each(lambda b_, k_, x, v_: _mm(
        jnp.concatenate([b_, k_], axis=0).T,
        jnp.concatenate([x.astype(BF16), jnp.concatenate([zeros, v_.astype(BF16)], axis=1)],
                        axis=0)), bh, kh, ah_d, v)
    gl = each(lambda x: jnp.where(cs["bd"], x[:, 0:LANES], 0.0), gf)
    f = each(lambda x: jnp.where(cs["bd"], x[:, LANES:], 0.0), gf)
    decay_col = each(lambda x: jnp.exp(x.T[:, c - 1:c]), cum)

    ys = [None] * len(r)
    states = list(z0)
    for ci in range(n_chunks):
        for p in range(len(states)):
            n = p * n_chunks + ci
            z = states[p]
            rz_gz = _mm(jnp.concatenate([rh[n], gl[n]], axis=0), z)
            ys[n] = rz_gz[0:c] + e[n]
            states[p] = decay_col[n] * z + rz_gz[c:] + f[n]
    return ys, states


def _wkv_consts():
    c = CHUNK
    assert c == HEAD_SIZE and 2 * c == LANES
    t = lax.broadcasted_iota(jnp.int32, (c, 2 * c), 0)
    i = jnp.bitwise_and(lax.broadcasted_iota(jnp.int32, (c, 2 * c), 1), c - 1)
    rr = lax.broadcasted_iota(jnp.int32, (c, c), 0)
    cc = lax.broadcasted_iota(jnp.int32, (c, c), 1)
    def head_lane_masks(width):
        lane = jnp.bitwise_and(lax.broadcasted_iota(jnp.int32, (c, width), 1), LANES - 1)
        return (jnp.where(lane < HEAD_SIZE, 1.0, 0.0).astype(BF16),
                jnp.where(lane < HEAD_SIZE, 0.0, 1.0).astype(BF16))

    m0, m1 = head_lane_masks(LANES)
    m0_2, m1_2 = head_lane_masks(2 * LANES)
    bd = _iota_shr((2 * c, 2 * c), 0, LOG2_HEAD) == _iota_shr((2 * c, 2 * c), 1, LOG2_HEAD)
    t2 = lax.broadcasted_iota(jnp.int32, (2 * c, 2 * c), 0)
    i2 = jnp.bitwise_and(lax.broadcasted_iota(jnp.int32, (2 * c, 2 * c), 1), c - 1)
    strict_incl = jnp.logical_or(jnp.logical_and(t2 < c, i2 < t2),
                                 jnp.logical_and(t2 >= c, i2 <= t2 - c))
    tt = jnp.bitwise_and(t2, c - 1)
    blk2 = [lax.shift_right_logical(tt, s) == lax.shift_right_logical(i2, s) for s in (3, 4, 5)]
    out2 = [jnp.logical_not(b) for b in blk2]
    joins = [jnp.logical_and(blk2[1], out2[0]), jnp.logical_and(blk2[2], out2[1]), out2[2]]

    def bd_mask(m):
        return jnp.where(jnp.logical_and(bd, m), 1.0, 0.0).astype(BF16)

    return dict(
        strict=i < t, incl=i <= t, strict_incl=strict_incl,
        eye=jnp.where(i == t, 1.0, 0.0).astype(F32),
        blk8=lax.shift_right_logical(t, 3) == lax.shift_right_logical(i, 3),
        blk8_bd=bd_mask(blk2[0]), joins_bd=[bd_mask(m) for m in joins],
        tri=jnp.where(cc <= rr, 1.0, 0.0).astype(BF16),
        m0=m0, m1=m1, m0_2=m0_2, m1_2=m1_2,
        bd=bd, bd_bf=jnp.where(bd, 1.0, 0.0).astype(BF16))


def _wkv_kernel(r_ref, lw_ref, k_ref, v_ref, kk_ref, a_ref, c1_ref, og_ref, gn_ref,
                x_ref, wo_ref, ln_ref, wgu_ref, wdn_ref, o_ref, obf_ref, wgu_bf_ref, wdn_bf_ref,
                z_ref, *, n_chunks, n_pairs, alpha):
    @pl.when(pl.program_id(1) == 0)
    def _():
        z_ref[...] = jnp.zeros_like(z_ref)

    wgu_bf_ref[...] = wgu_ref[...].astype(BF16)
    wdn_bf_ref[...] = wdn_ref[...].astype(BF16)

    def chunks(ref, dtype=F32):
        return [ref[0, ci * CHUNK:(ci + 1) * CHUNK, p * LANES:(p + 1) * LANES].astype(dtype)
                for p in range(n_pairs) for ci in range(n_chunks)]

    ys, states = _wkv_block(chunks(r_ref), chunks(lw_ref), chunks(k_ref), chunks(v_ref, BF16),
                            chunks(kk_ref), chunks(a_ref), [z_ref[p] for p in range(n_pairs)],
                            _wkv_consts(), n_chunks)
    ones = _head_ones(LANES)
    inv_n = 1.0 / HEAD_SIZE
    y = jnp.concatenate(ys, axis=0)
    mu = _mm(y, ones) * inv_n
    yc = y - mu
    var = _mm(yc * yc, ones) * inv_n
    yn = yc * lax.rsqrt(var + GN_EPS)
    tb = n_chunks * CHUNK
    merged = []
    for p in range(n_pairs):
        z_ref[p] = states[p]
        ls = slice(p * LANES, (p + 1) * LANES)
        yp = yn[p * tb:(p + 1) * tb] * gn_ref[0:1, ls] + gn_ref[1:2, ls]
        merged.append((c1_ref[0, :, ls].astype(F32)
                       + og_ref[0, :, ls].astype(F32) * yp).astype(BF16))
    h = alpha * x_ref[0] + jnp.dot(jnp.concatenate(merged, axis=1), wo_ref[...],
                                   preferred_element_type=F32)
    x1 = _layer_norm(h, ln_ref[0:1, :], ln_ref[1:2, :])
    o_ref[0] = x1
    obf_ref[0] = x1.astype(BF16)


def _layer_norm(h, g, b):
    mu = jnp.mean(h, axis=-1, keepdims=True)
    hc = h - mu
    var = jnp.mean(hc * hc, axis=-1, keepdims=True)
    return hc * lax.rsqrt(var + LN_EPS) * g + b


def _cast_spec(w, n_blocks_t, n_steps):
    rows, cols = w.shape
    nb = max(n for n in range(1, n_steps + 1)
             if rows % n == 0 and (rows // n) % BF16_ROWS == 0)
    return pl.BlockSpec((rows // nb, cols),
                        lambda b, i: (lax.div((b * n_blocks_t + i) * nb, n_steps), 0))


def _wkv(r, lw, k, v, kk, a, c1, og, gn, x, wo, ln, w_gu, w_down, *, alpha, tb):
    bsz, t, dm = r.shape
    n_pairs = dm // LANES
    n_blocks_t = t // tb
    spec = pl.BlockSpec((1, tb, dm), lambda b, i: (b, i, 0))
    row_spec = pl.BlockSpec((SUBLANES, dm), lambda b, i: (0, 0))
    cast_specs = [_cast_spec(w, n_blocks_t, bsz * n_blocks_t) for w in (w_gu, w_down)]
    return pl.pallas_call(
        functools.partial(_wkv_kernel, n_chunks=tb // CHUNK, n_pairs=n_pairs, alpha=alpha),
        grid=(bsz, n_blocks_t),
        in_specs=[spec] * 8 + [row_spec, spec, pl.BlockSpec((dm, dm), lambda b, i: (0, 0)),
                               row_spec] + cast_specs,
        out_specs=[spec, spec] + cast_specs,
        out_shape=[jax.ShapeDtypeStruct((bsz, t, dm), F32),
                   jax.ShapeDtypeStruct((bsz, t, dm), BF16),
                   jax.ShapeDtypeStruct(w_gu.shape, BF16),
                   jax.ShapeDtypeStruct(w_down.shape, BF16)],
        scratch_shapes=[pltpu.VMEM((n_pairs, LANES, LANES), F32)],
        compiler_params=pltpu.CompilerParams(
            dimension_semantics=("arbitrary", "arbitrary"),
            vmem_limit_bytes=VMEM_LIMIT_BYTES),
        name="wkv_out",
    )(r, lw, k, v, kk, a, c1, og, gn, x, wo, ln, w_gu, w_down)


def _ffn_kernel(xb_ref, x_ref, wg_ref, wu_ref, wd_ref, ln_ref, o_ref, acc_ref, *, alpha):
    f = pl.program_id(1)

    @pl.when(jnp.logical_and(pl.program_id(0) == 0, f == 0))
    def _():
        acc_ref[...] = jnp.zeros_like(acc_ref)

    xb = xb_ref[...]
    gate = jnp.dot(xb, wg_ref[...], preferred_element_type=F32)
    up = jnp.dot(xb, wu_ref[...], preferred_element_type=F32)
    hid = (gate * jax.nn.sigmoid(gate) * up).astype(BF16)
    part = jnp.dot(hid, wd_ref[...], preferred_element_type=F32)
    acc_ref[...] += part

    @pl.when(f == pl.num_programs(1) - 1)
    def _():
        h = alpha * x_ref[...] + acc_ref[...]
        o_ref[...] = _layer_norm(h, ln_ref[0:1, :], ln_ref[1:2, :])
        acc_ref[...] = jnp.zeros_like(acc_ref)


def _ffn(x_bf, x2d, w_gu, w_down, ln, *, alpha, tm, tf):
    m, dm = x2d.shape
    dff = w_down.shape[0]
    nf = dff // tf
    return pl.pallas_call(
        functools.partial(_ffn_kernel, alpha=alpha),
        grid=(m // tm, nf),
        in_specs=[
            pl.BlockSpec((tm, dm), lambda i, f: (i, 0)),
            pl.BlockSpec((tm, dm), lambda i, f: (i, 0)),
            pl.BlockSpec((dm, tf), lambda i, f: (0, f)),
            pl.BlockSpec((dm, tf), lambda i, f: (0, nf + f)),
            pl.BlockSpec((tf, dm), lambda i, f: (f, 0)),
            pl.BlockSpec((SUBLANES, dm), lambda i, f: (0, 0)),
        ],
        out_specs=pl.BlockSpec((tm, dm), lambda i, f: (i, 0)),
        out_shape=jax.ShapeDtypeStruct((m, dm), F32),
        scratch_shapes=[pltpu.VMEM((tm, dm), F32)],
        compiler_params=pltpu.CompilerParams(
            dimension_semantics=("arbitrary", "arbitrary"), vmem_limit_bytes=VMEM_LIMIT_BYTES),
        name="ffn",
    )(x_bf, x2d, w_gu, w_gu, w_down, ln)


def _pad_rows(a, n):
    return jnp.pad(a, ((0, n - a.shape[0]), (0, 0)))


def _pick(n, candidates):
    for c in candidates:
        if n % c == 0:
            return c
    raise ValueError(f"no tile size for extent {n}")


def kernel(x, w_in, shift_mu, conv_w, w0, w_up, a0, a_up, g_up, k_k, k_a, r_k,
           gn_g, gn_b, w_o, ln1_g, ln1_b, w_gu, w_down, ln2_g, ln2_b):
    bsz, t, dm = x.shape
    depth = w_in.shape[0]
    lw_n, la_n, lg_n = w_up.shape[1], a_up.shape[1], g_up.shape[1]
    assert dm % (2 * LANES) == 0 and t % CHUNK == 0
    assert lw_n <= LANES and la_n <= LANES and lg_n <= 2 * LANES
    assert w_in.shape[2] == 8 * dm + lw_n + la_n + lg_n
    alpha = (2.0 * depth) ** 0.25
    tn = 2 * LANES
    tm_in = _pick(t, (512, 256, 128, 64))
    tb = _pick(t, (128, 64))
    m = bsz * t
    tm = _pick(m, (512, 256, 128, 64))
    tf = _pick(w_down.shape[1], (512, 256, 128))

    for l in range(depth):
        wi = w_in[l]
        c0 = 6 * dm
        wi_bf = wi.astype(BF16)
        g0 = c0 + lw_n + la_n + lg_n
        nj = dm // tn
        w_groups = [(wi_bf, 0), (wi_bf, nj), (wi_bf, 2 * nj), (wi_bf[:, g0:g0 + dm], 0),
                    (wi_bf, 3 * nj), (wi_bf, 4 * nj), (wi_bf, 5 * nj), (wi_bf[:, g0 + dm:], 0)]

        def lane_pad(a, n):
            return jnp.pad(a, ((0, 0), (0, n - a.shape[1])))

        wl = jnp.concatenate([
            lane_pad(wi[:, c0:c0 + lw_n], LANES),
            lane_pad(wi[:, c0 + lw_n:c0 + lw_n + la_n], LANES),
            lane_pad(wi[:, c0 + lw_n + la_n:c0 + lw_n + la_n + lg_n], 2 * LANES)],
            axis=1).astype(BF16)
        mu = shift_mu[l]
        s0 = 3 * dm
        mu_l = jnp.concatenate([
            jnp.pad(mu[s0:s0 + lw_n], (0, LANES - lw_n)),
            jnp.pad(mu[s0 + lw_n:s0 + lw_n + la_n], (0, LANES - la_n)),
            jnp.pad(mu[s0 + lw_n + la_n:], (0, 2 * LANES - lg_n))])[None, :]
        wup = jnp.concatenate([_pad_rows(w_up[l], LANES), _pad_rows(a_up[l], LANES),
                               _pad_rows(g_up[l], 2 * LANES)], axis=0).astype(BF16)
        prm = _pad_rows(jnp.stack([
            mu[0:dm], mu[dm:2 * dm], mu[2 * dm:3 * dm],
            conv_w[l, 0], conv_w[l, 1], conv_w[l, 2],
            w0[l], a0[l], k_k[l], k_a[l], r_k[l].reshape(-1)]), 16)

        r, lw, k, v, kk, a, c1, og = _mixer_in(
            x, w_groups, wl, wup, prm, mu_l, tm=tm_in, tn=tn)
        gn = _pad_rows(jnp.stack([gn_g[l], gn_b[l]]), SUBLANES)
        ln1 = _pad_rows(jnp.stack([ln1_g[l], ln1_b[l]]), SUBLANES)
        x1, x1_bf, w_gu_bf, w_down_bf = _wkv(
            r, lw, k, v, kk, a, c1, og, gn, x, w_o[l].astype(BF16), ln1, w_gu[l], w_down[l],
            alpha=alpha, tb=tb)
        ln2 = _pad_rows(jnp.stack([ln2_g[l], ln2_b[l]]), SUBLANES)
        x = _ffn(x1_bf.reshape(m, dm), x1.reshape(m, dm), w_gu_bf, w_down_bf, ln2,
                 alpha=alpha, tm=tm, tf=tf).reshape(bsz, t, dm)
    return x
```

```python
import functools
import math

import jax
import jax.numpy as jnp
from jax import lax
from jax.experimental import pallas as pl
from jax.experimental.pallas import tpu as pltpu

HEAD_SIZE = 64
LOG2_HEAD = 6
LOG_DECAY_SCALE = -math.exp(-0.5)
LN_EPS = 1e-5
GN_EPS = 64e-5
LANES = 128
SUBLANES = 8
BF16_ROWS = 16
CHUNK = 64
VMEM_LIMIT_BYTES = 56 * 1024 * 1024

F32 = jnp.float32
BF16 = jnp.bfloat16


def _mm(a, b):
    return jnp.dot(a.astype(BF16), b.astype(BF16), preferred_element_type=F32)


def _mm_nt(a, b):
    return lax.dot_general(a.astype(BF16), b.astype(BF16), (((1,), (1,)), ((), ())),
                           preferred_element_type=F32)


def _split2(a):
    hi = a.astype(BF16)
    lo = (a - hi.astype(F32)).astype(BF16)
    return hi, lo


def _mm_exact_lhs(a_exact, b):
    d = functools.partial(jnp.dot, preferred_element_type=F32)
    h, l = _split2(b)
    return d(a_exact, h) + d(a_exact, l)


def _iota_shr(shape, dim, log2_div):
    return lax.shift_right_logical(lax.broadcasted_iota(jnp.int32, shape, dim), log2_div)


def _head_ones(n):
    r = _iota_shr((n, n), 0, LOG2_HEAD)
    c = _iota_shr((n, n), 1, LOG2_HEAD)
    return jnp.where(r == c, 1.0, 0.0).astype(BF16)


N_GROUPS = 8
N_CARRY = 4
LORA_PAD = 512


def _mixer_in_kernel(x_ref, *refs, tm, tn):
    w_refs = refs[:N_GROUPS]
    wl_ref, wup_ref, p_ref, mul_ref = refs[N_GROUPS:N_GROUPS + 4]
    r_ref, lw_ref, k_ref, v_ref, kk_ref, a_ref, c1_ref, og_ref = refs[N_GROUPS + 4:N_GROUPS + 12]
    xb_ref, lora_ref, carry_ref, carry_l_ref = refs[N_GROUPS + 12:]
    i = pl.program_id(1)
    j = pl.program_id(2)
    first = i == 0
    row = lax.broadcasted_iota(jnp.int32, (tm, tn), 0)

    @pl.when(j == 0)
    def _():
        xb_ref[...] = x_ref[0].astype(BF16)

    def xdot(w_ref):
        return jnp.dot(xb_ref[...], w_ref[...], preferred_element_type=F32)

    def prev1(p, c8):
        return jnp.where(row == 0, c8[SUBLANES - 1:SUBLANES, :], pltpu.roll(p, 1, 0))

    def prev2(p, c8):
        rolled = pltpu.roll(p, 2, 0)
        rolled = jnp.where(row == 1, c8[SUBLANES - 1:SUBLANES, :], rolled)
        return jnp.where(row == 0, c8[SUBLANES - 2:SUBLANES - 1, :], rolled)

    @pl.when(j == 0)
    def _():
        pl_ = xdot(wl_ref)
        cl = jnp.where(first, 0.0, carry_l_ref[...])
        rowl = lax.broadcasted_iota(jnp.int32, (tm, LORA_PAD), 0)
        prev = jnp.where(rowl == 0, cl[SUBLANES - 1:SUBLANES, :], pltpu.roll(pl_, 1, 0))
        carry_l_ref[...] = pl_[tm - SUBLANES:, :]
        z = pl_ + mul_ref[...] * (prev - pl_)
        lora_ref[:, 0:LANES] = jnp.tanh(z[:, 0:LANES]).astype(BF16)
        lora_ref[:, LANES:2 * LANES] = z[:, LANES:2 * LANES].astype(BF16)
        lora_ref[:, 2 * LANES:] = jax.nn.sigmoid(z[:, 2 * LANES:]).astype(BF16)

    prm = p_ref[...]
    mu_r, mu_k, mu_v = prm[0:1], prm[1:2], prm[2:3]
    cw0, cw1, cw2 = prm[3:4], prm[4:5], prm[5:6]
    w0, a0, k_k, k_a, r_k = prm[6:7], prm[7:8], prm[8:9], prm[9:10], prm[10:11]
    carry = [jnp.where(first, 0.0, carry_ref[j, q]) for q in range(N_CARRY)]
    w_cb, w_cc, w_ch, w_gc, w_r, w_k, w_v, w_gr = w_refs
    d = functools.partial(jnp.dot, preferred_element_type=F32)

    wup = wup_ref[...]
    w_pre = w0 + d(lora_ref[:, 0:LANES], wup[0:LANES])
    a_pre = a0 + d(lora_ref[:, LANES:2 * LANES], wup[LANES:2 * LANES])
    g = d(lora_ref[:, 2 * LANES:], wup[2 * LANES:])
    lw = LOG_DECAY_SCALE * jax.nn.sigmoid(w_pre)
    a = jax.nn.sigmoid(a_pre)

    p_r, p_k, p_v = xdot(w_r), xdot(w_k), xdot(w_v)
    r = p_r + mu_r * (prev1(p_r, carry[1]) - p_r)
    k = p_k + mu_k * (prev1(p_k, carry[2]) - p_k)
    v = p_v + mu_v * (prev1(p_v, carry[3]) - p_v)
    p_gr, p_cc, p_ch = xdot(w_gr), xdot(w_cc), xdot(w_ch)

    ones = _head_ones(tn)
    kraw = k * k_k
    ss = _mm(kraw * kraw, ones)
    kk = kraw / jnp.maximum(jnp.sqrt(ss), 1e-12)
    k_mod = k * (1.0 + (a - 1.0) * k_a)
    bonus = _mm(r * k_mod * r_k, ones) * v
    og = g * jax.nn.sigmoid(p_gr)

    u = p_cc * p_ch
    conv = cw2 * u + cw1 * prev1(u, carry[0]) + cw0 * prev2(u, carry[0])
    p_gc, p_cb = xdot(w_gc), xdot(w_cb)
    y_conv = (jax.nn.sigmoid(p_gc) * conv) * p_cb
    for q, val in enumerate((u, p_r, p_k, p_v)):
        carry_ref[j, q] = val[tm - SUBLANES:, :]

    r_ref[0] = r.astype(r_ref.dtype)
    lw_ref[0] = lw
    k_ref[0] = k_mod.astype(k_ref.dtype)
    v_ref[0] = v.astype(v_ref.dtype)
    kk_ref[0] = kk.astype(kk_ref.dtype)
    a_ref[0] = a.astype(a_ref.dtype)
    c1_ref[0] = (y_conv + og * bonus).astype(c1_ref.dtype)
    og_ref[0] = og.astype(og_ref.dtype)


def _mixer_in(x, w_groups, wl, wup, prm, mu_l, *, tm, tn):
    bsz, t, dm = x.shape
    nj = dm // tn
    grid = (bsz, t // tm, nj)
    out_sds = jax.ShapeDtypeStruct((bsz, t, dm), F32)
    out_bf = jax.ShapeDtypeStruct((bsz, t, dm), BF16)
    out_spec = pl.BlockSpec((1, tm, tn), lambda b, i, j: (b, i, j))
    return pl.pallas_call(
        functools.partial(_mixer_in_kernel, tm=tm, tn=tn),
        grid=grid,
        in_specs=[
            pl.BlockSpec((1, tm, dm), lambda b, i, j: (b, i, 0)),
            *[pl.BlockSpec((dm, tn), lambda b, i, j, o=o: (0, o + j)) for _, o in w_groups],
            pl.BlockSpec((dm, LORA_PAD), lambda b, i, j: (0, 0)),
            pl.BlockSpec((LORA_PAD, tn), lambda b, i, j: (0, j)),
            pl.BlockSpec((16, tn), lambda b, i, j: (0, j)),
            pl.BlockSpec((1, LORA_PAD), lambda b, i, j: (0, 0)),
        ],
        out_specs=[out_spec] * 8,
        out_shape=[out_bf, out_sds, out_bf, out_bf, out_bf, out_bf, out_bf, out_bf],
        scratch_shapes=[
            pltpu.VMEM((tm, dm), BF16),
            pltpu.VMEM((tm, LORA_PAD), BF16),
            pltpu.VMEM((nj, N_CARRY, SUBLANES, tn), F32),
            pltpu.VMEM((SUBLANES, LORA_PAD), F32),
        ],
        compiler_params=pltpu.CompilerParams(
            dimension_semantics=("arbitrary", "arbitrary", "arbitrary"),
            vmem_limit_bytes=VMEM_LIMIT_BYTES),
        name="mixer_in",
    )(x, *[w for w, _ in w_groups], wl, wup, prm, mu_l)


def _wkv_block(r, lw, k, v, kk, a, z0, consts, n_chunks):
    cs = consts
    c = CHUNK

    def stack(x):
        xb = x.astype(BF16)
        m0, m1 = (cs["m0"], cs["m1"]) if x.shape[1] == LANES else (cs["m0_2"], cs["m1_2"])
        return jnp.concatenate([xb * m0, xb * m1], axis=0)

    def bdiag(xw):
        xb = xw.astype(BF16)
        return jnp.concatenate([xb, xb], axis=0) * cs["bd_bf"]

    def each(fn, *lists):
        return [fn(*args) for args in zip(*lists)]

    cum = each(lambda x: _mm_exact_lhs(cs["tri"], x), lw)
    beta = each(lambda x, y: x * y, kk, a)
    e_inv = each(lambda x: jnp.exp(-x), cum)
    e_rem = each(lambda x: jnp.exp(x[c - 1:c, :] - x), cum)
    rt = each(lambda x, y: x * jnp.exp(y), r, cum)
    at = each(lambda x, y, z_: -x * jnp.exp(y - z_), kk, cum, lw)
    kt = each(lambda x, y: x * y, k, e_inv)
    bt = each(lambda x, y: x * y, beta, e_inv)
    kh = each(lambda x, y: x * y, k, e_rem)
    bh = each(lambda x, y: x * y, beta, e_rem)

    s = each(lambda a_, r_, b_, k_: _mm_nt(jnp.concatenate([a_, r_], axis=0),
                                           jnp.concatenate([stack(b_), stack(k_)], axis=0)),
             at, rt, bt, kt)
    lab = each(lambda x: jnp.where(cs["strict"], x[0:c, 0:2 * c], 0.0), s)
    urb = each(lambda x: jnp.where(cs["incl"], x[c:, 0:2 * c], 0.0), s)

    def twice(x):
        xb = x.astype(BF16)
        return jnp.concatenate([xb, xb], axis=0)

    lab2 = each(twice, lab)
    l8 = each(lambda x: jnp.where(cs["blk8"], x, 0.0), lab)
    l8_2 = each(lambda x, x2: _mm(x, x2 * cs["blk8_bd"]), l8, lab2)
    l8_4 = each(lambda x: _mm(x, bdiag(x)), l8_2)
    tw = each(lambda x, y: _mm(cs["eye"] + x, bdiag(cs["eye"] + y)), l8, l8_2)
    tw = each(lambda x, y: _mm(x, bdiag(cs["eye"] + y)), tw, l8_4)
    for join_bd in cs["joins_bd"]:
        half = each(lambda t_, x2: _mm(t_, x2 * join_bd), tw, lab2)
        tw = each(lambda t_, h_: t_ + _mm(h_, bdiag(t_)), tw, half)

    nv_uv = each(lambda x, v_: _mm(jnp.where(cs["strict_incl"], x[:, 2 * c:], 0.0), stack(v_)),
                 s, v)
    ah_d = each(lambda t_, a_, n_: _mm(t_, stack(jnp.concatenate([a_, n_[0:c]], axis=1))),
                tw, at, nv_uv)
    u2 = each(lambda u_, x: _mm(u_, stack(x)), urb, ah_d)
    rh = each(lambda r_, u_: r_ + u_[:, 0:LANES], rt, u2)
    e = each(lambda u_, w_: u_[:, LANES:] + w_[c:], u2, nv_uv)
    zeros = jnp.zeros((c, LANES), BF16)
    gf = each(lambda b_, k_, x, v_: _mm(
        jnp.concatenate([b_, k_], axis=0).T,
        jnp.concatenate([x.astype(BF16), jnp.concatenate([zeros, v_.astype(BF16)], axis=1)],
                        axis=0)), bh, kh, ah_d, v)
    gl = each(lambda x: jnp.where(cs["bd"], x[:, 0:LANES], 0.0), gf)
    f = each(lambda x: jnp.where(cs["bd"], x[:, LANES:], 0.0), gf)
    decay_col = each(lambda x: jnp.exp(x.T[:, c - 1:c]), cum)

    ys = [None] * len(r)
    states = list(z0)
    for ci in range(n_chunks):
        for p in range(len(states)):
            n = p * n_chunks + ci
            z = states[p]
            rz_gz = _mm(jnp.concatenate([rh[n], gl[n]], axis=0), z)
            ys[n] = rz_gz[0:c] + e[n]
            states[p] = decay_col[n] * z + rz_gz[c:] + f[n]
    return ys, states


def _wkv_consts():
    c = CHUNK
    assert c == HEAD_SIZE and 2 * c == LANES
    t = lax.broadcasted_iota(jnp.int32, (c, 2 * c), 0)
    i = jnp.bitwise_and(lax.broadcasted_iota(jnp.int32, (c, 2 * c), 1), c - 1)
    rr = lax.broadcasted_iota(jnp.int32, (c, c), 0)
    cc = lax.broadcasted_iota(jnp.int32, (c, c), 1)
    def head_lane_masks(width):
        lane = jnp.bitwise_and(lax.broadcasted_iota(jnp.int32, (c, width), 1), LANES - 1)
        return (jnp.where(lane < HEAD_SIZE, 1.0, 0.0).astype(BF16),
                jnp.where(lane < HEAD_SIZE, 0.0, 1.0).astype(BF16))

    m0, m1 = head_lane_masks(LANES)
    m0_2, m1_2 = head_lane_masks(2 * LANES)
    bd = _iota_shr((2 * c, 2 * c), 0, LOG2_HEAD) == _iota_shr((2 * c, 2 * c), 1, LOG2_HEAD)
    t2 = lax.broadcasted_iota(jnp.int32, (2 * c, 2 * c), 0)
    i2 = jnp.bitwise_and(lax.broadcasted_iota(jnp.int32, (2 * c, 2 * c), 1), c - 1)
    strict_incl = jnp.logical_or(jnp.logical_and(t2 < c, i2 < t2),
                                 jnp.logical_and(t2 >= c, i2 <= t2 - c))
    tt = jnp.bitwise_and(t2, c - 1)
    blk2 = [lax.shift_right_logical(tt, s) == lax.shift_right_logical(i2, s) for s in (3, 4, 5)]
    out2 = [jnp.logical_not(b) for b in blk2]
    joins = [jnp.logical_and(blk2[1], out2[0]), jnp.logical_and(blk2[2], out2[1]), out2[2]]

    def bd_mask(m):
        return jnp.where(jnp.logical_and(bd, m), 1.0, 0.0).astype(BF16)

    return dict(
        strict=i < t, incl=i <= t, strict_incl=strict_incl,
        eye=jnp.where(i == t, 1.0, 0.0).astype(F32),
        blk8=lax.shift_right_logical(t, 3) == lax.shift_right_logical(i, 3),
        blk8_bd=bd_mask(blk2[0]), joins_bd=[bd_mask(m) for m in joins],
        tri=jnp.where(cc <= rr, 1.0, 0.0).astype(BF16),
        m0=m0, m1=m1, m0_2=m0_2, m1_2=m1_2,
        bd=bd, bd_bf=jnp.where(bd, 1.0, 0.0).astype(BF16))


def _wkv_kernel(r_ref, lw_ref, k_ref, v_ref, kk_ref, a_ref, c1_ref, og_ref, gn_ref,
                x_ref, wo_ref, ln_ref, wgu_ref, wdn_ref, o_ref, obf_ref, wgu_bf_ref, wdn_bf_ref,
                z_ref, *, n_chunks, n_pairs, alpha):
    @pl.when(pl.program_id(1) == 0)
    def _():
        z_ref[...] = jnp.zeros_like(z_ref)

    wgu_bf_ref[...] = wgu_ref[...].astype(BF16)
    wdn_bf_ref[...] = wdn_ref[...].astype(BF16)

    def chunks(ref, dtype=F32):
        return [ref[0, ci * CHUNK:(ci + 1) * CHUNK, p * LANES:(p + 1) * LANES].astype(dtype)
                for p in range(n_pairs) for ci in range(n_chunks)]

    ys, states = _wkv_block(chunks(r_ref), chunks(lw_ref), chunks(k_ref), chunks(v_ref, BF16),
                            chunks(kk_ref), chunks(a_ref), [z_ref[p] for p in range(n_pairs)],
                            _wkv_consts(), n_chunks)
    ones = _head_ones(LANES)
    inv_n = 1.0 / HEAD_SIZE
    y = jnp.concatenate(ys, axis=0)
    mu = _mm(y, ones) * inv_n
    yc = y - mu
    var = _mm(yc * yc, ones) * inv_n
    yn = yc * lax.rsqrt(var + GN_EPS)
    tb = n_chunks * CHUNK
    merged = []
    for p in range(n_pairs):
        z_ref[p] = states[p]
        ls = slice(p * LANES, (p + 1) * LANES)
        yp = yn[p * tb:(p + 1) * tb] * gn_ref[0:1, ls] + gn_ref[1:2, ls]
        merged.append((c1_ref[0, :, ls].astype(F32)
                       + og_ref[0, :, ls].astype(F32) * yp).astype(BF16))
    h = alpha * x_ref[0] + jnp.dot(jnp.concatenate(merged, axis=1), wo_ref[...],
                                   preferred_element_type=F32)
    x1 = _layer_norm(h, ln_ref[0:1, :], ln_ref[1:2, :])
    o_ref[0] = x1
    obf_ref[0] = x1.astype(BF16)


def _layer_norm(h, g, b):
    mu = jnp.mean(h, axis=-1, keepdims=True)
    hc = h - mu
    var = jnp.mean(hc * hc, axis=-1, keepdims=True)
    return hc * lax.rsqrt(var + LN_EPS) * g + b


def _cast_spec(w, n_blocks_t, n_steps):
    rows, cols = w.shape
    nb = max(n for n in range(1, n_steps + 1)
             if rows % n == 0 and (rows // n) % BF16_ROWS == 0)
    return pl.BlockSpec((rows // nb, cols),
                        lambda b, i: (lax.div((b * n_blocks_t + i) * nb, n_steps), 0))


def _wkv(r, lw, k, v, kk, a, c1, og, gn, x, wo, ln, w_gu, w_down, *, alpha, tb):
    bsz, t, dm = r.shape
    n_pairs = dm // LANES
    n_blocks_t = t // tb
    spec = pl.BlockSpec((1, tb, dm), lambda b, i: (b, i, 0))
    row_spec = pl.BlockSpec((SUBLANES, dm), lambda b, i: (0, 0))
    cast_specs = [_cast_spec(w, n_blocks_t, bsz * n_blocks_t) for w in (w_gu, w_down)]
    return pl.pallas_call(
        functools.partial(_wkv_kernel, n_chunks=tb // CHUNK, n_pairs=n_pairs, alpha=alpha),
        grid=(bsz, n_blocks_t),
        in_specs=[spec] * 8 + [row_spec, spec, pl.BlockSpec((dm, dm), lambda b, i: (0, 0)),
                               row_spec] + cast_specs,
        out_specs=[spec, spec] + cast_specs,
        out_shape=[jax.ShapeDtypeStruct((bsz, t, dm), F32),
                   jax.ShapeDtypeStruct((bsz, t, dm), BF16),
                   jax.ShapeDtypeStruct(w_gu.shape, BF16),
                   jax.ShapeDtypeStruct(w_down.shape, BF16)],
        scratch_shapes=[pltpu.VMEM((n_pairs, LANES, LANES), F32)],
        compiler_params=pltpu.CompilerParams(
            dimension_semantics=("arbitrary", "arbitrary"),
            vmem_limit_bytes=VMEM_LIMIT_BYTES),
        name="wkv_out",
    )(r, lw, k, v, kk, a, c1, og, gn, x, wo, ln, w_gu, w_down)


def _ffn_kernel(xb_ref, x_ref, wg_ref, wu_ref, wd_ref, ln_ref, o_ref, acc_ref, h_ref,
                *, alpha, n_tiles, n_slices):
    i = pl.program_id(0)
    f = pl.program_id(1)
    rows = h_ref.shape[0] // n_slices

    def norm_slice():
        r0 = pl.multiple_of(jnp.minimum(f, n_slices - 1) * rows, rows)
        o_ref[pl.ds(r0, rows), :] = _layer_norm(h_ref[pl.ds(r0, rows), :],
                                                ln_ref[0:1, :], ln_ref[1:2, :])

    @pl.when(jnp.logical_and(i == 0, f == 0))
    def _():
        acc_ref[...] = jnp.zeros_like(acc_ref)
        h_ref[...] = jnp.zeros_like(h_ref)

    @pl.when(i < n_tiles)
    def _():
        norm_slice()
        xb = xb_ref[...]
        gate = jnp.dot(xb, wg_ref[...], preferred_element_type=F32)
        up = jnp.dot(xb, wu_ref[...], preferred_element_type=F32)
        hid = (gate * jax.nn.sigmoid(gate) * up).astype(BF16)
        acc_ref[...] += jnp.dot(hid, wd_ref[...], preferred_element_type=F32)

    @pl.when(i == n_tiles)
    def _():
        norm_slice()

    @pl.when(jnp.logical_and(f == pl.num_programs(1) - 1, i < n_tiles))
    def _():
        h_ref[...] = alpha * x_ref[...] + acc_ref[...]
        acc_ref[...] = jnp.zeros_like(acc_ref)


def _ffn(x_bf, x2d, w_gu, w_down, ln, *, alpha, tm, tf):
    m, dm = x2d.shape
    dff = w_down.shape[0]
    nf = dff // tf
    n_tiles = m // tm
    n_slices = max(n for n in range(1, nf + 1) if tm % (n * SUBLANES) == 0)

    def tile(i):
        return jnp.minimum(i, n_tiles - 1)

    def fblk(i, f):
        return jnp.where(i == n_tiles, 0, f)

    return pl.pallas_call(
        functools.partial(_ffn_kernel, alpha=alpha, n_tiles=n_tiles, n_slices=n_slices),
        grid=(n_tiles + 1, nf),
        in_specs=[
            pl.BlockSpec((tm, dm), lambda i, f: (tile(i), 0)),
            pl.BlockSpec((tm, dm), lambda i, f: (tile(i), 0)),
            pl.BlockSpec((dm, tf), lambda i, f: (0, fblk(i, f))),
            pl.BlockSpec((dm, tf), lambda i, f: (0, nf + fblk(i, f))),
            pl.BlockSpec((tf, dm), lambda i, f: (fblk(i, f), 0)),
            pl.BlockSpec((SUBLANES, dm), lambda i, f: (0, 0)),
        ],
        out_specs=pl.BlockSpec((tm, dm), lambda i, f: (jnp.maximum(i - 1, 0), 0)),
        out_shape=jax.ShapeDtypeStruct((m, dm), F32),
        scratch_shapes=[pltpu.VMEM((tm, dm), F32), pltpu.VMEM((tm, dm), F32)],
        compiler_params=pltpu.CompilerParams(
            dimension_semantics=("arbitrary", "arbitrary"), vmem_limit_bytes=VMEM_LIMIT_BYTES),
        name="ffn",
    )(x_bf, x2d, w_gu, w_gu, w_down, ln)


def _pad_rows(a, n):
    return jnp.pad(a, ((0, n - a.shape[0]), (0, 0)))


def _pick(n, candidates):
    for c in candidates:
        if n % c == 0:
            return c
    raise ValueError(f"no tile size for extent {n}")


def kernel(x, w_in, shift_mu, conv_w, w0, w_up, a0, a_up, g_up, k_k, k_a, r_k,
           gn_g, gn_b, w_o, ln1_g, ln1_b, w_gu, w_down, ln2_g, ln2_b):
    bsz, t, dm = x.shape
    depth = w_in.shape[0]
    lw_n, la_n, lg_n = w_up.shape[1], a_up.shape[1], g_up.shape[1]
    assert dm % (2 * LANES) == 0 and t % CHUNK == 0
    assert lw_n <= LANES and la_n <= LANES and lg_n <= 2 * LANES
    assert w_in.shape[2] == 8 * dm + lw_n + la_n + lg_n
    alpha = (2.0 * depth) ** 0.25
    tn = 2 * LANES
    tm_in = _pick(t, (512, 256, 128, 64))
    tb = _pick(t, (128, 64))
    m = bsz * t
    tm = _pick(m, (512, 256, 128, 64))
    tf = _pick(w_down.shape[1], (512, 256, 128))

    for l in range(depth):
        wi = w_in[l]
        c0 = 6 * dm
        wi_bf = wi.astype(BF16)
        g0 = c0 + lw_n + la_n + lg_n
        nj = dm // tn
        w_groups = [(wi_bf, 0), (wi_bf, nj), (wi_bf, 2 * nj), (wi_bf[:, g0:g0 + dm], 0),
                    (wi_bf, 3 * nj), (wi_bf, 4 * nj), (wi_bf, 5 * nj), (wi_bf[:, g0 + dm:], 0)]

        def lane_pad(a, n):
            return jnp.pad(a, ((0, 0), (0, n - a.shape[1])))

        wl = jnp.concatenate([
            lane_pad(wi[:, c0:c0 + lw_n], LANES),
            lane_pad(wi[:, c0 + lw_n:c0 + lw_n + la_n], LANES),
            lane_pad(wi[:, c0 + lw_n + la_n:c0 + lw_n + la_n + lg_n], 2 * LANES)],
            axis=1).astype(BF16)
        mu = shift_mu[l]
        s0 = 3 * dm
        mu_l = jnp.concatenate([
            jnp.pad(mu[s0:s0 + lw_n], (0, LANES - lw_n)),
            jnp.pad(mu[s0 + lw_n:s0 + lw_n + la_n], (0, LANES - la_n)),
            jnp.pad(mu[s0 + lw_n + la_n:], (0, 2 * LANES - lg_n))])[None, :]
        wup = jnp.concatenate([_pad_rows(w_up[l], LANES), _pad_rows(a_up[l], LANES),
                               _pad_rows(g_up[l], 2 * LANES)], axis=0).astype(BF16)
        prm = _pad_rows(jnp.stack([
            mu[0:dm], mu[dm:2 * dm], mu[2 * dm:3 * dm],
            conv_w[l, 0], conv_w[l, 1], conv_w[l, 2],
            w0[l], a0[l], k_k[l], k_a[l], r_k[l].reshape(-1)]), 16)

        r, lw, k, v, kk, a, c1, og = _mixer_in(
            x, w_groups, wl, wup, prm, mu_l, tm=tm_in, tn=tn)
        gn = _pad_rows(jnp.stack([gn_g[l], gn_b[l]]), SUBLANES)
        ln1 = _pad_rows(jnp.stack([ln1_g[l], ln1_b[l]]), SUBLANES)
        x1, x1_bf, w_gu_bf, w_down_bf = _wkv(
            r, lw, k, v, kk, a, c1, og, gn, x, w_o[l].astype(BF16), ln1, w_gu[l], w_down[l],
            alpha=alpha, tb=tb)
        ln2 = _pad_rows(jnp.stack([ln2_g[l], ln2_b[l]]), SUBLANES)
        x = _ffn(x1_bf.reshape(m, dm), x1.reshape(m, dm), w_gu_bf, w_down_bf, ln2,
                 alpha=alpha, tm=tm, tf=tf).reshape(bsz, t, dm)
    return x
```

```python
import functools
import math

import jax
import jax.numpy as jnp
from jax import lax
from jax.experimental import pallas as pl
from jax.experimental.pallas import tpu as pltpu

HEAD_SIZE = 64
LOG2_HEAD = 6
LOG_DECAY_SCALE = -math.exp(-0.5)
LN_EPS = 1e-5
GN_EPS = 64e-5
LANES = 128
SUBLANES = 8
BF16_ROWS = 16
CHUNK = 64
VMEM_LIMIT_BYTES = 56 * 1024 * 1024

F32 = jnp.float32
BF16 = jnp.bfloat16


def _mm(a, b):
    return jnp.dot(a.astype(BF16), b.astype(BF16), preferred_element_type=F32)


def _mm_nt(a, b):
    return lax.dot_general(a.astype(BF16), b.astype(BF16), (((1,), (1,)), ((), ())),
                           preferred_element_type=F32)


def _split2(a):
    hi = a.astype(BF16)
    lo = (a - hi.astype(F32)).astype(BF16)
    return hi, lo


def _mm_exact_lhs(a_exact, b):
    d = functools.partial(jnp.dot, preferred_element_type=F32)
    h, l = _split2(b)
    return d(a_exact, h) + d(a_exact, l)


def _iota_shr(shape, dim, log2_div):
    return lax.shift_right_logical(lax.broadcasted_iota(jnp.int32, shape, dim), log2_div)


def _head_ones(n):
    r = _iota_shr((n, n), 0, LOG2_HEAD)
    c = _iota_shr((n, n), 1, LOG2_HEAD)
    return jnp.where(r == c, 1.0, 0.0).astype(BF16)


N_GROUPS = 8
N_CARRY = 4
LORA_PAD = 512


def _mixer_in_kernel(x_ref, *refs, tm, tn):
    w_refs = refs[:N_GROUPS]
    wl_ref, wup_ref, p_ref, mul_ref = refs[N_GROUPS:N_GROUPS + 4]
    r_ref, lw_ref, k_ref, v_ref, kk_ref, a_ref, c1_ref, og_ref = refs[N_GROUPS + 4:N_GROUPS + 12]
    xb_ref, lora_ref, carry_ref, carry_l_ref = refs[N_GROUPS + 12:]
    i = pl.program_id(1)
    j = pl.program_id(2)
    first = i == 0
    row = lax.broadcasted_iota(jnp.int32, (tm, tn), 0)

    @pl.when(j == 0)
    def _():
        xb_ref[...] = x_ref[0].astype(BF16)

    def xdot(w_ref):
        return jnp.dot(xb_ref[...], w_ref[...], preferred_element_type=F32)

    def prev1(p, c8):
        return jnp.where(row == 0, c8[SUBLANES - 1:SUBLANES, :], pltpu.roll(p, 1, 0))

    def prev2(p, c8):
        rolled = pltpu.roll(p, 2, 0)
        rolled = jnp.where(row == 1, c8[SUBLANES - 1:SUBLANES, :], rolled)
        return jnp.where(row == 0, c8[SUBLANES - 2:SUBLANES - 1, :], rolled)

    @pl.when(j == 0)
    def _():
        pl_ = xdot(wl_ref)
        cl = jnp.where(first, 0.0, carry_l_ref[...])
        rowl = lax.broadcasted_iota(jnp.int32, (tm, LORA_PAD), 0)
        prev = jnp.where(rowl == 0, cl[SUBLANES - 1:SUBLANES, :], pltpu.roll(pl_, 1, 0))
        carry_l_ref[...] = pl_[tm - SUBLANES:, :]
        z = pl_ + mul_ref[...] * (prev - pl_)
        lora_ref[:, 0:LANES] = jnp.tanh(z[:, 0:LANES]).astype(BF16)
        lora_ref[:, LANES:2 * LANES] = z[:, LANES:2 * LANES].astype(BF16)
        lora_ref[:, 2 * LANES:] = jax.nn.sigmoid(z[:, 2 * LANES:]).astype(BF16)

    prm = p_ref[...]
    mu_r, mu_k, mu_v = prm[0:1], prm[1:2], prm[2:3]
    cw0, cw1, cw2 = prm[3:4], prm[4:5], prm[5:6]
    w0, a0, k_k, k_a, r_k = prm[6:7], prm[7:8], prm[8:9], prm[9:10], prm[10:11]
    carry = [jnp.where(first, 0.0, carry_ref[j, q]) for q in range(N_CARRY)]
    w_cb, w_cc, w_ch, w_gc, w_r, w_k, w_v, w_gr = w_refs
    d = functools.partial(jnp.dot, preferred_element_type=F32)

    wup = wup_ref[...]
    w_pre = w0 + d(lora_ref[:, 0:LANES], wup[0:LANES])
    a_pre = a0 + d(lora_ref[:, LANES:2 * LANES], wup[LANES:2 * LANES])
    g = d(lora_ref[:, 2 * LANES:], wup[2 * LANES:])
    lw = LOG_DECAY_SCALE * jax.nn.sigmoid(w_pre)
    a = jax.nn.sigmoid(a_pre)

    p_r, p_k, p_v = xdot(w_r), xdot(w_k), xdot(w_v)
    r = p_r + mu_r * (prev1(p_r, carry[1]) - p_r)
    k = p_k + mu_k * (prev1(p_k, carry[2]) - p_k)
    v = p_v + mu_v * (prev1(p_v, carry[3]) - p_v)
    p_gr, p_cc, p_ch = xdot(w_gr), xdot(w_cc), xdot(w_ch)

    lo_head = lax.broadcasted_iota(jnp.int32, (tm, LANES), 1) < HEAD_SIZE

    def head_sum(q):
        tiles = []
        for t0 in range(0, tn, LANES):
            qt = q[:, t0:t0 + LANES]
            s_lo = jnp.sum(jnp.where(lo_head, qt, 0.0), axis=-1, keepdims=True)
            s_hi = jnp.sum(jnp.where(lo_head, 0.0, qt), axis=-1, keepdims=True)
            tiles.append(jnp.where(lo_head, s_lo, s_hi))
        return jnp.concatenate(tiles, axis=1)

    kraw = k * k_k
    ss = head_sum(kraw * kraw)
    kk = kraw / jnp.maximum(jnp.sqrt(ss), 1e-12)
    k_mod = k * (1.0 + (a - 1.0) * k_a)
    bonus = head_sum(r * k_mod * r_k) * v
    og = g * jax.nn.sigmoid(p_gr)

    u = p_cc * p_ch
    conv = cw2 * u + cw1 * prev1(u, carry[0]) + cw0 * prev2(u, carry[0])
    p_gc, p_cb = xdot(w_gc), xdot(w_cb)
    y_conv = (jax.nn.sigmoid(p_gc) * conv) * p_cb
    for q, val in enumerate((u, p_r, p_k, p_v)):
        carry_ref[j, q] = val[tm - SUBLANES:, :]

    r_ref[0] = r.astype(r_ref.dtype)
    lw_ref[0] = lw
    k_ref[0] = k_mod.astype(k_ref.dtype)
    v_ref[0] = v.astype(v_ref.dtype)
    kk_ref[0] = kk.astype(kk_ref.dtype)
    a_ref[0] = a.astype(a_ref.dtype)
    c1_ref[0] = (y_conv + og * bonus).astype(c1_ref.dtype)
    og_ref[0] = og.astype(og_ref.dtype)


def _mixer_in(x, w_groups, wl, wup, prm, mu_l, *, tm, tn):
    bsz, t, dm = x.shape
    nj = dm // tn
    grid = (bsz, t // tm, nj)
    out_sds = jax.ShapeDtypeStruct((bsz, t, dm), F32)
    out_bf = jax.ShapeDtypeStruct((bsz, t, dm), BF16)
    out_spec = pl.BlockSpec((1, tm, tn), lambda b, i, j: (b, i, j))
    return pl.pallas_call(
        functools.partial(_mixer_in_kernel, tm=tm, tn=tn),
        grid=grid,
        in_specs=[
            pl.BlockSpec((1, tm, dm), lambda b, i, j: (b, i, 0)),
            *[pl.BlockSpec((dm, tn), lambda b, i, j, o=o: (0, o + j)) for _, o in w_groups],
            pl.BlockSpec((dm, LORA_PAD), lambda b, i, j: (0, 0)),
            pl.BlockSpec((LORA_PAD, tn), lambda b, i, j: (0, j)),
            pl.BlockSpec((16, tn), lambda b, i, j: (0, j)),
            pl.BlockSpec((1, LORA_PAD), lambda b, i, j: (0, 0)),
        ],
        out_specs=[out_spec] * 8,
        out_shape=[out_bf, out_sds, out_bf, out_bf, out_bf, out_bf, out_bf, out_bf],
        scratch_shapes=[
            pltpu.VMEM((tm, dm), BF16),
            pltpu.VMEM((tm, LORA_PAD), BF16),
            pltpu.VMEM((nj, N_CARRY, SUBLANES, tn), F32),
            pltpu.VMEM((SUBLANES, LORA_PAD), F32),
        ],
        compiler_params=pltpu.CompilerParams(
            dimension_semantics=("arbitrary", "arbitrary", "arbitrary"),
            vmem_limit_bytes=VMEM_LIMIT_BYTES),
        name="mixer_in",
    )(x, *[w for w, _ in w_groups], wl, wup, prm, mu_l)


def _wkv_block(r, lw, k, v, kk, a, z0, consts, n_chunks):
    cs = consts
    c = CHUNK

    def stack(x):
        xb = x.astype(BF16)
        m0, m1 = (cs["m0"], cs["m1"]) if x.shape[1] == LANES else (cs["m0_2"], cs["m1_2"])
        return jnp.concatenate([xb * m0, xb * m1], axis=0)

    def bdiag(xw):
        xb = xw.astype(BF16)
        return jnp.concatenate([xb, xb], axis=0) * cs["bd_bf"]

    def each(fn, *lists):
        return [fn(*args) for args in zip(*lists)]

    cum = each(lambda x: _mm_exact_lhs(cs["tri"], x), lw)
    beta = each(lambda x, y: x * y, kk, a)
    e_inv = each(lambda x: jnp.exp(-x), cum)
    e_rem = each(lambda x: jnp.exp(x[c - 1:c, :] - x), cum)
    rt = each(lambda x, y: x * jnp.exp(y), r, cum)
    at = each(lambda x, y, z_: -x * jnp.exp(y - z_), kk, cum, lw)
    kt = each(lambda x, y: x * y, k, e_inv)
    bt = each(lambda x, y: x * y, beta, e_inv)
    kh = each(lambda x, y: x * y, k, e_rem)
    bh = each(lambda x, y: x * y, beta, e_rem)

    s = each(lambda a_, r_, b_, k_: _mm_nt(jnp.concatenate([a_, r_], axis=0),
                                           jnp.concatenate([stack(b_), stack(k_)], axis=0)),
             at, rt, bt, kt)
    lab = each(lambda x: jnp.where(cs["strict"], x[0:c, 0:2 * c], 0.0), s)
    urb = each(lambda x: jnp.where(cs["incl"], x[c:, 0:2 * c], 0.0), s)

    def twice(x):
        xb = x.astype(BF16)
        return jnp.concatenate([xb, xb], axis=0)

    lab2 = each(twice, lab)
    l8 = each(lambda x: jnp.where(cs["blk8"], x, 0.0), lab)
    l8_2 = each(lambda x, x2: _mm(x, x2 * cs["blk8_bd"]), l8, lab2)
    l8_4 = each(lambda x: _mm(x, bdiag(x)), l8_2)
    tw = each(lambda x, y: _mm(cs["eye"] + x, bdiag(cs["eye"] + y)), l8, l8_2)
    tw = each(lambda x, y: _mm(x, bdiag(cs["eye"] + y)), tw, l8_4)
    for join_bd in cs["joins_bd"]:
        half = each(lambda t_, x2: _mm(t_, x2 * join_bd), tw, lab2)
        tw = each(lambda t_, h_: t_ + _mm(h_, bdiag(t_)), tw, half)

    nv_uv = each(lambda x, v_: _mm(jnp.where(cs["strict_incl"], x[:, 2 * c:], 0.0), stack(v_)),
                 s, v)
    ah_d = each(lambda t_, a_, n_: _mm(t_, stack(jnp.concatenate([a_, n_[0:c]], axis=1))),
                tw, at, nv_uv)
    u2 = each(lambda u_, x: _mm(u_, stack(x)), urb, ah_d)
    rh = each(lambda r_, u_: r_ + u_[:, 0:LANES], rt, u2)
    e = each(lambda u_, w_: u_[:, LANES:] + w_[c:], u2, nv_uv)
    zeros = jnp.zeros((c, LANES), BF16)
    gf = each(lambda b_, k_, x, v_: _mm(
        jnp.concatenate([b_, k_], axis=0).T,
        jnp.concatenate([x.astype(BF16), jnp.concatenate([zeros, v_.astype(BF16)], axis=1)],
                        axis=0)), bh, kh, ah_d, v)
    gl = each(lambda x: jnp.where(cs["bd"], x[:, 0:LANES], 0.0), gf)
    f = each(lambda x: jnp.where(cs["bd"], x[:, LANES:], 0.0), gf)
    decay_col = each(lambda x: jnp.exp(x.T[:, c - 1:c]), cum)

    ys = [None] * len(r)
    states = list(z0)
    for ci in range(n_chunks):
        for p in range(len(states)):
            n = p * n_chunks + ci
            z = states[p]
            rz_gz = _mm(jnp.concatenate([rh[n], gl[n]], axis=0), z)
            ys[n] = rz_gz[0:c] + e[n]
            states[p] = decay_col[n] * z + rz_gz[c:] + f[n]
    return ys, states


def _wkv_consts():
    c = CHUNK
    assert c == HEAD_SIZE and 2 * c == LANES
    t = lax.broadcasted_iota(jnp.int32, (c, 2 * c), 0)
    i = jnp.bitwise_and(lax.broadcasted_iota(jnp.int32, (c, 2 * c), 1), c - 1)
    rr = lax.broadcasted_iota(jnp.int32, (c, c), 0)
    cc = lax.broadcasted_iota(jnp.int32, (c, c), 1)
    def head_lane_masks(width):
        lane = jnp.bitwise_and(lax.broadcasted_iota(jnp.int32, (c, width), 1), LANES - 1)
        return (jnp.where(lane < HEAD_SIZE, 1.0, 0.0).astype(BF16),
                jnp.where(lane < HEAD_SIZE, 0.0, 1.0).astype(BF16))

    m0, m1 = head_lane_masks(LANES)
    m0_2, m1_2 = head_lane_masks(2 * LANES)
    bd = _iota_shr((2 * c, 2 * c), 0, LOG2_HEAD) == _iota_shr((2 * c, 2 * c), 1, LOG2_HEAD)
    t2 = lax.broadcasted_iota(jnp.int32, (2 * c, 2 * c), 0)
    i2 = jnp.bitwise_and(lax.broadcasted_iota(jnp.int32, (2 * c, 2 * c), 1), c - 1)
    strict_incl = jnp.logical_or(jnp.logical_and(t2 < c, i2 < t2),
                                 jnp.logical_and(t2 >= c, i2 <= t2 - c))
    tt = jnp.bitwise_and(t2, c - 1)
    blk2 = [lax.shift_right_logical(tt, s) == lax.shift_right_logical(i2, s) for s in (3, 4, 5)]
    out2 = [jnp.logical_not(b) for b in blk2]
    joins = [jnp.logical_and(blk2[1], out2[0]), jnp.logical_and(blk2[2], out2[1]), out2[2]]

    def bd_mask(m):
        return jnp.where(jnp.logical_and(bd, m), 1.0, 0.0).astype(BF16)

    return dict(
        strict=i < t, incl=i <= t, strict_incl=strict_incl,
        eye=jnp.where(i == t, 1.0, 0.0).astype(F32),
        blk8=lax.shift_right_logical(t, 3) == lax.shift_right_logical(i, 3),
        blk8_bd=bd_mask(blk2[0]), joins_bd=[bd_mask(m) for m in joins],
        tri=jnp.where(cc <= rr, 1.0, 0.0).astype(BF16),
        m0=m0, m1=m1, m0_2=m0_2, m1_2=m1_2,
        bd=bd, bd_bf=jnp.where(bd, 1.0, 0.0).astype(BF16))


def _wkv_kernel(r_ref, lw_ref, k_ref, v_ref, kk_ref, a_ref, c1_ref, og_ref, gn_ref,
                x_ref, wo_ref, ln_ref, wgu_ref, wdn_ref, o_ref, obf_ref, wgu_bf_ref, wdn_bf_ref,
                z_ref, *, n_chunks, n_pairs, alpha):
    @pl.when(pl.program_id(1) == 0)
    def _():
        z_ref[...] = jnp.zeros_like(z_ref)

    wgu_bf_ref[...] = wgu_ref[...].astype(BF16)
    wdn_bf_ref[...] = wdn_ref[...].astype(BF16)

    def chunks(ref, dtype=F32):
        return [ref[0, ci * CHUNK:(ci + 1) * CHUNK, p * LANES:(p + 1) * LANES].astype(dtype)
                for p in range(n_pairs) for ci in range(n_chunks)]

    ys, states = _wkv_block(chunks(r_ref), chunks(lw_ref), chunks(k_ref), chunks(v_ref, BF16),
                            chunks(kk_ref), chunks(a_ref), [z_ref[p] for p in range(n_pairs)],
                            _wkv_consts(), n_chunks)
    ones = _head_ones(LANES)
    inv_n = 1.0 / HEAD_SIZE
    y = jnp.concatenate(ys, axis=0)
    mu = _mm(y, ones) * inv_n
    yc = y - mu
    var = _mm(yc * yc, ones) * inv_n
    yn = yc * lax.rsqrt(var + GN_EPS)
    tb = n_chunks * CHUNK
    merged = []
    for p in range(n_pairs):
        z_ref[p] = states[p]
        ls = slice(p * LANES, (p + 1) * LANES)
        yp = yn[p * tb:(p + 1) * tb] * gn_ref[0:1, ls] + gn_ref[1:2, ls]
        merged.append((c1_ref[0, :, ls].astype(F32)
                       + og_ref[0, :, ls].astype(F32) * yp).astype(BF16))
    h = alpha * x_ref[0] + jnp.dot(jnp.concatenate(merged, axis=1), wo_ref[...],
                                   preferred_element_type=F32)
    x1 = _layer_norm(h, ln_ref[0:1, :], ln_ref[1:2, :])
    o_ref[0] = x1
    obf_ref[0] = x1.astype(BF16)


def _layer_norm(h, g, b):
    mu = jnp.mean(h, axis=-1, keepdims=True)
    hc = h - mu
    var = jnp.mean(hc * hc, axis=-1, keepdims=True)
    return hc * lax.rsqrt(var + LN_EPS) * g + b


def _cast_spec(w, n_blocks_t, n_steps):
    rows, cols = w.shape
    nb = max(n for n in range(1, n_steps + 1)
             if rows % n == 0 and (rows // n) % BF16_ROWS == 0)
    return pl.BlockSpec((rows // nb, cols),
                        lambda b, i: (lax.div((b * n_blocks_t + i) * nb, n_steps), 0))


def _wkv(r, lw, k, v, kk, a, c1, og, gn, x, wo, ln, w_gu, w_down, *, alpha, tb):
    bsz, t, dm = r.shape
    n_pairs = dm // LANES
    n_blocks_t = t // tb
    spec = pl.BlockSpec((1, tb, dm), lambda b, i: (b, i, 0))
    row_spec = pl.BlockSpec((SUBLANES, dm), lambda b, i: (0, 0))
    cast_specs = [_cast_spec(w, n_blocks_t, bsz * n_blocks_t) for w in (w_gu, w_down)]
    return pl.pallas_call(
        functools.partial(_wkv_kernel, n_chunks=tb // CHUNK, n_pairs=n_pairs, alpha=alpha),
        grid=(bsz, n_blocks_t),
        in_specs=[spec] * 8 + [row_spec, spec, pl.BlockSpec((dm, dm), lambda b, i: (0, 0)),
                               row_spec] + cast_specs,
        out_specs=[spec, spec] + cast_specs,
        out_shape=[jax.ShapeDtypeStruct((bsz, t, dm), F32),
                   jax.ShapeDtypeStruct((bsz, t, dm), BF16),
                   jax.ShapeDtypeStruct(w_gu.shape, BF16),
                   jax.ShapeDtypeStruct(w_down.shape, BF16)],
        scratch_shapes=[pltpu.VMEM((n_pairs, LANES, LANES), F32)],
        compiler_params=pltpu.CompilerParams(
            dimension_semantics=("arbitrary", "arbitrary"),
            vmem_limit_bytes=VMEM_LIMIT_BYTES),
        name="wkv_out",
    )(r, lw, k, v, kk, a, c1, og, gn, x, wo, ln, w_gu, w_down)


def _ffn_kernel(xb_ref, x_ref, wg_ref, wu_ref, wd_ref, ln_ref, o_ref, acc_ref, *, alpha):
    f = pl.program_id(1)

    @pl.when(jnp.logical_and(pl.program_id(0) == 0, f == 0))
    def _():
        acc_ref[...] = jnp.zeros_like(acc_ref)

    xb = xb_ref[...]
    gate = jnp.dot(xb, wg_ref[...], preferred_element_type=F32)
    up = jnp.dot(xb, wu_ref[...], preferred_element_type=F32)
    hid = (gate * jax.nn.sigmoid(gate) * up).astype(BF16)
    part = jnp.dot(hid, wd_ref[...], preferred_element_type=F32)
    acc_ref[...] += part

    @pl.when(f == pl.num_programs(1) - 1)
    def _():
        h = alpha * x_ref[...] + acc_ref[...]
        o_ref[...] = _layer_norm(h, ln_ref[0:1, :], ln_ref[1:2, :])
        acc_ref[...] = jnp.zeros_like(acc_ref)


def _ffn(x_bf, x2d, w_gu, w_down, ln, *, alpha, tm, tf):
    m, dm = x2d.shape
    dff = w_down.shape[0]
    nf = dff // tf
    return pl.pallas_call(
        functools.partial(_ffn_kernel, alpha=alpha),
        grid=(m // tm, nf),
        in_specs=[
            pl.BlockSpec((tm, dm), lambda i, f: (i, 0)),
            pl.BlockSpec((tm, dm), lambda i, f: (i, 0)),
            pl.BlockSpec((dm, tf), lambda i, f: (0, f)),
            pl.BlockSpec((dm, tf), lambda i, f: (0, nf + f)),
            pl.BlockSpec((tf, dm), lambda i, f: (f, 0)),
            pl.BlockSpec((SUBLANES, dm), lambda i, f: (0, 0)),
        ],
        out_specs=pl.BlockSpec((tm, dm), lambda i, f: (i, 0)),
        out_shape=jax.ShapeDtypeStruct((m, dm), F32),
        scratch_shapes=[pltpu.VMEM((tm, dm), F32)],
        compiler_params=pltpu.CompilerParams(
            dimension_semantics=("arbitrary", "arbitrary"), vmem_limit_bytes=VMEM_LIMIT_BYTES),
        name="ffn",
    )(x_bf, x2d, w_gu, w_gu, w_down, ln)


def _pad_rows(a, n):
    return jnp.pad(a, ((0, n - a.shape[0]), (0, 0)))


def _pick(n, candidates):
    for c in candidates:
        if n % c == 0:
            return c
    raise ValueError(f"no tile size for extent {n}")


def kernel(x, w_in, shift_mu, conv_w, w0, w_up, a0, a_up, g_up, k_k, k_a, r_k,
           gn_g, gn_b, w_o, ln1_g, ln1_b, w_gu, w_down, ln2_g, ln2_b):
    bsz, t, dm = x.shape
    depth = w_in.shape[0]
    lw_n, la_n, lg_n = w_up.shape[1], a_up.shape[1], g_up.shape[1]
    assert dm % (2 * LANES) == 0 and t % CHUNK == 0
    assert lw_n <= LANES and la_n <= LANES and lg_n <= 2 * LANES
    assert w_in.shape[2] == 8 * dm + lw_n + la_n + lg_n
    alpha = (2.0 * depth) ** 0.25
    tn = 2 * LANES
    tm_in = _pick(t, (512, 256, 128, 64))
    tb = _pick(t, (128, 64))
    m = bsz * t
    tm = _pick(m, (512, 256, 128, 64))
    tf = _pick(w_down.shape[1], (512, 256, 128))

    for l in range(depth):
        wi = w_in[l]
        c0 = 6 * dm
        wi_bf = wi.astype(BF16)
        g0 = c0 + lw_n + la_n + lg_n
        nj = dm // tn
        w_groups = [(wi_bf, 0), (wi_bf, nj), (wi_bf, 2 * nj), (wi_bf[:, g0:g0 + dm], 0),
                    (wi_bf, 3 * nj), (wi_bf, 4 * nj), (wi_bf, 5 * nj), (wi_bf[:, g0 + dm:], 0)]

        def lane_pad(a, n):
            return jnp.pad(a, ((0, 0), (0, n - a.shape[1])))

        wl = jnp.concatenate([
            lane_pad(wi[:, c0:c0 + lw_n], LANES),
            lane_pad(wi[:, c0 + lw_n:c0 + lw_n + la_n], LANES),
            lane_pad(wi[:, c0 + lw_n + la_n:c0 + lw_n + la_n + lg_n], 2 * LANES)],
            axis=1).astype(BF16)
        mu = shift_mu[l]
        s0 = 3 * dm
        mu_l = jnp.concatenate([
            jnp.pad(mu[s0:s0 + lw_n], (0, LANES - lw_n)),
            jnp.pad(mu[s0 + lw_n:s0 + lw_n + la_n], (0, LANES - la_n)),
            jnp.pad(mu[s0 + lw_n + la_n:], (0, 2 * LANES - lg_n))])[None, :]
        wup = jnp.concatenate([_pad_rows(w_up[l], LANES), _pad_rows(a_up[l], LANES),
                               _pad_rows(g_up[l], 2 * LANES)], axis=0).astype(BF16)
        prm = _pad_rows(jnp.stack([
            mu[0:dm], mu[dm:2 * dm], mu[2 * dm:3 * dm],
            conv_w[l, 0], conv_w[l, 1], conv_w[l, 2],
            w0[l], a0[l], k_k[l], k_a[l], r_k[l].reshape(-1)]), 16)

        r, lw, k, v, kk, a, c1, og = _mixer_in(
            x, w_groups, wl, wup, prm, mu_l, tm=tm_in, tn=tn)
        gn = _pad_rows(jnp.stack([gn_g[l], gn_b[l]]), SUBLANES)
        ln1 = _pad_rows(jnp.stack([ln1_g[l], ln1_b[l]]), SUBLANES)
        x1, x1_bf, w_gu_bf, w_down_bf = _wkv(
            r, lw, k, v, kk, a, c1, og, gn, x, w_o[l].astype(BF16), ln1, w_gu[l], w_down[l],
            alpha=alpha, tb=tb)
        ln2 = _pad_rows(jnp.stack([ln2_g[l], ln2_b[l]]), SUBLANES)
        x = _ffn(x1_bf.reshape(m, dm), x1.reshape(m, dm), w_gu_bf, w_down_bf, ln2,
                 alpha=alpha, tm=tm, tf=tf).reshape(bsz, t, dm)
    return x
```

```python
import functools
import math

import jax
import jax.numpy as jnp
from jax import lax
from jax.experimental import pallas as pl
from jax.experimental.pallas import tpu as pltpu

HEAD_SIZE = 64
LOG2_HEAD = 6
LOG_DECAY_SCALE = -math.exp(-0.5)
LN_EPS = 1e-5
GN_EPS = 64e-5
LANES = 128
SUBLANES = 8
BF16_ROWS = 16
CHUNK = 64
VMEM_LIMIT_BYTES = 56 * 1024 * 1024

F32 = jnp.float32
BF16 = jnp.bfloat16


def _mm(a, b):
    return jnp.dot(a.astype(BF16), b.astype(BF16), preferred_element_type=F32)


def _mm_nt(a, b):
    return lax.dot_general(a.astype(BF16), b.astype(BF16), (((1,), (1,)), ((), ())),
                           preferred_element_type=F32)


def _split2(a):
    hi = a.astype(BF16)
    lo = (a - hi.astype(F32)).astype(BF16)
    return hi, lo


def _mm_exact_lhs(a_exact, b):
    d = functools.partial(jnp.dot, preferred_element_type=F32)
    h, l = _split2(b)
    return d(a_exact, h) + d(a_exact, l)


def _iota_shr(shape, dim, log2_div):
    return lax.shift_right_logical(lax.broadcasted_iota(jnp.int32, shape, dim), log2_div)


def _head_ones(n):
    r = _iota_shr((n, n), 0, LOG2_HEAD)
    c = _iota_shr((n, n), 1, LOG2_HEAD)
    return jnp.where(r == c, 1.0, 0.0).astype(BF16)


def _head_sum(q):
    rows, width = q.shape
    lo_head = lax.broadcasted_iota(jnp.int32, (rows, LANES), 1) < HEAD_SIZE
    tiles = []
    for t0 in range(0, width, LANES):
        qt = q[:, t0:t0 + LANES]
        s_lo = jnp.sum(jnp.where(lo_head, qt, 0.0), axis=-1, keepdims=True)
        s_hi = jnp.sum(jnp.where(lo_head, 0.0, qt), axis=-1, keepdims=True)
        tiles.append(jnp.where(lo_head, s_lo, s_hi))
    return tiles[0] if len(tiles) == 1 else jnp.concatenate(tiles, axis=1)


N_GROUPS = 8
N_CARRY = 4
LORA_PAD = 512


def _mixer_in_kernel(x_ref, *refs, tm, tn):
    w_refs = refs[:N_GROUPS]
    wl_ref, wup_ref, p_ref, mul_ref = refs[N_GROUPS:N_GROUPS + 4]
    r_ref, lw_ref, k_ref, v_ref, kk_ref, a_ref, c1_ref, og_ref = refs[N_GROUPS + 4:N_GROUPS + 12]
    xb_ref, lora_ref, carry_ref, carry_l_ref = refs[N_GROUPS + 12:]
    i = pl.program_id(1)
    j = pl.program_id(2)
    first = i == 0
    row = lax.broadcasted_iota(jnp.int32, (tm, tn), 0)

    @pl.when(j == 0)
    def _():
        xb_ref[...] = x_ref[0].astype(BF16)

    def xdot(w_ref):
        return jnp.dot(xb_ref[...], w_ref[...], preferred_element_type=F32)

    def prev1(p, c8):
        return jnp.where(row == 0, c8[SUBLANES - 1:SUBLANES, :], pltpu.roll(p, 1, 0))

    def prev2(p, c8):
        rolled = pltpu.roll(p, 2, 0)
        rolled = jnp.where(row == 1, c8[SUBLANES - 1:SUBLANES, :], rolled)
        return jnp.where(row == 0, c8[SUBLANES - 2:SUBLANES - 1, :], rolled)

    @pl.when(j == 0)
    def _():
        pl_ = xdot(wl_ref)
        cl = jnp.where(first, 0.0, carry_l_ref[...])
        rowl = lax.broadcasted_iota(jnp.int32, (tm, LORA_PAD), 0)
        prev = jnp.where(rowl == 0, cl[SUBLANES - 1:SUBLANES, :], pltpu.roll(pl_, 1, 0))
        carry_l_ref[...] = pl_[tm - SUBLANES:, :]
        z = pl_ + mul_ref[...] * (prev - pl_)
        lora_ref[:, 0:LANES] = jnp.tanh(z[:, 0:LANES]).astype(BF16)
        lora_ref[:, LANES:2 * LANES] = z[:, LANES:2 * LANES].astype(BF16)
        lora_ref[:, 2 * LANES:] = jax.nn.sigmoid(z[:, 2 * LANES:]).astype(BF16)

    prm = p_ref[...]
    mu_r, mu_k, mu_v = prm[0:1], prm[1:2], prm[2:3]
    cw0, cw1, cw2 = prm[3:4], prm[4:5], prm[5:6]
    w0, a0, k_k, k_a, r_k = prm[6:7], prm[7:8], prm[8:9], prm[9:10], prm[10:11]
    carry = [jnp.where(first, 0.0, carry_ref[j, q]) for q in range(N_CARRY)]
    w_cb, w_cc, w_ch, w_gc, w_r, w_k, w_v, w_gr = w_refs
    d = functools.partial(jnp.dot, preferred_element_type=F32)

    wup = wup_ref[...]
    w_pre = w0 + d(lora_ref[:, 0:LANES], wup[0:LANES])
    a_pre = a0 + d(lora_ref[:, LANES:2 * LANES], wup[LANES:2 * LANES])
    g = d(lora_ref[:, 2 * LANES:], wup[2 * LANES:])
    lw = LOG_DECAY_SCALE * jax.nn.sigmoid(w_pre)
    a = jax.nn.sigmoid(a_pre)

    p_r, p_k, p_v = xdot(w_r), xdot(w_k), xdot(w_v)
    r = p_r + mu_r * (prev1(p_r, carry[1]) - p_r)
    k = p_k + mu_k * (prev1(p_k, carry[2]) - p_k)
    v = p_v + mu_v * (prev1(p_v, carry[3]) - p_v)
    p_gr, p_cc, p_ch = xdot(w_gr), xdot(w_cc), xdot(w_ch)

    kraw = k * k_k
    ss = _head_sum(kraw * kraw)
    kk = kraw / jnp.maximum(jnp.sqrt(ss), 1e-12)
    k_mod = k * (1.0 + (a - 1.0) * k_a)
    bonus = _head_sum(r * k_mod * r_k) * v
    og = g * jax.nn.sigmoid(p_gr)

    u = p_cc * p_ch
    conv = cw2 * u + cw1 * prev1(u, carry[0]) + cw0 * prev2(u, carry[0])
    p_gc, p_cb = xdot(w_gc), xdot(w_cb)
    y_conv = (jax.nn.sigmoid(p_gc) * conv) * p_cb
    for q, val in enumerate((u, p_r, p_k, p_v)):
        carry_ref[j, q] = val[tm - SUBLANES:, :]

    r_ref[0] = r.astype(r_ref.dtype)
    lw_ref[0] = lw
    k_ref[0] = k_mod.astype(k_ref.dtype)
    v_ref[0] = v.astype(v_ref.dtype)
    kk_ref[0] = kk.astype(kk_ref.dtype)
    a_ref[0] = a.astype(a_ref.dtype)
    c1_ref[0] = (y_conv + og * bonus).astype(c1_ref.dtype)
    og_ref[0] = og.astype(og_ref.dtype)


def _mixer_in(x, w_groups, wl, wup, prm, mu_l, *, tm, tn):
    bsz, t, dm = x.shape
    nj = dm // tn
    grid = (bsz, t // tm, nj)
    out_sds = jax.ShapeDtypeStruct((bsz, t, dm), F32)
    out_bf = jax.ShapeDtypeStruct((bsz, t, dm), BF16)
    out_spec = pl.BlockSpec((1, tm, tn), lambda b, i, j: (b, i, j))
    return pl.pallas_call(
        functools.partial(_mixer_in_kernel, tm=tm, tn=tn),
        grid=grid,
        in_specs=[
            pl.BlockSpec((1, tm, dm), lambda b, i, j: (b, i, 0)),
            *[pl.BlockSpec((dm, tn), lambda b, i, j, o=o: (0, o + j)) for _, o in w_groups],
            pl.BlockSpec((dm, LORA_PAD), lambda b, i, j: (0, 0)),
            pl.BlockSpec((LORA_PAD, tn), lambda b, i, j: (0, j)),
            pl.BlockSpec((16, tn), lambda b, i, j: (0, j)),
            pl.BlockSpec((1, LORA_PAD), lambda b, i, j: (0, 0)),
        ],
        out_specs=[out_spec] * 8,
        out_shape=[out_bf, out_sds, out_bf, out_bf, out_bf, out_bf, out_bf, out_bf],
        scratch_shapes=[
            pltpu.VMEM((tm, dm), BF16),
            pltpu.VMEM((tm, LORA_PAD), BF16),
            pltpu.VMEM((nj, N_CARRY, SUBLANES, tn), F32),
            pltpu.VMEM((SUBLANES, LORA_PAD), F32),
        ],
        compiler_params=pltpu.CompilerParams(
            dimension_semantics=("arbitrary", "arbitrary", "arbitrary"),
            vmem_limit_bytes=VMEM_LIMIT_BYTES),
        name="mixer_in",
    )(x, *[w for w, _ in w_groups], wl, wup, prm, mu_l)


def _wkv_block(r, lw, k, v, kk, a, z0, consts, n_chunks):
    cs = consts
    c = CHUNK

    def stack(x):
        xb = x.astype(BF16)
        m0, m1 = (cs["m0"], cs["m1"]) if x.shape[1] == LANES else (cs["m0_2"], cs["m1_2"])
        return jnp.concatenate([xb * m0, xb * m1], axis=0)

    def bdiag(xw):
        xb = xw.astype(BF16)
        return jnp.concatenate([xb, xb], axis=0) * cs["bd_bf"]

    def each(fn, *lists):
        return [fn(*args) for args in zip(*lists)]

    cum = each(lambda x: _mm_exact_lhs(cs["tri"], x), lw)
    beta = each(lambda x, y: x * y, kk, a)
    e_inv = each(lambda x: jnp.exp(-x), cum)
    e_rem = each(lambda x: jnp.exp(x[c - 1:c, :] - x), cum)
    rt = each(lambda x, y: x * jnp.exp(y), r, cum)
    at = each(lambda x, y, z_: -x * jnp.exp(y - z_), kk, cum, lw)
    kt = each(lambda x, y: x * y, k, e_inv)
    bt = each(lambda x, y: x * y, beta, e_inv)
    kh = each(lambda x, y: x * y, k, e_rem)
    bh = each(lambda x, y: x * y, beta, e_rem)

    s = each(lambda a_, r_, b_, k_: _mm_nt(jnp.concatenate([a_, r_], axis=0),
                                           jnp.concatenate([stack(b_), stack(k_)], axis=0)),
             at, rt, bt, kt)
    lab = each(lambda x: jnp.where(cs["strict"], x[0:c, 0:2 * c], 0.0), s)
    urb = each(lambda x: jnp.where(cs["incl"], x[c:, 0:2 * c], 0.0), s)

    def twice(x):
        xb = x.astype(BF16)
        return jnp.concatenate([xb, xb], axis=0)

    lab2 = each(twice, lab)
    l8 = each(lambda x: jnp.where(cs["blk8"], x, 0.0), lab)
    l8_2 = each(lambda x, x2: _mm(x, x2 * cs["blk8_bd"]), l8, lab2)
    l8_4 = each(lambda x: _mm(x, bdiag(x)), l8_2)
    tw = each(lambda x, y: _mm(cs["eye"] + x, bdiag(cs["eye"] + y)), l8, l8_2)
    tw = each(lambda x, y: _mm(x, bdiag(cs["eye"] + y)), tw, l8_4)
    for join_bd in cs["joins_bd"]:
        half = each(lambda t_, x2: _mm(t_, x2 * join_bd), tw, lab2)
        tw = each(lambda t_, h_: t_ + _mm(h_, bdiag(t_)), tw, half)

    nv_uv = each(lambda x, v_: _mm(jnp.where(cs["strict_incl"], x[:, 2 * c:], 0.0), stack(v_)),
                 s, v)
    ah_d = each(lambda t_, a_, n_: _mm(t_, stack(jnp.concatenate([a_, n_[0:c]], axis=1))),
                tw, at, nv_uv)
    u2 = each(lambda u_, x: _mm(u_, stack(x)), urb, ah_d)
    rh = each(lambda r_, u_: r_ + u_[:, 0:LANES], rt, u2)
    e = each(lambda u_, w_: u_[:, LANES:] + w_[c:], u2, nv_uv)
    zeros = jnp.zeros((c, LANES), BF16)
    gf = each(lambda b_, k_, x, v_: _mm(
        jnp.concatenate([b_, k_], axis=0).T,
        jnp.concatenate([x.astype(BF16), jnp.concatenate([zeros, v_.astype(BF16)], axis=1)],
                        axis=0)), bh, kh, ah_d, v)
    gl = each(lambda x: jnp.where(cs["bd"], x[:, 0:LANES], 0.0), gf)
    f = each(lambda x: jnp.where(cs["bd"], x[:, LANES:], 0.0), gf)
    decay_col = each(lambda x: jnp.exp(x.T[:, c - 1:c]), cum)

    ys = [None] * len(r)
    states = list(z0)
    for ci in range(n_chunks):
        for p in range(len(states)):
            n = p * n_chunks + ci
            z = states[p]
            rz_gz = _mm(jnp.concatenate([rh[n], gl[n]], axis=0), z)
            ys[n] = rz_gz[0:c] + e[n]
            states[p] = decay_col[n] * z + rz_gz[c:] + f[n]
    return ys, states


def _wkv_consts():
    c = CHUNK
    assert c == HEAD_SIZE and 2 * c == LANES
    t = lax.broadcasted_iota(jnp.int32, (c, 2 * c), 0)
    i = jnp.bitwise_and(lax.broadcasted_iota(jnp.int32, (c, 2 * c), 1), c - 1)
    rr = lax.broadcasted_iota(jnp.int32, (c, c), 0)
    cc = lax.broadcasted_iota(jnp.int32, (c, c), 1)
    def head_lane_masks(width):
        lane = jnp.bitwise_and(lax.broadcasted_iota(jnp.int32, (c, width), 1), LANES - 1)
        return (jnp.where(lane < HEAD_SIZE, 1.0, 0.0).astype(BF16),
                jnp.where(lane < HEAD_SIZE, 0.0, 1.0).astype(BF16))

    m0, m1 = head_lane_masks(LANES)
    m0_2, m1_2 = head_lane_masks(2 * LANES)
    bd = _iota_shr((2 * c, 2 * c), 0, LOG2_HEAD) == _iota_shr((2 * c, 2 * c), 1, LOG2_HEAD)
    t2 = lax.broadcasted_iota(jnp.int32, (2 * c, 2 * c), 0)
    i2 = jnp.bitwise_and(lax.broadcasted_iota(jnp.int32, (2 * c, 2 * c), 1), c - 1)
    strict_incl = jnp.logical_or(jnp.logical_and(t2 < c, i2 < t2),
                                 jnp.logical_and(t2 >= c, i2 <= t2 - c))
    tt = jnp.bitwise_and(t2, c - 1)
    blk2 = [lax.shift_right_logical(tt, s) == lax.shift_right_logical(i2, s) for s in (3, 4, 5)]
    out2 = [jnp.logical_not(b) for b in blk2]
    joins = [jnp.logical_and(blk2[1], out2[0]), jnp.logical_and(blk2[2], out2[1]), out2[2]]

    def bd_mask(m):
        return jnp.where(jnp.logical_and(bd, m), 1.0, 0.0).astype(BF16)

    return dict(
        strict=i < t, incl=i <= t, strict_incl=strict_incl,
        eye=jnp.where(i == t, 1.0, 0.0).astype(F32),
        blk8=lax.shift_right_logical(t, 3) == lax.shift_right_logical(i, 3),
        blk8_bd=bd_mask(blk2[0]), joins_bd=[bd_mask(m) for m in joins],
        tri=jnp.where(cc <= rr, 1.0, 0.0).astype(BF16),
        m0=m0, m1=m1, m0_2=m0_2, m1_2=m1_2,
        bd=bd, bd_bf=jnp.where(bd, 1.0, 0.0).astype(BF16))


def _wkv_kernel(r_ref, lw_ref, k_ref, v_ref, kk_ref, a_ref, c1_ref, og_ref, gn_ref,
                x_ref, wo_ref, ln_ref, wgu_ref, wdn_ref, o_ref, obf_ref, wgu_bf_ref, wdn_bf_ref,
                z_ref, *, n_chunks, n_pairs, alpha):
    @pl.when(pl.program_id(1) == 0)
    def _():
        z_ref[...] = jnp.zeros_like(z_ref)

    wgu_bf_ref[...] = wgu_ref[...].astype(BF16)
    wdn_bf_ref[...] = wdn_ref[...].astype(BF16)

    def chunks(ref, dtype=F32):
        return [ref[0, ci * CHUNK:(ci + 1) * CHUNK, p * LANES:(p + 1) * LANES].astype(dtype)
                for p in range(n_pairs) for ci in range(n_chunks)]

    ys, states = _wkv_block(chunks(r_ref), chunks(lw_ref), chunks(k_ref), chunks(v_ref, BF16),
                            chunks(kk_ref), chunks(a_ref), [z_ref[p] for p in range(n_pairs)],
                            _wkv_consts(), n_chunks)
    ones = _head_ones(LANES)
    inv_n = 1.0 / HEAD_SIZE
    y = jnp.concatenate(ys, axis=0)
    mu = _mm(y, ones) * inv_n
    yc = y - mu
    var = _mm(yc * yc, ones) * inv_n
    yn = yc * lax.rsqrt(var + GN_EPS)
    tb = n_chunks * CHUNK
    merged = []
    for p in range(n_pairs):
        z_ref[p] = states[p]
        ls = slice(p * LANES, (p + 1) * LANES)
        yp = yn[p * tb:(p + 1) * tb] * gn_ref[0:1, ls] + gn_ref[1:2, ls]
        merged.append((c1_ref[0, :, ls].astype(F32)
                       + og_ref[0, :, ls].astype(F32) * yp).astype(BF16))
    h = alpha * x_ref[0] + jnp.dot(jnp.concatenate(merged, axis=1), wo_ref[...],
                                   preferred_element_type=F32)
    x1 = _layer_norm(h, ln_ref[0:1, :], ln_ref[1:2, :])
    o_ref[0] = x1
    obf_ref[0] = x1.astype(BF16)


def _layer_norm(h, g, b):
    mu = jnp.mean(h, axis=-1, keepdims=True)
    hc = h - mu
    var = jnp.mean(hc * hc, axis=-1, keepdims=True)
    return hc * lax.rsqrt(var + LN_EPS) * g + b


def _cast_spec(w, n_blocks_t, n_steps):
    rows, cols = w.shape
    nb = max(n for n in range(1, n_steps + 1)
             if rows % n == 0 and (rows // n) % BF16_ROWS == 0)
    return pl.BlockSpec((rows // nb, cols),
                        lambda b, i: (lax.div((b * n_blocks_t + i) * nb, n_steps), 0))


def _wkv(r, lw, k, v, kk, a, c1, og, gn, x, wo, ln, w_gu, w_down, *, alpha, tb):
    bsz, t, dm = r.shape
    n_pairs = dm // LANES
    n_blocks_t = t // tb
    spec = pl.BlockSpec((1, tb, dm), lambda b, i: (b, i, 0))
    row_spec = pl.BlockSpec((SUBLANES, dm), lambda b, i: (0, 0))
    cast_specs = [_cast_spec(w, n_blocks_t, bsz * n_blocks_t) for w in (w_gu, w_down)]
    return pl.pallas_call(
        functools.partial(_wkv_kernel, n_chunks=tb // CHUNK, n_pairs=n_pairs, alpha=alpha),
        grid=(bsz, n_blocks_t),
        in_specs=[spec] * 8 + [row_spec, spec, pl.BlockSpec((dm, dm), lambda b, i: (0, 0)),
                               row_spec] + cast_specs,
        out_specs=[spec, spec] + cast_specs,
        out_shape=[jax.ShapeDtypeStruct((bsz, t, dm), F32),
                   jax.ShapeDtypeStruct((bsz, t, dm), BF16),
                   jax.ShapeDtypeStruct(w_gu.shape, BF16),
                   jax.ShapeDtypeStruct(w_down.shape, BF16)],
        scratch_shapes=[pltpu.VMEM((n_pairs, LANES, LANES), F32)],
        compiler_params=pltpu.CompilerParams(
            dimension_semantics=("arbitrary", "arbitrary"),
            vmem_limit_bytes=VMEM_LIMIT_BYTES),
        name="wkv_out",
    )(r, lw, k, v, kk, a, c1, og, gn, x, wo, ln, w_gu, w_down)


def _ffn_kernel(xb_ref, x_ref, wg_ref, wu_ref, wd_ref, ln_ref, o_ref, acc_ref, *, alpha):
    f = pl.program_id(1)

    @pl.when(jnp.logical_and(pl.program_id(0) == 0, f == 0))
    def _():
        acc_ref[...] = jnp.zeros_like(acc_ref)

    xb = xb_ref[...]
    gate = jnp.dot(xb, wg_ref[...], preferred_element_type=F32)
    up = jnp.dot(xb, wu_ref[...], preferred_element_type=F32)
    hid = (gate * jax.nn.sigmoid(gate) * up).astype(BF16)
    part = jnp.dot(hid, wd_ref[...], preferred_element_type=F32)
    acc_ref[...] += part

    @pl.when(f == pl.num_programs(1) - 1)
    def _():
        h = alpha * x_ref[...] + acc_ref[...]
        o_ref[...] = _layer_norm(h, ln_ref[0:1, :], ln_ref[1:2, :])
        acc_ref[...] = jnp.zeros_like(acc_ref)


def _ffn(x_bf, x2d, w_gu, w_down, ln, *, alpha, tm, tf):
    m, dm = x2d.shape
    dff = w_down.shape[0]
    nf = dff // tf
    return pl.pallas_call(
        functools.partial(_ffn_kernel, alpha=alpha),
        grid=(m // tm, nf),
        in_specs=[
            pl.BlockSpec((tm, dm), lambda i, f: (i, 0)),
            pl.BlockSpec((tm, dm), lambda i, f: (i, 0)),
            pl.BlockSpec((dm, tf), lambda i, f: (0, f)),
            pl.BlockSpec((dm, tf), lambda i, f: (0, nf + f)),
            pl.BlockSpec((tf, dm), lambda i, f: (f, 0)),
            pl.BlockSpec((SUBLANES, dm), lambda i, f: (0, 0)),
        ],
        out_specs=pl.BlockSpec((tm, dm), lambda i, f: (i, 0)),
        out_shape=jax.ShapeDtypeStruct((m, dm), F32),
        scratch_shapes=[pltpu.VMEM((tm, dm), F32)],
        compiler_params=pltpu.CompilerParams(
            dimension_semantics=("arbitrary", "arbitrary"), vmem_limit_bytes=VMEM_LIMIT_BYTES),
        name="ffn",
    )(x_bf, x2d, w_gu, w_gu, w_down, ln)


def _pad_rows(a, n):
    return jnp.pad(a, ((0, n - a.shape[0]), (0, 0)))


def _pick(n, candidates):
    for c in candidates:
        if n % c == 0:
            return c
    raise ValueError(f"no tile size for extent {n}")


def kernel(x, w_in, shift_mu, conv_w, w0, w_up, a0, a_up, g_up, k_k, k_a, r_k,
           gn_g, gn_b, w_o, ln1_g, ln1_b, w_gu, w_down, ln2_g, ln2_b):
    bsz, t, dm = x.shape
    depth = w_in.shape[0]
    lw_n, la_n, lg_n = w_up.shape[1], a_up.shape[1], g_up.shape[1]
    assert dm % (2 * LANES) == 0 and t % CHUNK == 0
    assert lw_n <= LANES and la_n <= LANES and lg_n <= 2 * LANES
    assert w_in.shape[2] == 8 * dm + lw_n + la_n + lg_n
    alpha = (2.0 * depth) ** 0.25
    tn = 2 * LANES
    tm_in = _pick(t, (512, 256, 128, 64))
    tb = _pick(t, (128, 64))
    m = bsz * t
    tm = _pick(m, (512, 256, 128, 64))
    tf = _pick(w_down.shape[1], (512, 256, 128))

    for l in range(depth):
        wi = w_in[l]
        c0 = 6 * dm
        wi_bf = wi.astype(BF16)
        g0 = c0 + lw_n + la_n + lg_n
        nj = dm // tn
        w_groups = [(wi_bf, 0), (wi_bf, nj), (wi_bf, 2 * nj), (wi_bf[:, g0:g0 + dm], 0),
                    (wi_bf, 3 * nj), (wi_bf, 4 * nj), (wi_bf, 5 * nj), (wi_bf[:, g0 + dm:], 0)]

        def lane_pad(a, n):
            return jnp.pad(a, ((0, 0), (0, n - a.shape[1])))

        wl = jnp.concatenate([
            lane_pad(wi[:, c0:c0 + lw_n], LANES),
            lane_pad(wi[:, c0 + lw_n:c0 + lw_n + la_n], LANES),
            lane_pad(wi[:, c0 + lw_n + la_n:c0 + lw_n + la_n + lg_n], 2 * LANES)],
            axis=1).astype(BF16)
        mu = shift_mu[l]
        s0 = 3 * dm
        mu_l = jnp.concatenate([
            jnp.pad(mu[s0:s0 + lw_n], (0, LANES - lw_n)),
            jnp.pad(mu[s0 + lw_n:s0 + lw_n + la_n], (0, LANES - la_n)),
            jnp.pad(mu[s0 + lw_n + la_n:], (0, 2 * LANES - lg_n))])[None, :]
        wup = jnp.concatenate([_pad_rows(w_up[l], LANES), _pad_rows(a_up[l], LANES),
                               _pad_rows(g_up[l], 2 * LANES)], axis=0).astype(BF16)
        prm = _pad_rows(jnp.stack([
            mu[0:dm], mu[dm:2 * dm], mu[2 * dm:3 * dm],
            conv_w[l, 0], conv_w[l, 1], conv_w[l, 2],
            w0[l], a0[l], k_k[l], k_a[l], r_k[l].reshape(-1)]), 16)

        r, lw, k, v, kk, a, c1, og = _mixer_in(
            x, w_groups, wl, wup, prm, mu_l, tm=tm_in, tn=tn)
        gn = _pad_rows(jnp.stack([gn_g[l], gn_b[l]]), SUBLANES)
        ln1 = _pad_rows(jnp.stack([ln1_g[l], ln1_b[l]]), SUBLANES)
        x1, x1_bf, w_gu_bf, w_down_bf = _wkv(
            r, lw, k, v, kk, a, c1, og, gn, x, w_o[l].astype(BF16), ln1, w_gu[l], w_down[l],
            alpha=alpha, tb=tb)
        ln2 = _pad_rows(jnp.stack([ln2_g[l], ln2_b[l]]), SUBLANES)
        x = _ffn(x1_bf.reshape(m, dm), x1.reshape(m, dm), w_gu_bf, w_down_bf, ln2,
                 alpha=alpha, tm=tm, tf=tf).reshape(bsz, t, dm)
    return x
```

```python
import functools
import math

import jax
import jax.numpy as jnp
from jax import lax
from jax.experimental import pallas as pl
from jax.experimental.pallas import tpu as pltpu

HEAD_SIZE = 64
LOG2_HEAD = 6
LOG_DECAY_SCALE = -math.exp(-0.5)
LN_EPS = 1e-5
GN_EPS = 64e-5
LANES = 128
SUBLANES = 8
BF16_ROWS = 16
CHUNK = 64
VMEM_LIMIT_BYTES = 56 * 1024 * 1024

F32 = jnp.float32
BF16 = jnp.bfloat16


def _mm(a, b):
    return jnp.dot(a.astype(BF16), b.astype(BF16), preferred_element_type=F32)


def _mm_nt(a, b):
    return lax.dot_general(a.astype(BF16), b.astype(BF16), (((1,), (1,)), ((), ())),
                           preferred_element_type=F32)


def _split2(a):
    hi = a.astype(BF16)
    lo = (a - hi.astype(F32)).astype(BF16)
    return hi, lo


def _mm_exact_lhs(a_exact, b):
    d = functools.partial(jnp.dot, preferred_element_type=F32)
    h, l = _split2(b)
    return d(a_exact, h) + d(a_exact, l)


def _iota_shr(shape, dim, log2_div):
    return lax.shift_right_logical(lax.broadcasted_iota(jnp.int32, shape, dim), log2_div)


def _head_ones(n):
    r = _iota_shr((n, n), 0, LOG2_HEAD)
    c = _iota_shr((n, n), 1, LOG2_HEAD)
    return jnp.where(r == c, 1.0, 0.0).astype(BF16)


def _head_sum(q):
    rows, width = q.shape
    lo_head = lax.broadcasted_iota(jnp.int32, (rows, LANES), 1) < HEAD_SIZE
    tiles = []
    for t0 in range(0, width, LANES):
        qt = q[:, t0:t0 + LANES]
        s_lo = jnp.sum(jnp.where(lo_head, qt, 0.0), axis=-1, keepdims=True)
        s_hi = jnp.sum(jnp.where(lo_head, 0.0, qt), axis=-1, keepdims=True)
        tiles.append(jnp.where(lo_head, s_lo, s_hi))
    return tiles[0] if len(tiles) == 1 else jnp.concatenate(tiles, axis=1)


N_GROUPS = 8
N_CARRY = 4
LORA_PAD = 512
PARAM_ROWS = 16


def _mixer_in_kernel(x_ref, *refs, tm, tn):
    w_refs = refs[:N_GROUPS]
    wl_ref, wup_ref, p_ref, mul_ref = refs[N_GROUPS:N_GROUPS + 4]
    r_ref, lw_ref, k_ref, v_ref, kk_ref, a_ref, c1_ref, og_ref = refs[N_GROUPS + 4:N_GROUPS + 12]
    xb_ref, lora_ref, carry_ref, carry_l_ref = refs[N_GROUPS + 12:]
    i = pl.program_id(1)
    j = pl.program_id(2)
    first = i == 0
    row = lax.broadcasted_iota(jnp.int32, (tm, tn), 0)

    @pl.when(j == 0)
    def _():
        xb_ref[...] = x_ref[0].astype(BF16)

    def xdot(w_ref):
        return jnp.dot(xb_ref[...], w_ref[...], preferred_element_type=F32)

    def prev1(p, c8):
        return jnp.where(row == 0, c8[SUBLANES - 1:SUBLANES, :], pltpu.roll(p, 1, 0))

    def prev2(p, c8):
        rolled = pltpu.roll(p, 2, 0)
        rolled = jnp.where(row == 1, c8[SUBLANES - 1:SUBLANES, :], rolled)
        return jnp.where(row == 0, c8[SUBLANES - 2:SUBLANES - 1, :], rolled)

    @pl.when(j == 0)
    def _():
        pl_ = xdot(wl_ref)
        cl = jnp.where(first, 0.0, carry_l_ref[...])
        rowl = lax.broadcasted_iota(jnp.int32, (tm, LORA_PAD), 0)
        prev = jnp.where(rowl == 0, cl[SUBLANES - 1:SUBLANES, :], pltpu.roll(pl_, 1, 0))
        carry_l_ref[...] = pl_[tm - SUBLANES:, :]
        z = pl_ + mul_ref[...] * (prev - pl_)
        lora_ref[:, 0:LANES] = jnp.tanh(z[:, 0:LANES]).astype(BF16)
        lora_ref[:, LANES:2 * LANES] = z[:, LANES:2 * LANES].astype(BF16)
        lora_ref[:, 2 * LANES:] = jax.nn.sigmoid(z[:, 2 * LANES:]).astype(BF16)

    prm = p_ref[...]
    mu_r, mu_k, mu_v = prm[0:1], prm[1:2], prm[2:3]
    cw0, cw1, cw2 = prm[3:4], prm[4:5], prm[5:6]
    w0, a0, k_k, k_a, r_k = prm[6:7], prm[7:8], prm[8:9], prm[9:10], prm[10:11]
    carry = [jnp.where(first, 0.0, carry_ref[j, q]) for q in range(N_CARRY)]
    w_cb, w_cc, w_ch, w_gc, w_r, w_k, w_v, w_gr = w_refs
    d = functools.partial(jnp.dot, preferred_element_type=F32)

    wup = wup_ref[...]
    w_pre = w0 + d(lora_ref[:, 0:LANES], wup[0:LANES])
    a_pre = a0 + d(lora_ref[:, LANES:2 * LANES], wup[LANES:2 * LANES])
    g = d(lora_ref[:, 2 * LANES:], wup[2 * LANES:])
    lw = LOG_DECAY_SCALE * jax.nn.sigmoid(w_pre)
    a = jax.nn.sigmoid(a_pre)

    p_r, p_k, p_v = xdot(w_r), xdot(w_k), xdot(w_v)
    r = p_r + mu_r * (prev1(p_r, carry[1]) - p_r)
    k = p_k + mu_k * (prev1(p_k, carry[2]) - p_k)
    v = p_v + mu_v * (prev1(p_v, carry[3]) - p_v)
    p_gr, p_cc, p_ch = xdot(w_gr), xdot(w_cc), xdot(w_ch)

    kraw = k * k_k
    ss = _head_sum(kraw * kraw)
    kk = kraw / jnp.maximum(jnp.sqrt(ss), 1e-12)
    k_mod = k * (1.0 + (a - 1.0) * k_a)
    bonus = _head_sum(r * k_mod * r_k) * v
    og = g * jax.nn.sigmoid(p_gr)

    u = p_cc * p_ch
    conv = cw2 * u + cw1 * prev1(u, carry[0]) + cw0 * prev2(u, carry[0])
    p_gc, p_cb = xdot(w_gc), xdot(w_cb)
    y_conv = (jax.nn.sigmoid(p_gc) * conv) * p_cb
    for q, val in enumerate((u, p_r, p_k, p_v)):
        carry_ref[j, q] = val[tm - SUBLANES:, :]

    r_ref[0] = r.astype(r_ref.dtype)
    lw_ref[0] = lw
    k_ref[0] = k_mod.astype(k_ref.dtype)
    v_ref[0] = v.astype(v_ref.dtype)
    kk_ref[0] = kk.astype(kk_ref.dtype)
    a_ref[0] = a.astype(a_ref.dtype)
    c1_ref[0] = (y_conv + og * bonus).astype(c1_ref.dtype)
    og_ref[0] = og.astype(og_ref.dtype)


def _mixer_in(x, w_groups, wl, wup, prm, mu_l, *, tm, tn):
    bsz, t, dm = x.shape
    nj = dm // tn
    grid = (bsz, t // tm, nj)
    out_sds = jax.ShapeDtypeStruct((bsz, t, dm), F32)
    out_bf = jax.ShapeDtypeStruct((bsz, t, dm), BF16)
    out_spec = pl.BlockSpec((1, tm, tn), lambda b, i, j: (b, i, j))
    return pl.pallas_call(
        functools.partial(_mixer_in_kernel, tm=tm, tn=tn),
        grid=grid,
        in_specs=[
            pl.BlockSpec((1, tm, dm), lambda b, i, j: (b, i, 0)),
            *[pl.BlockSpec((dm, tn), lambda b, i, j, o=o: (0, o + j)) for _, o in w_groups],
            pl.BlockSpec((dm, LORA_PAD), lambda b, i, j: (0, 0)),
            pl.BlockSpec((LORA_PAD, tn), lambda b, i, j: (0, j)),
            pl.BlockSpec((PARAM_ROWS, tn), lambda b, i, j: (0, j)),
            pl.BlockSpec((1, LORA_PAD), lambda b, i, j: (0, 0)),
        ],
        out_specs=[out_spec] * 8,
        out_shape=[out_bf, out_sds, out_bf, out_bf, out_bf, out_bf, out_bf, out_bf],
        scratch_shapes=[
            pltpu.VMEM((tm, dm), BF16),
            pltpu.VMEM((tm, LORA_PAD), BF16),
            pltpu.VMEM((nj, N_CARRY, SUBLANES, tn), F32),
            pltpu.VMEM((SUBLANES, LORA_PAD), F32),
        ],
        compiler_params=pltpu.CompilerParams(
            dimension_semantics=("arbitrary", "arbitrary", "arbitrary"),
            vmem_limit_bytes=VMEM_LIMIT_BYTES),
        name="mixer_in",
    )(x, *[w for w, _ in w_groups], wl, wup, prm, mu_l)


def _wkv_block(r, lw, k, v, kk, a, z0, consts, n_chunks):
    cs = consts
    c = CHUNK

    def stack(x):
        xb = x.astype(BF16)
        m0, m1 = (cs["m0"], cs["m1"]) if x.shape[1] == LANES else (cs["m0_2"], cs["m1_2"])
        return jnp.concatenate([xb * m0, xb * m1], axis=0)

    def bdiag(xw):
        xb = xw.astype(BF16)
        return jnp.concatenate([xb, xb], axis=0) * cs["bd_bf"]

    def each(fn, *lists):
        return [fn(*args) for args in zip(*lists)]

    cum = each(lambda x: _mm_exact_lhs(cs["tri"], x), lw)
    beta = each(lambda x, y: x * y, kk, a)
    e_inv = each(lambda x: jnp.exp(-x), cum)
    e_rem = each(lambda x: jnp.exp(x[c - 1:c, :] - x), cum)
    rt = each(lambda x, y: x * jnp.exp(y), r, cum)
    at = each(lambda x, y, z_: -x * jnp.exp(y - z_), kk, cum, lw)
    kt = each(lambda x, y: x * y, k, e_inv)
    bt = each(lambda x, y: x * y, beta, e_inv)
    kh = each(lambda x, y: x * y, k, e_rem)
    bh = each(lambda x, y: x * y, beta, e_rem)

    s = each(lambda a_, r_, b_, k_: _mm_nt(jnp.concatenate([a_, r_], axis=0),
                                           jnp.concatenate([stack(b_), stack(k_)], axis=0)),
             at, rt, bt, kt)
    lab = each(lambda x: jnp.where(cs["strict"], x[0:c, 0:2 * c], 0.0), s)
    urb = each(lambda x: jnp.where(cs["incl"], x[c:, 0:2 * c], 0.0), s)

    def twice(x):
        xb = x.astype(BF16)
        return jnp.concatenate([xb, xb], axis=0)

    lab2 = each(twice, lab)
    l8 = each(lambda x: jnp.where(cs["blk8"], x, 0.0), lab)
    l8_2 = each(lambda x, x2: _mm(x, x2 * cs["blk8_bd"]), l8, lab2)
    l8_4 = each(lambda x: _mm(x, bdiag(x)), l8_2)
    tw = each(lambda x, y: _mm(cs["eye"] + x, bdiag(cs["eye"] + y)), l8, l8_2)
    tw = each(lambda x, y: _mm(x, bdiag(cs["eye"] + y)), tw, l8_4)
    for join_bd in cs["joins_bd"]:
        half = each(lambda t_, x2: _mm(t_, x2 * join_bd), tw, lab2)
        tw = each(lambda t_, h_: t_ + _mm(h_, bdiag(t_)), tw, half)

    nv_uv = each(lambda x, v_: _mm(jnp.where(cs["strict_incl"], x[:, 2 * c:], 0.0), stack(v_)),
                 s, v)
    ah_d = each(lambda t_, a_, n_: _mm(t_, stack(jnp.concatenate([a_, n_[0:c]], axis=1))),
                tw, at, nv_uv)
    u2 = each(lambda u_, x: _mm(u_, stack(x)), urb, ah_d)
    rh = each(lambda r_, u_: r_ + u_[:, 0:LANES], rt, u2)
    e = each(lambda u_, w_: u_[:, LANES:] + w_[c:], u2, nv_uv)
    zeros = jnp.zeros((c, LANES), BF16)
    gf = each(lambda b_, k_, x, v_: _mm(
        jnp.concatenate([b_, k_], axis=0).T,
        jnp.concatenate([x.astype(BF16), jnp.concatenate([zeros, v_.astype(BF16)], axis=1)],
                        axis=0)), bh, kh, ah_d, v)
    gl = each(lambda x: jnp.where(cs["bd"], x[:, 0:LANES], 0.0), gf)
    f = each(lambda x: jnp.where(cs["bd"], x[:, LANES:], 0.0), gf)
    decay_col = each(lambda x: jnp.exp(x.T[:, c - 1:c]), cum)

    ys = [None] * len(r)
    states = list(z0)
    for ci in range(n_chunks):
        for p in range(len(states)):
            n = p * n_chunks + ci
            z = states[p]
            rz_gz = _mm(jnp.concatenate([rh[n], gl[n]], axis=0), z)
            ys[n] = rz_gz[0:c] + e[n]
            states[p] = decay_col[n] * z + rz_gz[c:] + f[n]
    return ys, states


def _wkv_consts():
    c = CHUNK
    assert c == HEAD_SIZE and 2 * c == LANES
    t = lax.broadcasted_iota(jnp.int32, (c, 2 * c), 0)
    i = jnp.bitwise_and(lax.broadcasted_iota(jnp.int32, (c, 2 * c), 1), c - 1)
    rr = lax.broadcasted_iota(jnp.int32, (c, c), 0)
    cc = lax.broadcasted_iota(jnp.int32, (c, c), 1)
    def head_lane_masks(width):
        lane = jnp.bitwise_and(lax.broadcasted_iota(jnp.int32, (c, width), 1), LANES - 1)
        return (jnp.where(lane < HEAD_SIZE, 1.0, 0.0).astype(BF16),
                jnp.where(lane < HEAD_SIZE, 0.0, 1.0).astype(BF16))

    m0, m1 = head_lane_masks(LANES)
    m0_2, m1_2 = head_lane_masks(2 * LANES)
    bd = _iota_shr((2 * c, 2 * c), 0, LOG2_HEAD) == _iota_shr((2 * c, 2 * c), 1, LOG2_HEAD)
    t2 = lax.broadcasted_iota(jnp.int32, (2 * c, 2 * c), 0)
    i2 = jnp.bitwise_and(lax.broadcasted_iota(jnp.int32, (2 * c, 2 * c), 1), c - 1)
    strict_incl = jnp.logical_or(jnp.logical_and(t2 < c, i2 < t2),
                                 jnp.logical_and(t2 >= c, i2 <= t2 - c))
    tt = jnp.bitwise_and(t2, c - 1)
    blk2 = [lax.shift_right_logical(tt, s) == lax.shift_right_logical(i2, s) for s in (3, 4, 5)]
    out2 = [jnp.logical_not(b) for b in blk2]
    joins = [jnp.logical_and(blk2[1], out2[0]), jnp.logical_and(blk2[2], out2[1]), out2[2]]

    def bd_mask(m):
        return jnp.where(jnp.logical_and(bd, m), 1.0, 0.0).astype(BF16)

    return dict(
        strict=i < t, incl=i <= t, strict_incl=strict_incl,
        eye=jnp.where(i == t, 1.0, 0.0).astype(F32),
        blk8=lax.shift_right_logical(t, 3) == lax.shift_right_logical(i, 3),
        blk8_bd=bd_mask(blk2[0]), joins_bd=[bd_mask(m) for m in joins],
        tri=jnp.where(cc <= rr, 1.0, 0.0).astype(BF16),
        m0=m0, m1=m1, m0_2=m0_2, m1_2=m1_2,
        bd=bd, bd_bf=jnp.where(bd, 1.0, 0.0).astype(BF16))


def _wkv_kernel(r_ref, lw_ref, k_ref, v_ref, kk_ref, a_ref, c1_ref, og_ref, gn_ref,
                x_ref, wo_ref, ln_ref, wgu_ref, wdn_ref, o_ref, obf_ref, wgu_bf_ref, wdn_bf_ref,
                z_ref, *, n_chunks, n_pairs, alpha):
    @pl.when(pl.program_id(1) == 0)
    def _():
        z_ref[...] = jnp.zeros_like(z_ref)

    wgu_bf_ref[...] = wgu_ref[...].astype(BF16)
    wdn_bf_ref[...] = wdn_ref[...].astype(BF16)

    def chunks(ref, dtype=F32):
        return [ref[0, ci * CHUNK:(ci + 1) * CHUNK, p * LANES:(p + 1) * LANES].astype(dtype)
                for p in range(n_pairs) for ci in range(n_chunks)]

    ys, states = _wkv_block(chunks(r_ref), chunks(lw_ref), chunks(k_ref), chunks(v_ref, BF16),
                            chunks(kk_ref), chunks(a_ref), [z_ref[p] for p in range(n_pairs)],
                            _wkv_consts(), n_chunks)
    ones = _head_ones(LANES)
    inv_n = 1.0 / HEAD_SIZE
    y = jnp.concatenate(ys, axis=0)
    mu = _mm(y, ones) * inv_n
    yc = y - mu
    var = _mm(yc * yc, ones) * inv_n
    yn = yc * lax.rsqrt(var + GN_EPS)
    tb = n_chunks * CHUNK
    merged = []
    for p in range(n_pairs):
        z_ref[p] = states[p]
        ls = slice(p * LANES, (p + 1) * LANES)
        yp = yn[p * tb:(p + 1) * tb] * gn_ref[0:1, ls] + gn_ref[1:2, ls]
        merged.append((c1_ref[0, :, ls].astype(F32)
                       + og_ref[0, :, ls].astype(F32) * yp).astype(BF16))
    h = alpha * x_ref[0] + jnp.dot(jnp.concatenate(merged, axis=1), wo_ref[...],
                                   preferred_element_type=F32)
    x1 = _layer_norm(h, ln_ref[0:1, :], ln_ref[1:2, :])
    o_ref[0] = x1
    obf_ref[0] = x1.astype(BF16)


def _layer_norm(h, g, b):
    mu = jnp.mean(h, axis=-1, keepdims=True)
    hc = h - mu
    var = jnp.mean(hc * hc, axis=-1, keepdims=True)
    return hc * lax.rsqrt(var + LN_EPS) * g + b


def _cast_spec(w, n_blocks_t, n_steps):
    rows, cols = w.shape
    nb = max(n for n in range(1, n_steps + 1)
             if rows % n == 0 and (rows // n) % BF16_ROWS == 0)
    return pl.BlockSpec((rows // nb, cols),
                        lambda b, i: (lax.div((b * n_blocks_t + i) * nb, n_steps), 0))


def _wkv(r, lw, k, v, kk, a, c1, og, gn, x, wo, ln, w_gu, w_down, *, alpha, tb):
    bsz, t, dm = r.shape
    n_pairs = dm // LANES
    n_blocks_t = t // tb
    spec = pl.BlockSpec((1, tb, dm), lambda b, i: (b, i, 0))
    row_spec = pl.BlockSpec((SUBLANES, dm), lambda b, i: (0, 0))
    cast_specs = [_cast_spec(w, n_blocks_t, bsz * n_blocks_t) for w in (w_gu, w_down)]
    return pl.pallas_call(
        functools.partial(_wkv_kernel, n_chunks=tb // CHUNK, n_pairs=n_pairs, alpha=alpha),
        grid=(bsz, n_blocks_t),
        in_specs=[spec] * 8 + [row_spec, spec, pl.BlockSpec((dm, dm), lambda b, i: (0, 0)),
                               row_spec] + cast_specs,
        out_specs=[spec, spec] + cast_specs,
        out_shape=[jax.ShapeDtypeStruct((bsz, t, dm), F32),
                   jax.ShapeDtypeStruct((bsz, t, dm), BF16),
                   jax.ShapeDtypeStruct(w_gu.shape, BF16),
                   jax.ShapeDtypeStruct(w_down.shape, BF16)],
        scratch_shapes=[pltpu.VMEM((n_pairs, LANES, LANES), F32)],
        compiler_params=pltpu.CompilerParams(
            dimension_semantics=("arbitrary", "arbitrary"),
            vmem_limit_bytes=VMEM_LIMIT_BYTES),
        name="wkv_out",
    )(r, lw, k, v, kk, a, c1, og, gn, x, wo, ln, w_gu, w_down)


def _ffn_kernel(xb_ref, x_ref, wg_ref, wu_ref, wd_ref, ln_ref, o_ref, acc_ref, *, alpha):
    f = pl.program_id(1)

    @pl.when(jnp.logical_and(pl.program_id(0) == 0, f == 0))
    def _():
        acc_ref[...] = jnp.zeros_like(acc_ref)

    xb = xb_ref[...]
    gate = jnp.dot(xb, wg_ref[...], preferred_element_type=F32)
    up = jnp.dot(xb, wu_ref[...], preferred_element_type=F32)
    hid = (gate * jax.nn.sigmoid(gate) * up).astype(BF16)
    part = jnp.dot(hid, wd_ref[...], preferred_element_type=F32)
    acc_ref[...] += part

    @pl.when(f == pl.num_programs(1) - 1)
    def _():
        h = alpha * x_ref[...] + acc_ref[...]
        o_ref[...] = _layer_norm(h, ln_ref[0:1, :], ln_ref[1:2, :])
        acc_ref[...] = jnp.zeros_like(acc_ref)


def _ffn(x_bf, x2d, w_gu, w_down, ln, *, alpha, tm, tf):
    m, dm = x2d.shape
    dff = w_down.shape[0]
    nf = dff // tf
    return pl.pallas_call(
        functools.partial(_ffn_kernel, alpha=alpha),
        grid=(m // tm, nf),
        in_specs=[
            pl.BlockSpec((tm, dm), lambda i, f: (i, 0)),
            pl.BlockSpec((tm, dm), lambda i, f: (i, 0)),
            pl.BlockSpec((dm, tf), lambda i, f: (0, f)),
            pl.BlockSpec((dm, tf), lambda i, f: (0, nf + f)),
            pl.BlockSpec((tf, dm), lambda i, f: (f, 0)),
            pl.BlockSpec((SUBLANES, dm), lambda i, f: (0, 0)),
        ],
        out_specs=pl.BlockSpec((tm, dm), lambda i, f: (i, 0)),
        out_shape=jax.ShapeDtypeStruct((m, dm), F32),
        scratch_shapes=[pltpu.VMEM((tm, dm), F32)],
        compiler_params=pltpu.CompilerParams(
            dimension_semantics=("arbitrary", "arbitrary"), vmem_limit_bytes=VMEM_LIMIT_BYTES),
        name="ffn",
    )(x_bf, x2d, w_gu, w_gu, w_down, ln)


def _pad_rows(a, n):
    return jnp.pad(a, ((0, n - a.shape[0]), (0, 0)))


def _pick(n, candidates):
    for c in candidates:
        if n % c == 0:
            return c
    raise ValueError(f"no tile size for extent {n}")


def kernel(x, w_in, shift_mu, conv_w, w0, w_up, a0, a_up, g_up, k_k, k_a, r_k,
           gn_g, gn_b, w_o, ln1_g, ln1_b, w_gu, w_down, ln2_g, ln2_b):
    bsz, t, dm = x.shape
    depth = w_in.shape[0]
    lw_n, la_n, lg_n = w_up.shape[1], a_up.shape[1], g_up.shape[1]
    assert dm % (2 * LANES) == 0 and t % CHUNK == 0
    assert lw_n <= LANES and la_n <= LANES and lg_n <= 2 * LANES
    assert w_in.shape[2] == 8 * dm + lw_n + la_n + lg_n
    alpha = (2.0 * depth) ** 0.25
    tn = 2 * LANES
    tm_in = _pick(t, (512, 256, 128, 64))
    tb = _pick(t, (128, 64))
    m = bsz * t
    tm = _pick(m, (512, 256, 128, 64))
    tf = _pick(w_down.shape[1], (512, 256, 128))

    for l in range(depth):
        wi = w_in[l]
        c0 = 6 * dm
        wi_bf = wi.astype(BF16)
        g0 = c0 + lw_n + la_n + lg_n
        nj = dm // tn
        w_groups = [(wi_bf, 0), (wi_bf, nj), (wi_bf, 2 * nj), (wi_bf[:, g0:g0 + dm], 0),
                    (wi_bf, 3 * nj), (wi_bf, 4 * nj), (wi_bf, 5 * nj), (wi_bf[:, g0 + dm:], 0)]

        def lane_pad(a, n):
            return jnp.pad(a, ((0, 0), (0, n - a.shape[1])))

        wl = jnp.concatenate([
            lane_pad(wi[:, c0:c0 + lw_n], LANES),
            lane_pad(wi[:, c0 + lw_n:c0 + lw_n + la_n], LANES),
            lane_pad(wi[:, c0 + lw_n + la_n:c0 + lw_n + la_n + lg_n], 2 * LANES)],
            axis=1).astype(BF16)
        mu = shift_mu[l]
        s0 = 3 * dm
        mu_l = jnp.concatenate([
            jnp.pad(mu[s0:s0 + lw_n], (0, LANES - lw_n)),
            jnp.pad(mu[s0 + lw_n:s0 + lw_n + la_n], (0, LANES - la_n)),
            jnp.pad(mu[s0 + lw_n + la_n:], (0, 2 * LANES - lg_n))])[None, :]
        wup = jnp.concatenate([_pad_rows(w_up[l], LANES), _pad_rows(a_up[l], LANES),
                               _pad_rows(g_up[l], 2 * LANES)], axis=0).astype(BF16)
        prm = _pad_rows(jnp.stack([
            mu[0:dm], mu[dm:2 * dm], mu[2 * dm:3 * dm],
            conv_w[l, 0], conv_w[l, 1], conv_w[l, 2],
            w0[l], a0[l], k_k[l], k_a[l], r_k[l].reshape(-1)]), PARAM_ROWS)

        r, lw, k, v, kk, a, c1, og = _mixer_in(
            x, w_groups, wl, wup, prm, mu_l, tm=tm_in, tn=tn)
        gn = _pad_rows(jnp.stack([gn_g[l], gn_b[l]]), SUBLANES)
        ln1 = _pad_rows(jnp.stack([ln1_g[l], ln1_b[l]]), SUBLANES)
        x1, x1_bf, w_gu_bf, w_down_bf = _wkv(
            r, lw, k, v, kk, a, c1, og, gn, x, w_o[l].astype(BF16), ln1, w_gu[l], w_down[l],
            alpha=alpha, tb=tb)
        ln2 = _pad_rows(jnp.stack([ln2_g[l], ln2_b[l]]), SUBLANES)
        x = _ffn(x1_bf.reshape(m, dm), x1.reshape(m, dm), w_gu_bf, w_down_bf, ln2,
                 alpha=alpha, tm=tm, tf=tf).reshape(bsz, t, dm)
    return x
```

```python
import functools
import math

import jax
import jax.numpy as jnp
from jax import lax
from jax.experimental import pallas as pl
from jax.experimental.pallas import tpu as pltpu

HEAD_SIZE = 64
LOG2_HEAD = 6
LOG_DECAY_SCALE = -math.exp(-0.5)
LN_EPS = 1e-5
GN_EPS = 64e-5
LANES = 128
SUBLANES = 8
BF16_ROWS = 16
CHUNK = 64
VMEM_LIMIT_BYTES = 56 * 1024 * 1024

F32 = jnp.float32
BF16 = jnp.bfloat16


def _mm(a, b):
    return jnp.dot(a.astype(BF16), b.astype(BF16), preferred_element_type=F32)


def _mm_nt(a, b):
    return lax.dot_general(a.astype(BF16), b.astype(BF16), (((1,), (1,)), ((), ())),
                           preferred_element_type=F32)


def _split2(a):
    hi = a.astype(BF16)
    lo = (a - hi.astype(F32)).astype(BF16)
    return hi, lo


def _mm_exact_lhs(a_exact, b):
    d = functools.partial(jnp.dot, preferred_element_type=F32)
    h, l = _split2(b)
    return d(a_exact, h) + d(a_exact, l)


def _iota_shr(shape, dim, log2_div):
    return lax.shift_right_logical(lax.broadcasted_iota(jnp.int32, shape, dim), log2_div)


def _head_ones(n):
    r = _iota_shr((n, n), 0, LOG2_HEAD)
    c = _iota_shr((n, n), 1, LOG2_HEAD)
    return jnp.where(r == c, 1.0, 0.0).astype(BF16)


def _head_sum(q):
    rows, width = q.shape
    lo_head = lax.broadcasted_iota(jnp.int32, (rows, LANES), 1) < HEAD_SIZE
    tiles = []
    for t0 in range(0, width, LANES):
        qt = q[:, t0:t0 + LANES]
        s_lo = jnp.sum(jnp.where(lo_head, qt, 0.0), axis=-1, keepdims=True)
        s_hi = jnp.sum(jnp.where(lo_head, 0.0, qt), axis=-1, keepdims=True)
        tiles.append(jnp.where(lo_head, s_lo, s_hi))
    return tiles[0] if len(tiles) == 1 else jnp.concatenate(tiles, axis=1)


N_GROUPS = 8
N_CARRY = 4
LORA_PAD = 512
PARAM_ROWS = 16


def _mixer_in_kernel(x_ref, *refs, tm, tn):
    w_refs = refs[:N_GROUPS]
    wl_ref, wup_ref, p_ref, mul_ref = refs[N_GROUPS:N_GROUPS + 4]
    r_ref, lw_ref, k_ref, v_ref, kk_ref, a_ref, c1_ref, og_ref = refs[N_GROUPS + 4:N_GROUPS + 12]
    xb_ref, lora_ref, carry_ref, carry_l_ref = refs[N_GROUPS + 12:]
    i = pl.program_id(1)
    j = pl.program_id(2)
    first = i == 0
    row = lax.broadcasted_iota(jnp.int32, (tm, tn), 0)

    @pl.when(j == 0)
    def _():
        xb_ref[...] = x_ref[0].astype(BF16)

    def xdot(w_ref):
        return jnp.dot(xb_ref[...], w_ref[...], preferred_element_type=F32)

    def prev1(p, c8):
        return jnp.where(row == 0, c8[SUBLANES - 1:SUBLANES, :], pltpu.roll(p, 1, 0))

    def prev2(p, c8):
        rolled = pltpu.roll(p, 2, 0)
        rolled = jnp.where(row == 1, c8[SUBLANES - 1:SUBLANES, :], rolled)
        return jnp.where(row == 0, c8[SUBLANES - 2:SUBLANES - 1, :], rolled)

    @pl.when(j == 0)
    def _():
        pl_ = xdot(wl_ref)
        cl = jnp.where(first, 0.0, carry_l_ref[...])
        rowl = lax.broadcasted_iota(jnp.int32, (tm, LORA_PAD), 0)
        prev = jnp.where(rowl == 0, cl[SUBLANES - 1:SUBLANES, :], pltpu.roll(pl_, 1, 0))
        carry_l_ref[...] = pl_[tm - SUBLANES:, :]
        z = pl_ + mul_ref[...] * (prev - pl_)
        lora_ref[:, 0:LANES] = jnp.tanh(z[:, 0:LANES]).astype(BF16)
        lora_ref[:, LANES:2 * LANES] = z[:, LANES:2 * LANES].astype(BF16)
        lora_ref[:, 2 * LANES:] = jax.nn.sigmoid(z[:, 2 * LANES:]).astype(BF16)

    prm = p_ref[...]
    mu_r, mu_k, mu_v = prm[0:1], prm[1:2], prm[2:3]
    cw0, cw1, cw2 = prm[3:4], prm[4:5], prm[5:6]
    w0, a0, k_k, k_a, r_k = prm[6:7], prm[7:8], prm[8:9], prm[9:10], prm[10:11]
    carry = [jnp.where(first, 0.0, carry_ref[j, q]) for q in range(N_CARRY)]
    w_cb, w_cc, w_ch, w_gc, w_r, w_k, w_v, w_gr = w_refs
    d = functools.partial(jnp.dot, preferred_element_type=F32)

    wup = wup_ref[...]
    w_pre = w0 + d(lora_ref[:, 0:LANES], wup[0:LANES])
    a_pre = a0 + d(lora_ref[:, LANES:2 * LANES], wup[LANES:2 * LANES])
    g = d(lora_ref[:, 2 * LANES:], wup[2 * LANES:])
    lw = LOG_DECAY_SCALE * jax.nn.sigmoid(w_pre)
    a = jax.nn.sigmoid(a_pre)

    p_r, p_k, p_v = xdot(w_r), xdot(w_k), xdot(w_v)
    r = p_r + mu_r * (prev1(p_r, carry[1]) - p_r)
    k = p_k + mu_k * (prev1(p_k, carry[2]) - p_k)
    v = p_v + mu_v * (prev1(p_v, carry[3]) - p_v)
    p_gr, p_cc, p_ch = xdot(w_gr), xdot(w_cc), xdot(w_ch)

    kraw = k * k_k
    ss = _head_sum(kraw * kraw)
    kk = kraw / jnp.maximum(jnp.sqrt(ss), 1e-12)
    k_mod = k * (1.0 + (a - 1.0) * k_a)
    bonus = _head_sum(r * k_mod * r_k) * v
    og = g * jax.nn.sigmoid(p_gr)

    u = p_cc * p_ch
    conv = cw2 * u + cw1 * prev1(u, carry[0]) + cw0 * prev2(u, carry[0])
    p_gc, p_cb = xdot(w_gc), xdot(w_cb)
    y_conv = (jax.nn.sigmoid(p_gc) * conv) * p_cb
    for q, val in enumerate((u, p_r, p_k, p_v)):
        carry_ref[j, q] = val[tm - SUBLANES:, :]

    r_ref[0] = r.astype(r_ref.dtype)
    lw_ref[0] = lw
    k_ref[0] = k_mod.astype(k_ref.dtype)
    v_ref[0] = v.astype(v_ref.dtype)
    kk_ref[0] = kk.astype(kk_ref.dtype)
    a_ref[0] = a.astype(a_ref.dtype)
    c1_ref[0] = (y_conv + og * bonus).astype(c1_ref.dtype)
    og_ref[0] = og.astype(og_ref.dtype)


def _mixer_in(x, w_groups, wl, wup, prm, mu_l, *, tm, tn):
    bsz, t, dm = x.shape
    nj = dm // tn
    grid = (bsz, t // tm, nj)
    out_sds = jax.ShapeDtypeStruct((bsz, t, dm), F32)
    out_bf = jax.ShapeDtypeStruct((bsz, t, dm), BF16)
    out_spec = pl.BlockSpec((1, tm, tn), lambda b, i, j: (b, i, j))
    return pl.pallas_call(
        functools.partial(_mixer_in_kernel, tm=tm, tn=tn),
        grid=grid,
        in_specs=[
            pl.BlockSpec((1, tm, dm), lambda b, i, j: (b, i, 0)),
            *[pl.BlockSpec((dm, tn), lambda b, i, j, o=o: (0, o + j)) for _, o in w_groups],
            pl.BlockSpec((dm, LORA_PAD), lambda b, i, j: (0, 0)),
            pl.BlockSpec((LORA_PAD, tn), lambda b, i, j: (0, j)),
            pl.BlockSpec((PARAM_ROWS, tn), lambda b, i, j: (0, j)),
            pl.BlockSpec((1, LORA_PAD), lambda b, i, j: (0, 0)),
        ],
        out_specs=[out_spec] * 8,
        out_shape=[out_bf, out_sds, out_bf, out_bf, out_bf, out_bf, out_bf, out_bf],
        scratch_shapes=[
            pltpu.VMEM((tm, dm), BF16),
            pltpu.VMEM((tm, LORA_PAD), BF16),
            pltpu.VMEM((nj, N_CARRY, SUBLANES, tn), F32),
            pltpu.VMEM((SUBLANES, LORA_PAD), F32),
        ],
        compiler_params=pltpu.CompilerParams(
            dimension_semantics=("arbitrary", "arbitrary", "arbitrary"),
            vmem_limit_bytes=VMEM_LIMIT_BYTES),
        name="mixer_in",
    )(x, *[w for w, _ in w_groups], wl, wup, prm, mu_l)


def _wkv_block(r, lw, k, v, kk, a, z0, consts, n_chunks):
    cs = consts
    c = CHUNK

    def stack(x):
        xb = x.astype(BF16)
        m0, m1 = (cs["m0"], cs["m1"]) if x.shape[1] == LANES else (cs["m0_2"], cs["m1_2"])
        return jnp.concatenate([xb * m0, xb * m1], axis=0)

    def bdiag(xw):
        xb = xw.astype(BF16)
        return jnp.concatenate([xb, xb], axis=0) * cs["bd_bf"]

    def each(fn, *lists):
        return [fn(*args) for args in zip(*lists)]

    cum = each(lambda x: _mm_exact_lhs(cs["tri"], x), lw)
    beta = each(lambda x, y: x * y, kk, a)
    e_inv = each(lambda x: jnp.exp(-x), cum)
    e_rem = each(lambda x: jnp.exp(x[c - 1:c, :] - x), cum)
    rt = each(lambda x, y: x * jnp.exp(y), r, cum)
    at = each(lambda x, y, z_: -x * jnp.exp(y - z_), kk, cum, lw)
    kt = each(lambda x, y: x * y, k, e_inv)
    bt = each(lambda x, y: x * y, beta, e_inv)
    kh = each(lambda x, y: x * y, k, e_rem)
    bh = each(lambda x, y: x * y, beta, e_rem)

    s = each(lambda a_, r_, b_, k_: _mm_nt(jnp.concatenate([a_, r_], axis=0),
                                           jnp.concatenate([stack(b_), stack(k_)], axis=0)),
             at, rt, bt, kt)
    lab = each(lambda x: jnp.where(cs["strict"], x[0:c, 0:2 * c], 0.0), s)
    urb = each(lambda x: jnp.where(cs["incl"], x[c:, 0:2 * c], 0.0), s)

    def twice(x):
        xb = x.astype(BF16)
        return jnp.concatenate([xb, xb], axis=0)

    lab2 = each(twice, lab)
    l8 = each(lambda x: jnp.where(cs["blk8"], x, 0.0), lab)
    l8_2 = each(lambda x, x2: _mm(x, x2 * cs["blk8_bd"]), l8, lab2)
    l8_4 = each(lambda x: _mm(x, bdiag(x)), l8_2)
    tw = each(lambda x, y: _mm(cs["eye"] + x, bdiag(cs["eye"] + y)), l8, l8_2)
    tw = each(lambda x, y: _mm(x, bdiag(cs["eye"] + y)), tw, l8_4)
    for join_bd in cs["joins_bd"]:
        half = each(lambda t_, x2: _mm(t_, x2 * join_bd), tw, lab2)
        tw = each(lambda t_, h_: t_ + _mm(h_, bdiag(t_)), tw, half)

    nv_uv = each(lambda x, v_: _mm(jnp.where(cs["strict_incl"], x[:, 2 * c:], 0.0), stack(v_)),
                 s, v)
    ah_d = each(lambda t_, a_, n_: _mm(t_, stack(jnp.concatenate([a_, n_[0:c]], axis=1))),
                tw, at, nv_uv)
    u2 = each(lambda u_, x: _mm(u_, stack(x)), urb, ah_d)
    rh = each(lambda r_, u_: r_ + u_[:, 0:LANES], rt, u2)
    e = each(lambda u_, w_: u_[:, LANES:] + w_[c:], u2, nv_uv)
    zeros = jnp.zeros((c, LANES), BF16)
    gf = each(lambda b_, k_, x, v_: _mm(
        jnp.concatenate([b_, k_], axis=0).T,
        jnp.concatenate([x.astype(BF16), jnp.concatenate([zeros, v_.astype(BF16)], axis=1)],
                        axis=0)), bh, kh, ah_d, v)
    gl = each(lambda x: jnp.where(cs["bd"], x[:, 0:LANES], 0.0), gf)
    f = each(lambda x: jnp.where(cs["bd"], x[:, LANES:], 0.0), gf)
    decay_col = each(lambda x: jnp.exp(x.T[:, c - 1:c]), cum)

    ys = [None] * len(r)
    states = list(z0)
    for ci in range(n_chunks):
        for p in range(len(states)):
            n = p * n_chunks + ci
            z = states[p]
            rz_gz = _mm(jnp.concatenate([rh[n], gl[n]], axis=0), z)
            ys[n] = rz_gz[0:c] + e[n]
            states[p] = decay_col[n] * z + rz_gz[c:] + f[n]
    return ys, states


def _wkv_consts():
    c = CHUNK
    assert c == HEAD_SIZE and 2 * c == LANES
    t = lax.broadcasted_iota(jnp.int32, (c, 2 * c), 0)
    i = jnp.bitwise_and(lax.broadcasted_iota(jnp.int32, (c, 2 * c), 1), c - 1)
    rr = lax.broadcasted_iota(jnp.int32, (c, c), 0)
    cc = lax.broadcasted_iota(jnp.int32, (c, c), 1)
    def head_lane_masks(width):
        lane = jnp.bitwise_and(lax.broadcasted_iota(jnp.int32, (c, width), 1), LANES - 1)
        return (jnp.where(lane < HEAD_SIZE, 1.0, 0.0).astype(BF16),
                jnp.where(lane < HEAD_SIZE, 0.0, 1.0).astype(BF16))

    m0, m1 = head_lane_masks(LANES)
    m0_2, m1_2 = head_lane_masks(2 * LANES)
    bd = _iota_shr((2 * c, 2 * c), 0, LOG2_HEAD) == _iota_shr((2 * c, 2 * c), 1, LOG2_HEAD)
    t2 = lax.broadcasted_iota(jnp.int32, (2 * c, 2 * c), 0)
    i2 = jnp.bitwise_and(lax.broadcasted_iota(jnp.int32, (2 * c, 2 * c), 1), c - 1)
    strict_incl = jnp.logical_or(jnp.logical_and(t2 < c, i2 < t2),
                                 jnp.logical_and(t2 >= c, i2 <= t2 - c))
    tt = jnp.bitwise_and(t2, c - 1)
    blk2 = [lax.shift_right_logical(tt, s) == lax.shift_right_logical(i2, s) for s in (3, 4, 5)]
    out2 = [jnp.logical_not(b) for b in blk2]
    joins = [jnp.logical_and(blk2[1], out2[0]), jnp.logical_and(blk2[2], out2[1]), out2[2]]

    def bd_mask(m):
        return jnp.where(jnp.logical_and(bd, m), 1.0, 0.0).astype(BF16)

    return dict(
        strict=i < t, incl=i <= t, strict_incl=strict_incl,
        eye=jnp.where(i == t, 1.0, 0.0).astype(F32),
        blk8=lax.shift_right_logical(t, 3) == lax.shift_right_logical(i, 3),
        blk8_bd=bd_mask(blk2[0]), joins_bd=[bd_mask(m) for m in joins],
        tri=jnp.where(cc <= rr, 1.0, 0.0).astype(BF16),
        m0=m0, m1=m1, m0_2=m0_2, m1_2=m1_2,
        bd=bd, bd_bf=jnp.where(bd, 1.0, 0.0).astype(BF16))


def _wkv_kernel(r_ref, lw_ref, k_ref, v_ref, kk_ref, a_ref, c1_ref, og_ref, gn_ref,
                x_ref, wo_ref, ln_ref, wgu_ref, wdn_ref, o_ref, obf_ref, wgu_bf_ref, wdn_bf_ref,
                z_ref, *, n_chunks, n_pairs, alpha):
    @pl.when(pl.program_id(1) == 0)
    def _():
        z_ref[...] = jnp.zeros_like(z_ref)

    wgu_bf_ref[...] = wgu_ref[...].astype(BF16)
    wdn_bf_ref[...] = wdn_ref[...].astype(BF16)

    def chunks(ref, dtype=F32):
        return [ref[0, ci * CHUNK:(ci + 1) * CHUNK, p * LANES:(p + 1) * LANES].astype(dtype)
                for p in range(n_pairs) for ci in range(n_chunks)]

    ys, states = _wkv_block(chunks(r_ref), chunks(lw_ref), chunks(k_ref), chunks(v_ref, BF16),
                            chunks(kk_ref), chunks(a_ref), [z_ref[p] for p in range(n_pairs)],
                            _wkv_consts(), n_chunks)
    ones = _head_ones(LANES)
    inv_n = 1.0 / HEAD_SIZE
    y = jnp.concatenate(ys, axis=0)
    mu = _mm(y, ones) * inv_n
    yc = y - mu
    var = _mm(yc * yc, ones) * inv_n
    yn = yc * lax.rsqrt(var + GN_EPS)
    tb = n_chunks * CHUNK
    merged = []
    for p in range(n_pairs):
        z_ref[p] = states[p]
        ls = slice(p * LANES, (p + 1) * LANES)
        yp = yn[p * tb:(p + 1) * tb] * gn_ref[0:1, ls] + gn_ref[1:2, ls]
        merged.append((c1_ref[0, :, ls].astype(F32)
                       + og_ref[0, :, ls].astype(F32) * yp).astype(BF16))
    h = alpha * x_ref[0] + jnp.dot(jnp.concatenate(merged, axis=1), wo_ref[...],
                                   preferred_element_type=F32)
    x1 = _layer_norm(h, ln_ref[0:1, :], ln_ref[1:2, :])
    o_ref[0] = x1
    obf_ref[0] = x1.astype(BF16)


def _layer_norm(h, g, b):
    mu = jnp.mean(h, axis=-1, keepdims=True)
    hc = h - mu
    var = jnp.mean(hc * hc, axis=-1, keepdims=True)
    return hc * lax.rsqrt(var + LN_EPS) * g + b


def _cast_spec(w, n_blocks_t, n_steps):
    rows, cols = w.shape
    nb = max(n for n in range(1, n_steps + 1)
             if rows % n == 0 and (rows // n) % BF16_ROWS == 0)
    return pl.BlockSpec((rows // nb, cols),
                        lambda b, i: (lax.div((b * n_blocks_t + i) * nb, n_steps), 0))


def _wkv(r, lw, k, v, kk, a, c1, og, gn, x, wo, ln, w_gu, w_down, *, alpha, tb):
    bsz, t, dm = r.shape
    n_pairs = dm // LANES
    n_blocks_t = t // tb
    spec = pl.BlockSpec((1, tb, dm), lambda b, i: (b, i, 0))
    row_spec = pl.BlockSpec((SUBLANES, dm), lambda b, i: (0, 0))
    cast_specs = [_cast_spec(w, n_blocks_t, bsz * n_blocks_t) for w in (w_gu, w_down)]
    return pl.pallas_call(
        functools.partial(_wkv_kernel, n_chunks=tb // CHUNK, n_pairs=n_pairs, alpha=alpha),
        grid=(bsz, n_blocks_t),
        in_specs=[spec] * 8 + [row_spec, spec, pl.BlockSpec((dm, dm), lambda b, i: (0, 0)),
                               row_spec] + cast_specs,
        out_specs=[spec, spec] + cast_specs,
        out_shape=[jax.ShapeDtypeStruct((bsz, t, dm), F32),
                   jax.ShapeDtypeStruct((bsz, t, dm), BF16),
                   jax.ShapeDtypeStruct(w_gu.shape, BF16),
                   jax.ShapeDtypeStruct(w_down.shape, BF16)],
        scratch_shapes=[pltpu.VMEM((n_pairs, LANES, LANES), F32)],
        compiler_params=pltpu.CompilerParams(
            dimension_semantics=("arbitrary", "arbitrary"),
            vmem_limit_bytes=VMEM_LIMIT_BYTES),
        name="wkv_out",
    )(r, lw, k, v, kk, a, c1, og, gn, x, wo, ln, w_gu, w_down)


def _ffn_kernel(xb_ref, x_ref, wg_ref, wu_ref, wd_ref, ln_ref, o_ref, acc_ref, *, alpha):
    f = pl.program_id(1)

    @pl.when(jnp.logical_and(pl.program_id(0) == 0, f == 0))
    def _():
        acc_ref[...] = jnp.zeros_like(acc_ref)

    xb = xb_ref[...]
    tf = wg_ref.shape[1]
    half = min(tf, 2 * LANES)
    hids = []
    for c0 in range(0, tf, half):
        gate = jnp.dot(xb, wg_ref[:, c0:c0 + half], preferred_element_type=F32)
        up = jnp.dot(xb, wu_ref[:, c0:c0 + half], preferred_element_type=F32)
        hids.append((gate * jax.nn.sigmoid(gate) * up).astype(BF16))
    hid = hids[0] if len(hids) == 1 else jnp.concatenate(hids, axis=1)
    part = jnp.dot(hid, wd_ref[...], preferred_element_type=F32)
    acc_ref[...] += part

    @pl.when(f == pl.num_programs(1) - 1)
    def _():
        h = alpha * x_ref[...] + acc_ref[...]
        o_ref[...] = _layer_norm(h, ln_ref[0:1, :], ln_ref[1:2, :])
        acc_ref[...] = jnp.zeros_like(acc_ref)


def _ffn(x_bf, x2d, w_gu, w_down, ln, *, alpha, tm, tf):
    m, dm = x2d.shape
    dff = w_down.shape[0]
    nf = dff // tf
    return pl.pallas_call(
        functools.partial(_ffn_kernel, alpha=alpha),
        grid=(m // tm, nf),
        in_specs=[
            pl.BlockSpec((tm, dm), lambda i, f: (i, 0)),
            pl.BlockSpec((tm, dm), lambda i, f: (i, 0)),
            pl.BlockSpec((dm, tf), lambda i, f: (0, f)),
            pl.BlockSpec((dm, tf), lambda i, f: (0, nf + f)),
            pl.BlockSpec((tf, dm), lambda i, f: (f, 0)),
            pl.BlockSpec((SUBLANES, dm), lambda i, f: (0, 0)),
        ],
        out_specs=pl.BlockSpec((tm, dm), lambda i, f: (i, 0)),
        out_shape=jax.ShapeDtypeStruct((m, dm), F32),
        scratch_shapes=[pltpu.VMEM((tm, dm), F32)],
        compiler_params=pltpu.CompilerParams(
            dimension_semantics=("arbitrary", "arbitrary"), vmem_limit_bytes=VMEM_LIMIT_BYTES),
        name="ffn",
    )(x_bf, x2d, w_gu, w_gu, w_down, ln)


def _pad_rows(a, n):
    return jnp.pad(a, ((0, n - a.shape[0]), (0, 0)))


def _pick(n, candidates):
    for c in candidates:
        if n % c == 0:
            return c
    raise ValueError(f"no tile size for extent {n}")


def kernel(x, w_in, shift_mu, conv_w, w0, w_up, a0, a_up, g_up, k_k, k_a, r_k,
           gn_g, gn_b, w_o, ln1_g, ln1_b, w_gu, w_down, ln2_g, ln2_b):
    bsz, t, dm = x.shape
    depth = w_in.shape[0]
    lw_n, la_n, lg_n = w_up.shape[1], a_up.shape[1], g_up.shape[1]
    assert dm % (2 * LANES) == 0 and t % CHUNK == 0
    assert lw_n <= LANES and la_n <= LANES and lg_n <= 2 * LANES
    assert w_in.shape[2] == 8 * dm + lw_n + la_n + lg_n
    alpha = (2.0 * depth) ** 0.25
    tn = 2 * LANES
    tm_in = _pick(t, (512, 256, 128, 64))
    tb = _pick(t, (128, 64))
    m = bsz * t
    tm = _pick(m, (512, 256, 128, 64))
    tf = _pick(w_down.shape[1], (512, 256, 128))

    for l in range(depth):
        wi = w_in[l]
        c0 = 6 * dm
        wi_bf = wi.astype(BF16)
        g0 = c0 + lw_n + la_n + lg_n
        nj = dm // tn
        w_groups = [(wi_bf, 0), (wi_bf, nj), (wi_bf, 2 * nj), (wi_bf[:, g0:g0 + dm], 0),
                    (wi_bf, 3 * nj), (wi_bf, 4 * nj), (wi_bf, 5 * nj), (wi_bf[:, g0 + dm:], 0)]

        def lane_pad(a, n):
            return jnp.pad(a, ((0, 0), (0, n - a.shape[1])))

        wl = jnp.concatenate([
            lane_pad(wi[:, c0:c0 + lw_n], LANES),
            lane_pad(wi[:, c0 + lw_n:c0 + lw_n + la_n], LANES),
            lane_pad(wi[:, c0 + lw_n + la_n:c0 + lw_n + la_n + lg_n], 2 * LANES)],
            axis=1).astype(BF16)
        mu = shift_mu[l]
        s0 = 3 * dm
        mu_l = jnp.concatenate([
            jnp.pad(mu[s0:s0 + lw_n], (0, LANES - lw_n)),
            jnp.pad(mu[s0 + lw_n:s0 + lw_n + la_n], (0, LANES - la_n)),
            jnp.pad(mu[s0 + lw_n + la_n:], (0, 2 * LANES - lg_n))])[None, :]
        wup = jnp.concatenate([_pad_rows(w_up[l], LANES), _pad_rows(a_up[l], LANES),
                               _pad_rows(g_up[l], 2 * LANES)], axis=0).astype(BF16)
        prm = _pad_rows(jnp.stack([
            mu[0:dm], mu[dm:2 * dm], mu[2 * dm:3 * dm],
            conv_w[l, 0], conv_w[l, 1], conv_w[l, 2],
            w0[l], a0[l], k_k[l], k_a[l], r_k[l].reshape(-1)]), PARAM_ROWS)

        r, lw, k, v, kk, a, c1, og = _mixer_in(
            x, w_groups, wl, wup, prm, mu_l, tm=tm_in, tn=tn)
        gn = _pad_rows(jnp.stack([gn_g[l], gn_b[l]]), SUBLANES)
        ln1 = _pad_rows(jnp.stack([ln1_g[l], ln1_b[l]]), SUBLANES)
        x1, x1_bf, w_gu_bf, w_down_bf = _wkv(
            r, lw, k, v, kk, a, c1, og, gn, x, w_o[l].astype(BF16), ln1, w_gu[l], w_down[l],
            alpha=alpha, tb=tb)
        ln2 = _pad_rows(jnp.stack([ln2_g[l], ln2_b[l]]), SUBLANES)
        x = _ffn(x1_bf.reshape(m, dm), x1.reshape(m, dm), w_gu_bf, w_down_bf, ln2,
                 alpha=alpha, tm=tm, tf=tf).reshape(bsz, t, dm)
    return x
```
